```python
import jax, jax.numpy as jnp
from jax import lax
import numpy as np

D_MODEL = 1024
BATCH = 8
SEQ = 4096
DEPTH = 1

GRID_W = 64
CTX_LEN = 256
D_MIX = D_MODEL
D_ATTN = D_MIX // 2
D_POOL = D_MIX - D_ATTN
MLA_HEADS = 4
QK_NOPE_DIM = 128
QK_ROPE_DIM = 64
QK_HEAD_DIM = QK_NOPE_DIM + QK_ROPE_DIM
V_HEAD_DIM = D_ATTN // MLA_HEADS
Q_LORA_RANK = 256
KV_LORA_RANK = 128
ROPE_AXIS_DIM = QK_ROPE_DIM // 2
ROPE_BASE = 10000.0
POOL_WINDOWS = (2, 4, 8, 16)
POOL_GROUPS = len(POOL_WINDOWS)
POOL_GROUP_DIM = D_POOL // POOL_GROUPS
Q_BLOCK = 128
NORM_EPS = 1e-6

OFF_CQ = 0
OFF_CKV = OFF_CQ + Q_LORA_RANK
OFF_KR = OFF_CKV + KV_LORA_RANK
OFF_GA = OFF_KR + QK_ROPE_DIM
OFF_PIN = OFF_GA + D_ATTN
OFF_GP = OFF_PIN + D_POOL
D_IN_PROJ = OFF_GP + D_POOL

kernel_name = "hymba_mla_pool_adaln_prefix"


def _rms(x, g):
    xf = x.astype(jnp.float32)
    y = xf * lax.rsqrt(jnp.mean(xf * xf, axis=-1, keepdims=True) + NORM_EPS)
    return (y * g.astype(jnp.float32)).astype(x.dtype)


def _rotate_half(x):
    x1, x2 = jnp.split(x, 2, axis=-1)
    return jnp.concatenate([-x2, x1], axis=-1)


def _axial_rope_tables(L):
    rows = L // GRID_W
    row = jnp.repeat(jnp.arange(rows, dtype=jnp.float32), GRID_W)
    col = jnp.tile(jnp.arange(GRID_W, dtype=jnp.float32), rows)
    n_freq = ROPE_AXIS_DIM // 2
    inv = ROPE_BASE ** (-jnp.arange(n_freq, dtype=jnp.float32) / n_freq)
    ang_r = row[:, None] * inv
    ang_c = col[:, None] * inv
    ang = jnp.concatenate([ang_r, ang_r, ang_c, ang_c], axis=-1)
    return jnp.cos(ang), jnp.sin(ang)


def _apply_axial_rope(x, cos, sin):
    xr, xc = jnp.split(x, 2, axis=-1)
    rot = jnp.concatenate([_rotate_half(xr), _rotate_half(xc)], axis=-1)
    c = cos[:, None, :]
    s = sin[:, None, :]
    return (x.astype(jnp.float32) * c + rot.astype(jnp.float32) * s).astype(x.dtype)


def _mla_qkv(u, q_lora_g, w_uq, kv_lora_g, w_ukv, q_norm_g, k_norm_g, rope):
    B, L = u.shape[0], u.shape[1]
    cq = u[..., OFF_CQ:OFF_CKV]
    ckv = u[..., OFF_CKV:OFF_KR]
    k_rope = u[..., OFF_KR:OFF_GA]
    q = (_rms(cq, q_lora_g) @ w_uq).reshape(B, L, MLA_HEADS, QK_HEAD_DIM)
    kv = (_rms(ckv, kv_lora_g) @ w_ukv).reshape(B, L, MLA_HEADS, QK_NOPE_DIM + V_HEAD_DIM)
    k_nope, v = kv[..., :QK_NOPE_DIM], kv[..., QK_NOPE_DIM:]
    k = jnp.concatenate(
        [k_nope, jnp.broadcast_to(k_rope[:, :, None, :], (B, L, MLA_HEADS, QK_ROPE_DIM))], axis=-1)
    q = _rms(q, q_norm_g)
    k = _rms(k, k_norm_g)
    if rope is not None:
        cos, sin = rope
        q = jnp.concatenate([q[..., :QK_NOPE_DIM], _apply_axial_rope(q[..., QK_NOPE_DIM:], cos, sin)], axis=-1)
        k = jnp.concatenate([k[..., :QK_NOPE_DIM], _apply_axial_rope(k[..., QK_NOPE_DIM:], cos, sin)], axis=-1)
    tr = lambda t: jnp.transpose(t, (0, 2, 1, 3))
    return tr(q), tr(k), tr(v)


def _attend(q, k, v):
    B, H, Lq, dk = q.shape
    nb = Lq // Q_BLOCK
    scale = QK_HEAD_DIM ** -0.5
    qb = jnp.transpose(q.reshape(B, H, nb, Q_BLOCK, dk), (2, 0, 1, 3, 4))

    def one(qblk):
        s = jnp.einsum('bhqd,bhkd->bhqk', qblk, k).astype(jnp.float32) * scale
        p = jax.nn.softmax(s, axis=-1)
        return jnp.einsum('bhqk,bhkd->bhqd', p.astype(v.dtype), v)

    o = lax.map(one, qb)
    o = jnp.transpose(o, (1, 3, 0, 2, 4)).reshape(B, Lq, H * v.shape[-1])
    return o


def _multiscale_pool(u, w_pool, pool_scale):
    B, L, _ = u.shape
    ug = u.reshape(B, L, POOL_GROUPS, POOL_GROUP_DIM)
    cs = jnp.concatenate(
        [jnp.zeros((B, 1, POOL_GROUPS, POOL_GROUP_DIM), jnp.float32),
         jnp.cumsum(ug.astype(jnp.float32), axis=1)], axis=1)
    t = jnp.arange(L, dtype=jnp.int32)[:, None]
    w = jnp.array(POOL_WINDOWS, dtype=jnp.int32)[None, :]
    lo = jnp.clip(t - w // 2, 0, L)
    hi = jnp.clip(t - w // 2 + w, 0, L)
    g = jnp.arange(POOL_GROUPS, dtype=jnp.int32)[None, :]
    win_sum = cs[:, hi, g, :] - cs[:, lo, g, :]
    cnt = (hi - lo).astype(jnp.float32)[None, :, :, None]
    pooled = (win_sum / cnt - ug.astype(jnp.float32)).astype(u.dtype)
    y = jnp.einsum('blgc,gcd->blgd', pooled, w_pool).reshape(B, L, D_POOL)
    return y * pool_scale


def _branches(u, mla_out, w_pool, pool_scale):
    gate_a = u[..., OFF_GA:OFF_PIN]
    pool_in = u[..., OFF_PIN:OFF_GP]
    gate_p = u[..., OFF_GP:D_IN_PROJ]
    br_a = jax.nn.silu(gate_a) * mla_out
    br_p = jax.nn.silu(gate_p) * _multiscale_pool(pool_in, w_pool, pool_scale)
    return jnp.concatenate([br_a, br_p], axis=-1)


def setup_inputs(seed: int = 0) -> dict:
    key = jax.random.key(seed)
    ks = jax.random.split(key, 20)
    f32 = jnp.float32
    nrm = lambda k, shape, s: jax.random.normal(k, shape, f32) * s
    return {
        "x": nrm(ks[0], (BATCH, SEQ, D_MODEL), 1.0),
        "c": nrm(ks[1], (BATCH, D_MODEL), 1.0),
        "ctx": nrm(ks[2], (BATCH, CTX_LEN, D_MODEL), 1.0),
        "c_ctx": nrm(ks[3], (D_MODEL,), 1.0),
        "w_mod": nrm(ks[4], (DEPTH, D_MODEL, 3 * D_MODEL), 0.5 * D_MODEL ** -0.5),
        "b_mod": nrm(ks[5], (DEPTH, 3 * D_MODEL), 0.02),
        "norm_g": 1.0 + nrm(ks[6], (DEPTH, D_MODEL), 0.1),
        "w_in": nrm(ks[7], (DEPTH, D_MODEL, D_IN_PROJ), D_MODEL ** -0.5),
        "q_lora_g": 1.0 + nrm(ks[8], (DEPTH, Q_LORA_RANK), 0.1),
        "w_uq": nrm(ks[9], (DEPTH, Q_LORA_RANK, MLA_HEADS * QK_HEAD_DIM), Q_LORA_RANK ** -0.5),
        "kv_lora_g": 1.0 + nrm(ks[10], (DEPTH, KV_LORA_RANK), 0.1),
        "w_ukv": nrm(ks[11], (DEPTH, KV_LORA_RANK, MLA_HEADS * (QK_NOPE_DIM + V_HEAD_DIM)), KV_LORA_RANK ** -0.5),
        "q_norm_g": 1.0 + nrm(ks[12], (DEPTH, QK_HEAD_DIM), 0.1),
        "k_norm_g": 1.0 + nrm(ks[13], (DEPTH, QK_HEAD_DIM), 0.1),
        "w_pool": nrm(ks[14], (DEPTH, POOL_GROUPS, POOL_GROUP_DIM, POOL_GROUP_DIM), POOL_GROUP_DIM ** -0.5),
        "pool_scale": 1.0 + nrm(ks[15], (DEPTH, D_POOL), 0.1),
        "w_out": nrm(ks[16], (DEPTH, D_MIX, D_MODEL), D_MIX ** -0.5),
    }


def reference(x, c, ctx, c_ctx, w_mod, b_mod, norm_g, w_in, q_lora_g, w_uq, kv_lora_g, w_ukv,
              q_norm_g, k_norm_g, w_pool, pool_scale, w_out):
    L = x.shape[1]
    rope = _axial_rope_tables(L)
    for l in range(DEPTH):
        mod = jax.nn.silu(c) @ w_mod[l] + b_mod[l]
        shift, scale, gate = jnp.split(mod, 3, axis=-1)
        mod_c = jax.nn.silu(c_ctx) @ w_mod[l] + b_mod[l]
        shift_c, scale_c, gate_c = jnp.split(mod_c, 3, axis=-1)

        h = _rms(x, norm_g[l]) * (1.0 + scale[:, None, :]) + shift[:, None, :]
        hc = _rms(ctx, norm_g[l]) * (1.0 + scale_c) + shift_c
        u = h @ w_in[l]
        uc = hc @ w_in[l]

        q, k, v = _mla_qkv(u, q_lora_g[l], w_uq[l], kv_lora_g[l], w_ukv[l],
                           q_norm_g[l], k_norm_g[l], rope)
        qc, kc, vc = _mla_qkv(uc, q_lora_g[l], w_uq[l], kv_lora_g[l], w_ukv[l],
                              q_norm_g[l], k_norm_g[l], None)
        k_all = jnp.concatenate([kc, k], axis=2)
        v_all = jnp.concatenate([vc, v], axis=2)
        attn = _attend(q, k_all, v_all)
        y = _branches(u, attn, w_pool[l], pool_scale[l]) @ w_out[l]
        x_new = x + gate[:, None, :] * y

        if l < DEPTH - 1:
            attn_c = _attend(qc, kc, vc)
            yc = _branches(uc, attn_c, w_pool[l], pool_scale[l]) @ w_out[l]
            ctx = ctx + gate_c * yc
        x = x_new
    return x
```

```python
import functools
import math

import jax
import jax.numpy as jnp
import numpy as np
from jax import lax
from jax.experimental import pallas as pl
from jax.experimental.pallas import tpu as pltpu

F32 = jnp.float32
BF16 = jnp.bfloat16

GRID_W = 64
MLA_HEADS = 4
QK_NOPE_DIM = 128
QK_ROPE_DIM = 64
QK_HEAD_DIM = QK_NOPE_DIM + QK_ROPE_DIM
V_HEAD_DIM = 128
Q_LORA_RANK = 256
KV_LORA_RANK = 128
ROPE_BASE = 10000.0
POOL_WINDOWS = (2, 4, 8, 16)
POOL_GROUP_DIM = 128
NORM_EPS = 1e-6
Q_BLOCK = 128
D_ATTN = MLA_HEADS * V_HEAD_DIM
D_POOL = len(POOL_WINDOWS) * POOL_GROUP_DIM

LANES = 128
HEAD_COLS = 2 * LANES
HALO = 16
MOD_ROWS = 16

C_CQ = 0
C_CKV = C_CQ + Q_LORA_RANK
C_KR = C_CKV + KV_LORA_RANK
C_GA = C_KR + 2 * QK_ROPE_DIM
C_PIN = C_GA + D_ATTN
C_GP = C_PIN + D_POOL
C_END = C_GP + D_POOL

VMEM_LIMIT = 56 * 1024 * 1024

Q_SCALE = QK_HEAD_DIM ** -0.5 * math.log2(math.e)


def _silu(x):
    return x / (1.0 + jnp.exp(-x))


def _rms_rows(x, g):
    ms = jnp.mean(x * x, axis=-1, keepdims=True)
    return x * lax.rsqrt(ms + NORM_EPS) * g


def _mod_kernel(cc_ref, w_ref, b_ref, o_ref):
    a = _silu(cc_ref[...])
    o_ref[...] = jnp.dot(a, w_ref[...], preferred_element_type=F32,
                         precision=lax.Precision.HIGHEST) + b_ref[...]


def _modulation(cc, w_mod, b_mod):
    d = w_mod.shape[0]
    n = w_mod.shape[1]
    tn = 768
    return pl.pallas_call(
        _mod_kernel,
        grid=(n // tn,),
        in_specs=[
            pl.BlockSpec((MOD_ROWS, d), lambda j: (0, 0)),
            pl.BlockSpec((d, tn), lambda j: (0, j)),
            pl.BlockSpec((1, tn), lambda j: (0, j)),
        ],
        out_specs=pl.BlockSpec((MOD_ROWS, tn), lambda j: (0, j)),
        out_shape=jax.ShapeDtypeStruct((MOD_ROWS, n), F32),
        name="mod",
    )(cc, w_mod, b_mod)


def _low_mask():
    lane = lax.broadcasted_iota(jnp.int32, (1, LANES), 1)
    return (lane < QK_ROPE_DIM).astype(F32)


def _rope_pair(rr, g_rr, cs):
    pr = rr * g_rr * cs
    return pr + pltpu.roll(pr, QK_ROPE_DIM, axis=1)


def _kv_heads(ckv, kr2, kvlg, w_ukv, gk_ref, cs, k_ref, v_ref):
    low = _low_mask()
    ckv_n = _rms_rows(ckv, kvlg).astype(BF16)
    kv = jnp.dot(ckv_n, w_ukv, preferred_element_type=F32)
    g_nope = gk_ref[:, 0:LANES]
    g_rr = gk_ref[:, LANES:2 * LANES]
    k_rope = _rope_pair(kr2, g_rr, cs)
    ss_rope = jnp.sum(kr2 * kr2 * low, axis=-1, keepdims=True)
    for h in range(MLA_HEADS):
        kn = kv[:, h * HEAD_COLS:h * HEAD_COLS + LANES]
        vv = kv[:, h * HEAD_COLS + LANES:(h + 1) * HEAD_COLS]
        ss = jnp.sum(kn * kn, axis=-1, keepdims=True) + ss_rope
        r = lax.rsqrt(ss * (1.0 / QK_HEAD_DIM) + NORM_EPS)
        k_ref[0, h, :, 0:LANES] = (kn * r * g_nope).astype(BF16)
        k_ref[0, h, :, LANES:QK_HEAD_DIM] = (k_rope * r)[:, 0:QK_ROPE_DIM].astype(BF16)
        v_ref[0, h] = vv.astype(BF16)


def _proj_kernel(x_ref, xp_ref, xn_ref, mod_ref, ng_ref, w_in_ref, qlg_ref, w_uq_ref,
                 kvlg_ref, w_ukv_ref, gq_ref, gk_ref, cs_ref, w_pool_ref, ps_ref,
                 q_ref, k_ref, v_ref, sga_ref, brp_ref,
                 h_scr, pin_scr, *, tm, seq):
    t = pl.program_id(1)
    nt = pl.num_programs(1)
    d = x_ref.shape[-1]
    shift = mod_ref[0, :, 0:d]
    scale = mod_ref[0, :, d:2 * d]
    a = ng_ref[...] * (1.0 + scale)

    def hnorm(xv):
        ms = jnp.mean(xv * xv, axis=-1, keepdims=True)
        return xv * lax.rsqrt(ms + NORM_EPS) * a + shift

    h_scr[0:HALO, :] = jnp.where(t > 0, hnorm(xp_ref[0]), 0.0).astype(BF16)
    h_scr[HALO:HALO + tm, :] = hnorm(x_ref[0]).astype(BF16)
    h_scr[HALO + tm:, :] = jnp.where(t < nt - 1, hnorm(xn_ref[0]), 0.0).astype(BF16)

    h_main = h_scr[HALO:HALO + tm, :]
    cs = cs_ref[...]
    low = _low_mask()

    u_a = jnp.dot(h_main, w_in_ref[:, C_CQ:C_GA], preferred_element_type=F32)
    cq_n = _rms_rows(u_a[:, C_CQ:C_CKV], qlg_ref[...]).astype(BF16)
    qa = jnp.dot(cq_n, w_uq_ref[...], preferred_element_type=F32)
    gq_nope = gq_ref[:, 0:LANES]
    gq_rr = gq_ref[:, LANES:2 * LANES]
    for h in range(MLA_HEADS):
        qn = qa[:, h * HEAD_COLS:h * HEAD_COLS + LANES]
        rr = qa[:, h * HEAD_COLS + LANES:(h + 1) * HEAD_COLS]
        ss = (jnp.sum(qn * qn, axis=-1, keepdims=True)
              + jnp.sum(rr * rr * low, axis=-1, keepdims=True))
        r = lax.rsqrt(ss * (1.0 / QK_HEAD_DIM) + NORM_EPS) * Q_SCALE
        q_ref[0, h, :, 0:LANES] = (qn * r * gq_nope).astype(BF16)
        q_rope = _rope_pair(rr, gq_rr, cs) * r
        q_ref[0, h, :, LANES:QK_HEAD_DIM] = q_rope[:, 0:QK_ROPE_DIM].astype(BF16)

    _kv_heads(u_a[:, C_CKV:C_KR], u_a[:, C_KR:C_GA], kvlg_ref[...], w_ukv_ref[...],
              gk_ref, cs, k_ref, v_ref)

    u_ga = jnp.dot(h_main, w_in_ref[:, C_GA:C_PIN], preferred_element_type=F32)
    sga_ref[0] = _silu(u_ga).astype(BF16)

    pin_scr[...] = jnp.dot(h_scr[...], w_in_ref[:, C_PIN:C_GP], preferred_element_type=F32)
    u_gp = jnp.dot(h_main, w_in_ref[:, C_GP:C_END], preferred_element_type=F32)
    tok = t * tm + lax.broadcasted_iota(jnp.int32, (tm, LANES), 0)
    for g, w in enumerate(POOL_WINDOWS):
        c0 = g * POOL_GROUP_DIM
        c1 = c0 + POOL_GROUP_DIM
        start = HALO - w // 2
        ws = pin_scr[start:start + tm, c0:c1]
        for j in range(1, w):
            ws = ws + pin_scr[start + j:start + j + tm, c0:c1]
        lo = jnp.maximum(tok - w // 2, 0)
        hi = jnp.minimum(tok - w // 2 + w, seq)
        cnt = (hi - lo).astype(F32)
        pooled = ws / cnt - pin_scr[HALO:HALO + tm, c0:c1]
        y = jnp.dot(pooled.astype(BF16), w_pool_ref[g], preferred_element_type=F32)
        brp_ref[0, :, c0:c1] = (_silu(u_gp[:, c0:c1]) * (y * ps_ref[:, c0:c1])).astype(BF16)


def _proj(x, mod3, norm_g, w_in_r, qlg, w_uq_r, kvlg, w_ukv_b, gq, gk, cs, w_pool_b, ps, *, tm):
    b, s, d = x.shape
    nt = s // tm
    hb = tm // HALO
    nhb = s // HALO
    const2 = lambda bi, ti: (0, 0)
    kern = functools.partial(_proj_kernel, tm=tm, seq=s)
    return pl.pallas_call(
        kern,
        grid=(b, nt),
        in_specs=[
            pl.BlockSpec((1, tm, d), lambda bi, ti: (bi, ti, 0)),
            pl.BlockSpec((1, HALO, d), lambda bi, ti: (bi, jnp.maximum(ti * hb - 1, 0), 0)),
            pl.BlockSpec((1, HALO, d), lambda bi, ti: (bi, jnp.minimum((ti + 1) * hb, nhb - 1), 0)),
            pl.BlockSpec((1, 1, 3 * d), lambda bi, ti: (bi, 0, 0)),
            pl.BlockSpec((1, d), const2),
            pl.BlockSpec((d, C_END), const2),
            pl.BlockSpec((1, Q_LORA_RANK), const2),
            pl.BlockSpec((Q_LORA_RANK, MLA_HEADS * HEAD_COLS), const2),
            pl.BlockSpec((1, KV_LORA_RANK), const2),
            pl.BlockSpec((KV_LORA_RANK, MLA_HEADS * HEAD_COLS), const2),
            pl.BlockSpec((1, 2 * LANES), const2),
            pl.BlockSpec((1, 2 * LANES), const2),
            pl.BlockSpec((tm, LANES), lambda bi, ti: (ti, 0)),
            pl.BlockSpec((len(POOL_WINDOWS), POOL_GROUP_DIM, POOL_GROUP_DIM), lambda bi, ti: (0, 0, 0)),
            pl.BlockSpec((1, D_POOL), const2),
        ],
        out_specs=[
            pl.BlockSpec((1, MLA_HEADS, tm, QK_HEAD_DIM), lambda bi, ti: (bi, 0, ti, 0)),
            pl.BlockSpec((1, MLA_HEADS, tm, QK_HEAD_DIM), lambda bi, ti: (bi, 0, ti, 0)),
            pl.BlockSpec((1, MLA_HEADS, tm, V_HEAD_DIM), lambda bi, ti: (bi, 0, ti, 0)),
            pl.BlockSpec((1, tm, D_ATTN), lambda bi, ti: (bi, ti, 0)),
            pl.BlockSpec((1, tm, D_POOL), lambda bi, ti: (bi, ti, 0)),
        ],
        out_shape=[
            jax.ShapeDtypeStruct((b, MLA_HEADS, s, QK_HEAD_DIM), BF16),
            jax.ShapeDtypeStruct((b, MLA_HEADS, s, QK_HEAD_DIM), BF16),
            jax.ShapeDtypeStruct((b, MLA_HEADS, s, V_HEAD_DIM), BF16),
            jax.ShapeDtypeStruct((b, s, D_ATTN), BF16),
            jax.ShapeDtypeStruct((b, s, D_POOL), BF16),
        ],
        scratch_shapes=[
            pltpu.VMEM((tm + 2 * HALO, d), BF16),
            pltpu.VMEM((tm + 2 * HALO, D_POOL), F32),
        ],
        compiler_params=pltpu.CompilerParams(
            dimension_semantics=("parallel", "parallel"),
            vmem_limit_bytes=VMEM_LIMIT),
        name="proj",
    )(x, x, x, mod3, norm_g, w_in_r, qlg, w_uq_r, kvlg, w_ukv_b, gq, gk, cs, w_pool_b, ps)


def _ctxkv_kernel(x_ref, mod_ref, ng_ref, w_kv_ref, kvlg_ref, w_ukv_ref, gk_ref, cs_ref,
                  k_ref, v_ref):
    d = x_ref.shape[-1]
    shift = mod_ref[0, :, 0:d]
    scale = mod_ref[0, :, d:2 * d]
    a = ng_ref[...] * (1.0 + scale)
    xv = x_ref[0]
    ms = jnp.mean(xv * xv, axis=-1, keepdims=True)
    hc = (xv * lax.rsqrt(ms + NORM_EPS) * a + shift).astype(BF16)
    u = jnp.dot(hc, w_kv_ref[...], preferred_element_type=F32)
    _kv_heads(u[:, 0:KV_LORA_RANK], u[:, KV_LORA_RANK:], kvlg_ref[...], w_ukv_ref[...],
              gk_ref, cs_ref[...], k_ref, v_ref)


def _ctxkv(ctx, mod3, norm_g, w_kv, kvlg, w_ukv_b, gk, cs_ctx):
    b, lc, d = ctx.shape
    const2 = lambda bi: (0, 0)
    nkv = KV_LORA_RANK + 2 * QK_ROPE_DIM
    return pl.pallas_call(
        _ctxkv_kernel,
        grid=(b,),
        in_specs=[
            pl.BlockSpec((1, lc, d), lambda bi: (bi, 0, 0)),
            pl.BlockSpec((1, 1, 3 * d), lambda bi: (b, 0, 0)),
            pl.BlockSpec((1, d), const2),
            pl.BlockSpec((d, nkv), const2),
            pl.BlockSpec((1, KV_LORA_RANK), const2),
            pl.BlockSpec((KV_LORA_RANK, MLA_HEADS * HEAD_COLS), const2),
            pl.BlockSpec((1, 2 * LANES), const2),
            pl.BlockSpec((lc, LANES), const2),
        ],
        out_specs=[
            pl.BlockSpec((1, MLA_HEADS, lc, QK_HEAD_DIM), lambda bi: (bi, 0, 0, 0)),
            pl.BlockSpec((1, MLA_HEADS, lc, V_HEAD_DIM), lambda bi: (bi, 0, 0, 0)),
        ],
        out_shape=[
            jax.ShapeDtypeStruct((b, MLA_HEADS, lc, QK_HEAD_DIM), BF16),
            jax.ShapeDtypeStruct((b, MLA_HEADS, lc, V_HEAD_DIM), BF16),
        ],
        compiler_params=pltpu.CompilerParams(
            dimension_semantics=("parallel",), vmem_limit_bytes=VMEM_LIMIT),
        name="ctxkv",
    )(ctx, mod3, norm_g, w_kv, kvlg, w_ukv_b, gk, cs_ctx)


def _attn_kernel(q_ref, k_ref, v_ref, kc_ref, vc_ref, o_ref, m_scr, l_scr, acc_scr, *, tk):
    nk = k_ref.shape[2] // tk
    nqb = q_ref.shape[2] // Q_BLOCK
    dv = v_ref.shape[-1]

    for h in range(MLA_HEADS):
        q = q_ref[0, h]

        def step(kb, vb):
            s = lax.dot_general(q, kb, (((1,), (1,)), ((), ())), preferred_element_type=F32)
            m_prev = m_scr[...]
            m_next = jnp.maximum(m_prev, jnp.max(s, axis=-1, keepdims=True))
            alpha = jnp.exp2(m_prev - m_next)
            p = jnp.exp2(s - pltpu.repeat(m_next, s.shape[1] // LANES, 1))
            l_scr[...] = alpha * l_scr[...] + jnp.sum(p, axis=-1, keepdims=True)
            acc_scr[...] = alpha * acc_scr[...] + jnp.dot(p.astype(BF16), vb,
                                                          preferred_element_type=F32)
            m_scr[...] = m_next

        m_scr[...] = jnp.full(m_scr.shape, -jnp.inf, F32)
        l_scr[...] = jnp.zeros(l_scr.shape, F32)
        acc_scr[...] = jnp.zeros(acc_scr.shape, F32)
        step(kc_ref[0, h], vc_ref[0, h])

        def body(j, carry):
            off = pl.multiple_of(j * tk, tk)
            step(k_ref[0, h, pl.ds(off, tk), :], v_ref[0, h, pl.ds(off, tk), :])
            return carry

        lax.fori_loop(0, nk, body, 0)
        res = (acc_scr[...] / l_scr[...]).astype(o_ref.dtype)
        for n in range(nqb):
            c0 = (n * MLA_HEADS + h) * dv
            o_ref[0, :, c0:c0 + dv] = res[n * Q_BLOCK:(n + 1) * Q_BLOCK, :]


def _attention(q, k, v, kc, vc, *, tq, tk):
    b, h, s, dk = q.shape
    lc = kc.shape[2]
    dv = v.shape[-1]
    kern = functools.partial(_attn_kernel, tk=tk)
    return pl.pallas_call(
        kern,
        grid=(b, s // tq),
        in_specs=[
            pl.BlockSpec((1, h, tq, dk), lambda bi, qi: (bi, 0, qi, 0)),
            pl.BlockSpec((1, h, s, dk), lambda bi, qi: (bi, 0, 0, 0)),
            pl.BlockSpec((1, h, s, dv), lambda bi, qi: (bi, 0, 0, 0)),
            pl.BlockSpec((1, h, lc, dk), lambda bi, qi: (bi, 0, 0, 0)),
            pl.BlockSpec((1, h, lc, dv), lambda bi, qi: (bi, 0, 0, 0)),
        ],
        out_specs=pl.BlockSpec((1, Q_BLOCK, (tq // Q_BLOCK) * h * dv), lambda bi, qi: (bi, 0, qi)),
        out_shape=jax.ShapeDtypeStruct((b, Q_BLOCK, (s // Q_BLOCK) * h * dv), BF16),
        scratch_shapes=[
            pltpu.VMEM((tq, LANES), F32),
            pltpu.VMEM((tq, LANES), F32),
            pltpu.VMEM((tq, dv), F32),
        ],
        compiler_params=pltpu.CompilerParams(
            dimension_semantics=("parallel", "arbitrary"),
            vmem_limit_bytes=VMEM_LIMIT),
        name="attn",
    )(q, k, v, kc, vc)


def _out_kernel(attn_ref, sga_ref, brp_ref, x_ref, mod_ref, w_ref, o_ref):
    d = x_ref.shape[-1]
    gate = mod_ref[0, :, 2 * d:3 * d]
    br_a = (sga_ref[0].astype(F32) * attn_ref[0].astype(F32)).astype(BF16)
    y = jnp.dot(br_a, w_ref[0:D_ATTN, :], preferred_element_type=F32)
    y = y + jnp.dot(brp_ref[0], w_ref[D_ATTN:, :], preferred_element_type=F32)
    o_ref[0] = x_ref[0] + gate * y


def _out_proj(attn, sga, brp, x, mod3, w_out_b, *, tm):
    b, s, d = x.shape
    return pl.pallas_call(
        _out_kernel,
        grid=(b, s // tm),
        in_specs=[
            pl.BlockSpec((1, tm, D_ATTN), lambda bi, ti: (bi, ti, 0)),
            pl.BlockSpec((1, tm, D_ATTN), lambda bi, ti: (bi, ti, 0)),
            pl.BlockSpec((1, tm, D_POOL), lambda bi, ti: (bi, ti, 0)),
            pl.BlockSpec((1, tm, d), lambda bi, ti: (bi, ti, 0)),
            pl.BlockSpec((1, 1, 3 * d), lambda bi, ti: (bi, 0, 0)),
            pl.BlockSpec((D_ATTN + D_POOL, d), lambda bi, ti: (0, 0)),
        ],
        out_specs=pl.BlockSpec((1, tm, d), lambda bi, ti: (bi, ti, 0)),
        out_shape=jax.ShapeDtypeStruct((b, s, d), F32),
        compiler_params=pltpu.CompilerParams(
            dimension_semantics=("parallel", "parallel"), vmem_limit_bytes=VMEM_LIMIT),
        name="outproj",
    )(attn, sga, brp, x, mod3, w_out_b)


def _rot_perm():
    q = QK_ROPE_DIM // 4
    perm = np.concatenate([np.arange(q, 2 * q), np.arange(0, q),
                           np.arange(3 * q, 4 * q), np.arange(2 * q, 3 * q)])
    sign = np.concatenate([-np.ones(q), np.ones(q), -np.ones(q), np.ones(q)]).astype(np.float32)
    return perm, sign


def _rope_table(seq):
    rows = seq // GRID_W
    row = jnp.repeat(jnp.arange(rows, dtype=F32), GRID_W)
    col = jnp.tile(jnp.arange(GRID_W, dtype=F32), rows)
    n_freq = QK_ROPE_DIM // 4
    inv = ROPE_BASE ** (-jnp.arange(n_freq, dtype=F32) / n_freq)
    ang_r = row[:, None] * inv
    ang_c = col[:, None] * inv
    ang = jnp.concatenate([ang_r, ang_r, ang_c, ang_c], axis=-1)
    return jnp.concatenate([jnp.cos(ang), jnp.sin(ang)], axis=-1)


def kernel(x, c, ctx, c_ctx, w_mod, b_mod, norm_g, w_in, q_lora_g, w_uq, kv_lora_g, w_ukv,
           q_norm_g, k_norm_g, w_pool, pool_scale, w_out):
    b, s, d = x.shape
    lc = ctx.shape[1]
    depth = w_mod.shape[0]
    assert depth == 1, "single-layer block"
    perm, sign = _rot_perm()
    l = 0

    cc = jnp.concatenate([c, c_ctx[None, :], jnp.zeros((MOD_ROWS - b - 1, d), F32)], axis=0)
    mod = _modulation(cc, w_mod[l], b_mod[l][None, :])
    mod3 = mod[:, None, :]

    wi = w_in[l]
    o_cq, o_ckv, o_kr = 0, Q_LORA_RANK, Q_LORA_RANK + KV_LORA_RANK
    o_ga = o_kr + QK_ROPE_DIM
    o_pin = o_ga + D_ATTN
    o_gp = o_pin + D_POOL
    w_kr = wi[:, o_kr:o_ga]
    w_in_r = jnp.concatenate(
        [wi[:, o_cq:o_kr], w_kr, w_kr[:, perm] * sign, wi[:, o_ga:o_pin],
         wi[:, o_pin:o_gp], wi[:, o_gp:]], axis=1).astype(BF16)
    wq = w_uq[l].reshape(Q_LORA_RANK, MLA_HEADS, QK_HEAD_DIM)
    wq_rope = wq[:, :, QK_NOPE_DIM:]
    w_uq_r = jnp.concatenate([wq, wq_rope[:, :, perm] * sign], axis=-1)
    w_uq_r = w_uq_r.reshape(Q_LORA_RANK, MLA_HEADS * HEAD_COLS).astype(BF16)
    w_ukv_b = w_ukv[l].astype(BF16)
    w_pool_b = w_pool[l].astype(BF16)
    w_out_b = w_out[l].astype(BF16)

    def gain_row(g):
        return jnp.concatenate([g, g[QK_NOPE_DIM:][perm]])[None, :]

    gq = gain_row(q_norm_g[l])
    gk = gain_row(k_norm_g[l])
    cs = _rope_table(s)
    cs_ctx = jnp.concatenate([jnp.ones((lc, QK_ROPE_DIM), F32),
                              jnp.zeros((lc, QK_ROPE_DIM), F32)], axis=1)

    ng = norm_g[l][None, :]
    qlg = q_lora_g[l][None, :]
    kvlg = kv_lora_g[l][None, :]
    ps = pool_scale[l][None, :]

    q, k, v, sga, brp = _proj(x, mod3, ng, w_in_r, qlg, w_uq_r, kvlg, w_ukv_b, gq, gk, cs,
                              w_pool_b, ps, tm=512)
    kc, vc = _ctxkv(ctx, mod3, ng, w_in_r[:, C_CKV:C_GA], kvlg, w_ukv_b, gk, cs_ctx)
    attn = _attention(q, k, v, kc, vc, tq=512, tk=512).reshape(b, s, D_ATTN)
    return _out_proj(attn, sga, brp, x, mod3, w_out_b, tm=512)
```

```python
import functools
import math

import jax
import jax.numpy as jnp
import numpy as np
from jax import lax
from jax.experimental import pallas as pl
from jax.experimental.pallas import tpu as pltpu

F32 = jnp.float32
BF16 = jnp.bfloat16

GRID_W = 64
MLA_HEADS = 4
QK_NOPE_DIM = 128
QK_ROPE_DIM = 64
QK_HEAD_DIM = QK_NOPE_DIM + QK_ROPE_DIM
V_HEAD_DIM = 128
V_AUG_DIM = 2 * V_HEAD_DIM
Q_LORA_RANK = 256
KV_LORA_RANK = 128
ROPE_BASE = 10000.0
POOL_WINDOWS = (2, 4, 8, 16)
POOL_GROUP_DIM = 128
NORM_EPS = 1e-6
Q_BLOCK = 128
D_ATTN = MLA_HEADS * V_HEAD_DIM
D_POOL = len(POOL_WINDOWS) * POOL_GROUP_DIM

LANES = 128
HEAD_COLS = 2 * LANES
HALO = 16
MOD_ROWS = 16

C_CQ = 0
C_CKV = C_CQ + Q_LORA_RANK
C_KR = C_CKV + KV_LORA_RANK
C_GA = C_KR + 2 * QK_ROPE_DIM
C_PIN = C_GA + D_ATTN
C_GP = C_PIN + D_POOL
C_END = C_GP + D_POOL

VMEM_LIMIT = 56 * 1024 * 1024

Q_SCALE = QK_HEAD_DIM ** -0.5 * math.log2(math.e)


def _silu(x):
    return x / (1.0 + jnp.exp(-x))


def _rms_rows(x, g):
    ms = jnp.mean(x * x, axis=-1, keepdims=True)
    return x * lax.rsqrt(ms + NORM_EPS) * g


def _mod_kernel(cc_ref, w_ref, b_ref, o_ref):
    a = _silu(cc_ref[...])
    o_ref[...] = jnp.dot(a, w_ref[...], preferred_element_type=F32,
                         precision=lax.Precision.HIGHEST) + b_ref[...]


def _modulation(cc, w_mod, b_mod):
    d = w_mod.shape[0]
    n = w_mod.shape[1]
    tn = 768
    return pl.pallas_call(
        _mod_kernel,
        grid=(n // tn,),
        in_specs=[
            pl.BlockSpec((MOD_ROWS, d), lambda j: (0, 0)),
            pl.BlockSpec((d, tn), lambda j: (0, j)),
            pl.BlockSpec((1, tn), lambda j: (0, j)),
        ],
        out_specs=pl.BlockSpec((MOD_ROWS, tn), lambda j: (0, j)),
        out_shape=jax.ShapeDtypeStruct((MOD_ROWS, n), F32),
        name="mod",
    )(cc, w_mod, b_mod)


def _low_mask():
    lane = lax.broadcasted_iota(jnp.int32, (1, LANES), 1)
    return (lane < QK_ROPE_DIM).astype(F32)


def _rope_pair(rr, g_rr, cs):
    pr = rr * g_rr * cs
    return pr + pltpu.roll(pr, QK_ROPE_DIM, axis=1)


def _kv_heads(ckv, kr2, kvlg, w_ukv, gk_ref, cs, k_ref, v_ref):
    low = _low_mask()
    ckv_n = _rms_rows(ckv, kvlg).astype(BF16)
    kv = jnp.dot(ckv_n, w_ukv, preferred_element_type=F32)
    g_nope = gk_ref[:, 0:LANES]
    g_rr = gk_ref[:, LANES:2 * LANES]
    k_rope = _rope_pair(kr2, g_rr, cs)
    ss_rope = jnp.sum(kr2 * kr2 * low, axis=-1, keepdims=True)
    for h in range(MLA_HEADS):
        kn = kv[:, h * HEAD_COLS:h * HEAD_COLS + LANES]
        vv = kv[:, h * HEAD_COLS + LANES:(h + 1) * HEAD_COLS]
        ss = jnp.sum(kn * kn, axis=-1, keepdims=True) + ss_rope
        r = lax.rsqrt(ss * (1.0 / QK_HEAD_DIM) + NORM_EPS)
        k_ref[0, h, :, 0:LANES] = (kn * r * g_nope).astype(BF16)
        k_ref[0, h, :, LANES:QK_HEAD_DIM] = (k_rope * r)[:, 0:QK_ROPE_DIM].astype(BF16)
        v_ref[0, h, :, 0:V_HEAD_DIM] = vv.astype(BF16)
        v_ref[0, h, :, V_HEAD_DIM:] = jnp.ones((vv.shape[0], V_AUG_DIM - V_HEAD_DIM), BF16)


def _proj_kernel(x_ref, xp_ref, xn_ref, mod_ref, ng_ref, w_in_ref, qlg_ref, w_uq_ref,
                 kvlg_ref, w_ukv_ref, gq_ref, gk_ref, cs_ref, w_pool_ref, ps_ref,
                 q_ref, k_ref, v_ref, sga_ref, brp_ref,
                 h_scr, pin_scr, *, tm, seq):
    t = pl.program_id(1)
    nt = pl.num_programs(1)
    d = x_ref.shape[-1]
    shift = mod_ref[0, :, 0:d]
    scale = mod_ref[0, :, d:2 * d]
    a = ng_ref[...] * (1.0 + scale)

    def hnorm(xv):
        ms = jnp.mean(xv * xv, axis=-1, keepdims=True)
        return xv * lax.rsqrt(ms + NORM_EPS) * a + shift

    h_scr[0:HALO, :] = jnp.where(t > 0, hnorm(xp_ref[0]), 0.0).astype(BF16)
    h_scr[HALO:HALO + tm, :] = hnorm(x_ref[0]).astype(BF16)
    h_scr[HALO + tm:, :] = jnp.where(t < nt - 1, hnorm(xn_ref[0]), 0.0).astype(BF16)

    h_main = h_scr[HALO:HALO + tm, :]
    cs = cs_ref[...]
    low = _low_mask()

    u_a = jnp.dot(h_main, w_in_ref[:, C_CQ:C_GA], preferred_element_type=F32)
    cq_n = _rms_rows(u_a[:, C_CQ:C_CKV], qlg_ref[...]).astype(BF16)
    qa = jnp.dot(cq_n, w_uq_ref[...], preferred_element_type=F32)
    gq_nope = gq_ref[:, 0:LANES]
    gq_rr = gq_ref[:, LANES:2 * LANES]
    for h in range(MLA_HEADS):
        qn = qa[:, h * HEAD_COLS:h * HEAD_COLS + LANES]
        rr = qa[:, h * HEAD_COLS + LANES:(h + 1) * HEAD_COLS]
        ss = (jnp.sum(qn * qn, axis=-1, keepdims=True)
              + jnp.sum(rr * rr * low, axis=-1, keepdims=True))
        r = lax.rsqrt(ss * (1.0 / QK_HEAD_DIM) + NORM_EPS) * Q_SCALE
        q_ref[0, h, :, 0:LANES] = (qn * r * gq_nope).astype(BF16)
        q_rope = _rope_pair(rr, gq_rr, cs) * r
        q_ref[0, h, :, LANES:QK_HEAD_DIM] = q_rope[:, 0:QK_ROPE_DIM].astype(BF16)

    _kv_heads(u_a[:, C_CKV:C_KR], u_a[:, C_KR:C_GA], kvlg_ref[...], w_ukv_ref[...],
              gk_ref, cs, k_ref, v_ref)

    u_ga = jnp.dot(h_main, w_in_ref[:, C_GA:C_PIN], preferred_element_type=F32)
    sga_ref[0] = _silu(u_ga).astype(BF16)

    pin_scr[...] = jnp.dot(h_scr[...], w_in_ref[:, C_PIN:C_GP], preferred_element_type=F32)
    u_gp = jnp.dot(h_main, w_in_ref[:, C_GP:C_END], preferred_element_type=F32)
    tok = t * tm + lax.broadcasted_iota(jnp.int32, (tm, LANES), 0)
    for g, w in enumerate(POOL_WINDOWS):
        c0 = g * POOL_GROUP_DIM
        c1 = c0 + POOL_GROUP_DIM
        start = HALO - w // 2
        ws = pin_scr[start:start + tm, c0:c1]
        for j in range(1, w):
            ws = ws + pin_scr[start + j:start + j + tm, c0:c1]
        lo = jnp.maximum(tok - w // 2, 0)
        hi = jnp.minimum(tok - w // 2 + w, seq)
        cnt = (hi - lo).astype(F32)
        pooled = ws / cnt - pin_scr[HALO:HALO + tm, c0:c1]
        y = jnp.dot(pooled.astype(BF16), w_pool_ref[g], preferred_element_type=F32)
        brp_ref[0, :, c0:c1] = (_silu(u_gp[:, c0:c1]) * (y * ps_ref[:, c0:c1])).astype(BF16)


def _proj(x, mod3, norm_g, w_in_r, qlg, w_uq_r, kvlg, w_ukv_b, gq, gk, cs, w_pool_b, ps, *, tm):
    b, s, d = x.shape
    nt = s // tm
    hb = tm // HALO
    nhb = s // HALO
    const2 = lambda bi, ti: (0, 0)
    kern = functools.partial(_proj_kernel, tm=tm, seq=s)
    return pl.pallas_call(
        kern,
        grid=(b, nt),
        in_specs=[
            pl.BlockSpec((1, tm, d), lambda bi, ti: (bi, ti, 0)),
            pl.BlockSpec((1, HALO, d), lambda bi, ti: (bi, jnp.maximum(ti * hb - 1, 0), 0)),
            pl.BlockSpec((1, HALO, d), lambda bi, ti: (bi, jnp.minimum((ti + 1) * hb, nhb - 1), 0)),
            pl.BlockSpec((1, 1, 3 * d), lambda bi, ti: (bi, 0, 0)),
            pl.BlockSpec((1, d), const2),
            pl.BlockSpec((d, C_END), const2),
            pl.BlockSpec((1, Q_LORA_RANK), const2),
            pl.BlockSpec((Q_LORA_RANK, MLA_HEADS * HEAD_COLS), const2),
            pl.BlockSpec((1, KV_LORA_RANK), const2),
            pl.BlockSpec((KV_LORA_RANK, MLA_HEADS * HEAD_COLS), const2),
            pl.BlockSpec((1, 2 * LANES), const2),
            pl.BlockSpec((1, 2 * LANES), const2),
            pl.BlockSpec((tm, LANES), lambda bi, ti: (ti, 0)),
            pl.BlockSpec((len(POOL_WINDOWS), POOL_GROUP_DIM, POOL_GROUP_DIM), lambda bi, ti: (0, 0, 0)),
            pl.BlockSpec((1, D_POOL), const2),
        ],
        out_specs=[
            pl.BlockSpec((1, MLA_HEADS, tm, QK_HEAD_DIM), lambda bi, ti: (bi, 0, ti, 0)),
            pl.BlockSpec((1, MLA_HEADS, tm, QK_HEAD_DIM), lambda bi, ti: (bi, 0, ti, 0)),
            pl.BlockSpec((1, MLA_HEADS, tm, V_AUG_DIM), lambda bi, ti: (bi, 0, ti, 0)),
            pl.BlockSpec((1, tm, D_ATTN), lambda bi, ti: (bi, ti, 0)),
            pl.BlockSpec((1, tm, D_POOL), lambda bi, ti: (bi, ti, 0)),
        ],
        out_shape=[
            jax.ShapeDtypeStruct((b, MLA_HEADS, s, QK_HEAD_DIM), BF16),
            jax.ShapeDtypeStruct((b, MLA_HEADS, s, QK_HEAD_DIM), BF16),
            jax.ShapeDtypeStruct((b, MLA_HEADS, s, V_AUG_DIM), BF16),
            jax.ShapeDtypeStruct((b, s, D_ATTN), BF16),
            jax.ShapeDtypeStruct((b, s, D_POOL), BF16),
        ],
        scratch_shapes=[
            pltpu.VMEM((tm + 2 * HALO, d), BF16),
            pltpu.VMEM((tm + 2 * HALO, D_POOL), F32),
        ],
        compiler_params=pltpu.CompilerParams(
            dimension_semantics=("parallel", "parallel"),
            vmem_limit_bytes=VMEM_LIMIT),
        name="proj",
    )(x, x, x, mod3, norm_g, w_in_r, qlg, w_uq_r, kvlg, w_ukv_b, gq, gk, cs, w_pool_b, ps)


def _ctxkv_kernel(x_ref, mod_ref, ng_ref, w_kv_ref, kvlg_ref, w_ukv_ref, gk_ref, cs_ref,
                  k_ref, v_ref):
    d = x_ref.shape[-1]
    shift = mod_ref[0, :, 0:d]
    scale = mod_ref[0, :, d:2 * d]
    a = ng_ref[...] * (1.0 + scale)
    xv = x_ref[0]
    ms = jnp.mean(xv * xv, axis=-1, keepdims=True)
    hc = (xv * lax.rsqrt(ms + NORM_EPS) * a + shift).astype(BF16)
    u = jnp.dot(hc, w_kv_ref[...], preferred_element_type=F32)
    _kv_heads(u[:, 0:KV_LORA_RANK], u[:, KV_LORA_RANK:], kvlg_ref[...], w_ukv_ref[...],
              gk_ref, cs_ref[...], k_ref, v_ref)


def _ctxkv(ctx, mod3, norm_g, w_kv, kvlg, w_ukv_b, gk, cs_ctx):
    b, lc, d = ctx.shape
    const2 = lambda bi: (0, 0)
    nkv = KV_LORA_RANK + 2 * QK_ROPE_DIM
    return pl.pallas_call(
        _ctxkv_kernel,
        grid=(b,),
        in_specs=[
            pl.BlockSpec((1, lc, d), lambda bi: (bi, 0, 0)),
            pl.BlockSpec((1, 1, 3 * d), lambda bi: (b, 0, 0)),
            pl.BlockSpec((1, d), const2),
            pl.BlockSpec((d, nkv), const2),
            pl.BlockSpec((1, KV_LORA_RANK), const2),
            pl.BlockSpec((KV_LORA_RANK, MLA_HEADS * HEAD_COLS), const2),
            pl.BlockSpec((1, 2 * LANES), const2),
            pl.BlockSpec((lc, LANES), const2),
        ],
        out_specs=[
            pl.BlockSpec((1, MLA_HEADS, lc, QK_HEAD_DIM), lambda bi: (bi, 0, 0, 0)),
            pl.BlockSpec((1, MLA_HEADS, lc, V_AUG_DIM), lambda bi: (bi, 0, 0, 0)),
        ],
        out_shape=[
            jax.ShapeDtypeStruct((b, MLA_HEADS, lc, QK_HEAD_DIM), BF16),
            jax.ShapeDtypeStruct((b, MLA_HEADS, lc, V_AUG_DIM), BF16),
        ],
        compiler_params=pltpu.CompilerParams(
            dimension_semantics=("parallel",), vmem_limit_bytes=VMEM_LIMIT),
        name="ctxkv",
    )(ctx, mod3, norm_g, w_kv, kvlg, w_ukv_b, gk, cs_ctx)


def _attn_kernel(q_ref, k_ref, v_ref, kc_ref, vc_ref, o_ref, m_scr, acc_scr, *, tk):
    nk = k_ref.shape[2] // tk
    nqb = q_ref.shape[2] // Q_BLOCK
    dv = V_HEAD_DIM

    def step(h, kb, vb):
        s = lax.dot_general(q_ref[0, h], kb, (((1,), (1,)), ((), ())),
                            preferred_element_type=F32)
        m_prev = m_scr[h]
        m_next = jnp.maximum(m_prev, jnp.max(s, axis=-1, keepdims=True))
        alpha = jnp.exp2(m_prev - m_next)
        p = jnp.exp2(s - pltpu.repeat(m_next, s.shape[1] // LANES, 1))
        pv = jnp.dot(p.astype(BF16), vb, preferred_element_type=F32)
        acc_scr[h] = pltpu.repeat(alpha, V_AUG_DIM // LANES, 1) * acc_scr[h] + pv
        m_scr[h] = m_next

    m_scr[...] = jnp.full(m_scr.shape, -jnp.inf, F32)
    acc_scr[...] = jnp.zeros(acc_scr.shape, F32)
    for h in range(MLA_HEADS):
        step(h, kc_ref[0, h], vc_ref[0, h])

    def body(j, carry):
        off = pl.multiple_of(j * tk, tk)
        for h in range(MLA_HEADS):
            step(h, k_ref[0, h, pl.ds(off, tk), :], v_ref[0, h, pl.ds(off, tk), :])
        return carry

    lax.fori_loop(0, nk, body, 0)
    for h in range(MLA_HEADS):
        res = (acc_scr[h, :, 0:dv] / acc_scr[h, :, dv:2 * dv]).astype(o_ref.dtype)
        for n in range(nqb):
            c0 = (n * MLA_HEADS + h) * dv
            o_ref[0, :, c0:c0 + dv] = res[n * Q_BLOCK:(n + 1) * Q_BLOCK, :]


def _attention(q, k, v, kc, vc, *, tq, tk):
    b, h, s, dk = q.shape
    lc = kc.shape[2]
    dva = v.shape[-1]
    dv = V_HEAD_DIM
    kern = functools.partial(_attn_kernel, tk=tk)
    resident = dict(pipeline_mode=pl.Buffered(1))
    return pl.pallas_call(
        kern,
        grid=(b, s // tq),
        in_specs=[
            pl.BlockSpec((1, h, tq, dk), lambda bi, qi: (bi, 0, qi, 0)),
            pl.BlockSpec((1, h, s, dk), lambda bi, qi: (bi, 0, 0, 0), **resident),
            pl.BlockSpec((1, h, s, dva), lambda bi, qi: (bi, 0, 0, 0), **resident),
            pl.BlockSpec((1, h, lc, dk), lambda bi, qi: (bi, 0, 0, 0), **resident),
            pl.BlockSpec((1, h, lc, dva), lambda bi, qi: (bi, 0, 0, 0), **resident),
        ],
        out_specs=pl.BlockSpec((1, Q_BLOCK, (tq // Q_BLOCK) * h * dv), lambda bi, qi: (bi, 0, qi)),
        out_shape=jax.ShapeDtypeStruct((b, Q_BLOCK, (s // Q_BLOCK) * h * dv), BF16),
        scratch_shapes=[
            pltpu.VMEM((h, tq, LANES), F32),
            pltpu.VMEM((h, tq, dva), F32),
        ],
        compiler_params=pltpu.CompilerParams(
            dimension_semantics=("parallel", "arbitrary"),
            vmem_limit_bytes=VMEM_LIMIT),
        name="attn",
    )(q, k, v, kc, vc)


def _out_kernel(attn_ref, sga_ref, brp_ref, x_ref, mod_ref, w_ref, o_ref):
    d = x_ref.shape[-1]
    gate = mod_ref[0, :, 2 * d:3 * d]
    br_a = (sga_ref[0].astype(F32) * attn_ref[0].astype(F32)).astype(BF16)
    y = jnp.dot(br_a, w_ref[0:D_ATTN, :], preferred_element_type=F32)
    y = y + jnp.dot(brp_ref[0], w_ref[D_ATTN:, :], preferred_element_type=F32)
    o_ref[0] = x_ref[0] + gate * y


def _out_proj(attn, sga, brp, x, mod3, w_out_b, *, tm):
    b, s, d = x.shape
    return pl.pallas_call(
        _out_kernel,
        grid=(b, s // tm),
        in_specs=[
            pl.BlockSpec((1, tm, D_ATTN), lambda bi, ti: (bi, ti, 0)),
            pl.BlockSpec((1, tm, D_ATTN), lambda bi, ti: (bi, ti, 0)),
            pl.BlockSpec((1, tm, D_POOL), lambda bi, ti: (bi, ti, 0)),
            pl.BlockSpec((1, tm, d), lambda bi, ti: (bi, ti, 0)),
            pl.BlockSpec((1, 1, 3 * d), lambda bi, ti: (bi, 0, 0)),
            pl.BlockSpec((D_ATTN + D_POOL, d), lambda bi, ti: (0, 0)),
        ],
        out_specs=pl.BlockSpec((1, tm, d), lambda bi, ti: (bi, ti, 0)),
        out_shape=jax.ShapeDtypeStruct((b, s, d), F32),
        compiler_params=pltpu.CompilerParams(
            dimension_semantics=("parallel", "parallel"), vmem_limit_bytes=VMEM_LIMIT),
        name="outproj",
    )(attn, sga, brp, x, mod3, w_out_b)


def _rot_perm():
    q = QK_ROPE_DIM // 4
    perm = np.concatenate([np.arange(q, 2 * q), np.arange(0, q),
                           np.arange(3 * q, 4 * q), np.arange(2 * q, 3 * q)])
    sign = np.concatenate([-np.ones(q), np.ones(q), -np.ones(q), np.ones(q)]).astype(np.float32)
    return perm, sign


def _rope_table(seq):
    rows = seq // GRID_W
    row = jnp.repeat(jnp.arange(rows, dtype=F32), GRID_W)
    col = jnp.tile(jnp.arange(GRID_W, dtype=F32), rows)
    n_freq = QK_ROPE_DIM // 4
    inv = ROPE_BASE ** (-jnp.arange(n_freq, dtype=F32) / n_freq)
    ang_r = row[:, None] * inv
    ang_c = col[:, None] * inv
    ang = jnp.concatenate([ang_r, ang_r, ang_c, ang_c], axis=-1)
    return jnp.concatenate([jnp.cos(ang), jnp.sin(ang)], axis=-1)


def kernel(x, c, ctx, c_ctx, w_mod, b_mod, norm_g, w_in, q_lora_g, w_uq, kv_lora_g, w_ukv,
           q_norm_g, k_norm_g, w_pool, pool_scale, w_out):
    b, s, d = x.shape
    lc = ctx.shape[1]
    depth = w_mod.shape[0]
    assert depth == 1, "single-layer block"
    perm, sign = _rot_perm()
    l = 0

    cc = jnp.concatenate([c, c_ctx[None, :], jnp.zeros((MOD_ROWS - b - 1, d), F32)], axis=0)
    mod = _modulation(cc, w_mod[l], b_mod[l][None, :])
    mod3 = mod[:, None, :]

    wi = w_in[l]
    o_cq, o_ckv, o_kr = 0, Q_LORA_RANK, Q_LORA_RANK + KV_LORA_RANK
    o_ga = o_kr + QK_ROPE_DIM
    o_pin = o_ga + D_ATTN
    o_gp = o_pin + D_POOL
    w_kr = wi[:, o_kr:o_ga]
    w_in_r = jnp.concatenate(
        [wi[:, o_cq:o_kr], w_kr, w_kr[:, perm] * sign, wi[:, o_ga:o_pin],
         wi[:, o_pin:o_gp], wi[:, o_gp:]], axis=1).astype(BF16)
    wq = w_uq[l].reshape(Q_LORA_RANK, MLA_HEADS, QK_HEAD_DIM)
    wq_rope = wq[:, :, QK_NOPE_DIM:]
    w_uq_r = jnp.concatenate([wq, wq_rope[:, :, perm] * sign], axis=-1)
    w_uq_r = w_uq_r.reshape(Q_LORA_RANK, MLA_HEADS * HEAD_COLS).astype(BF16)
    w_ukv_b = w_ukv[l].astype(BF16)
    w_pool_b = w_pool[l].astype(BF16)
    w_out_b = w_out[l].astype(BF16)

    def gain_row(g):
        return jnp.concatenate([g, g[QK_NOPE_DIM:][perm]])[None, :]

    gq = gain_row(q_norm_g[l])
    gk = gain_row(k_norm_g[l])
    cs = _rope_table(s)
    cs_ctx = jnp.concatenate([jnp.ones((lc, QK_ROPE_DIM), F32),
                              jnp.zeros((lc, QK_ROPE_DIM), F32)], axis=1)

    ng = norm_g[l][None, :]
    qlg = q_lora_g[l][None, :]
    kvlg = kv_lora_g[l][None, :]
    ps = pool_scale[l][None, :]

    q, k, v, sga, brp = _proj(x, mod3, ng, w_in_r, qlg, w_uq_r, kvlg, w_ukv_b, gq, gk, cs,
                              w_pool_b, ps, tm=512)
    kc, vc = _ctxkv(ctx, mod3, ng, w_in_r[:, C_CKV:C_GA], kvlg, w_ukv_b, gk, cs_ctx)
    attn = _attention(q, k, v, kc, vc, tq=512, tk=512).reshape(b, s, D_ATTN)
    return _out_proj(attn, sga, brp, x, mod3, w_out_b, tm=512)
```

```python
import functools
import math

import jax
import jax.numpy as jnp
import numpy as np
from jax import lax
from jax.experimental import pallas as pl
from jax.experimental.pallas import tpu as pltpu

F32 = jnp.float32
BF16 = jnp.bfloat16

GRID_W = 64
MLA_HEADS = 4
QK_NOPE_DIM = 128
QK_ROPE_DIM = 64
QK_HEAD_DIM = QK_NOPE_DIM + QK_ROPE_DIM
V_HEAD_DIM = 128
V_AUG_DIM = 2 * V_HEAD_DIM
Q_LORA_RANK = 256
KV_LORA_RANK = 128
ROPE_BASE = 10000.0
POOL_WINDOWS = (2, 4, 8, 16)
POOL_GROUP_DIM = 128
NORM_EPS = 1e-6
Q_BLOCK = 128
D_ATTN = MLA_HEADS * V_HEAD_DIM
D_POOL = len(POOL_WINDOWS) * POOL_GROUP_DIM

LANES = 128
HEAD_COLS = 2 * LANES
HALO = 16
MOD_ROWS = 16

C_CQ = 0
C_CKV = C_CQ + Q_LORA_RANK
C_KR = C_CKV + KV_LORA_RANK
C_GA = C_KR + 2 * QK_ROPE_DIM
C_PIN = C_GA + D_ATTN
C_GP = C_PIN + D_POOL
C_END = C_GP + D_POOL

VMEM_LIMIT = 56 * 1024 * 1024

Q_SCALE = QK_HEAD_DIM ** -0.5 * math.log2(math.e)


def _silu(x):
    return x / (1.0 + jnp.exp(-x))


def _rms_rows(x, g):
    ms = jnp.mean(x * x, axis=-1, keepdims=True)
    return x * lax.rsqrt(ms + NORM_EPS) * g


def _mod_kernel(cc_ref, w_ref, b_ref, o_ref):
    a = _silu(cc_ref[...])
    o_ref[...] = jnp.dot(a, w_ref[...], preferred_element_type=F32,
                         precision=lax.Precision.HIGHEST) + b_ref[...]


def _modulation(cc, w_mod, b_mod):
    d = w_mod.shape[0]
    n = w_mod.shape[1]
    tn = 768
    return pl.pallas_call(
        _mod_kernel,
        grid=(n // tn,),
        in_specs=[
            pl.BlockSpec((MOD_ROWS, d), lambda j: (0, 0)),
            pl.BlockSpec((d, tn), lambda j: (0, j)),
            pl.BlockSpec((1, tn), lambda j: (0, j)),
        ],
        out_specs=pl.BlockSpec((MOD_ROWS, tn), lambda j: (0, j)),
        out_shape=jax.ShapeDtypeStruct((MOD_ROWS, n), F32),
        name="mod",
    )(cc, w_mod, b_mod)


def _low_mask():
    lane = lax.broadcasted_iota(jnp.int32, (1, LANES), 1)
    return (lane < QK_ROPE_DIM).astype(F32)


def _rope_pair(rr, g_rr, cs):
    pr = rr * g_rr * cs
    return pr + pltpu.roll(pr, QK_ROPE_DIM, axis=1)


def _kv_heads(ckv, kr2, kvlg, w_ukv, gk_ref, cs, k_ref, v_ref):
    low = _low_mask()
    ckv_n = _rms_rows(ckv, kvlg).astype(BF16)
    kv = jnp.dot(ckv_n, w_ukv, preferred_element_type=F32)
    g_nope = gk_ref[:, 0:LANES]
    g_rr = gk_ref[:, LANES:2 * LANES]
    k_rope = _rope_pair(kr2, g_rr, cs)
    ss_rope = jnp.sum(kr2 * kr2 * low, axis=-1, keepdims=True)
    for h in range(MLA_HEADS):
        kn = kv[:, h * HEAD_COLS:h * HEAD_COLS + LANES]
        vv = kv[:, h * HEAD_COLS + LANES:(h + 1) * HEAD_COLS]
        ss = jnp.sum(kn * kn, axis=-1, keepdims=True) + ss_rope
        r = lax.rsqrt(ss * (1.0 / QK_HEAD_DIM) + NORM_EPS)
        k_ref[0, h, :, 0:LANES] = (kn * r * g_nope).astype(BF16)
        k_ref[0, h, :, LANES:QK_HEAD_DIM] = (k_rope * r)[:, 0:QK_ROPE_DIM].astype(BF16)
        v_ref[0, h, :, 0:V_HEAD_DIM] = vv.astype(BF16)
        v_ref[0, h, :, V_HEAD_DIM:] = jnp.ones((vv.shape[0], V_AUG_DIM - V_HEAD_DIM), BF16)


def _proj_kernel(x_ref, xp_ref, xn_ref, mod_ref, ng_ref, w_in_ref, qlg_ref, w_uq_ref,
                 kvlg_ref, w_ukv_ref, gq_ref, gk_ref, cs_ref, w_pool_ref, ps_ref,
                 q_ref, k_ref, v_ref, sga_ref, brp_ref,
                 h_scr, pin_scr, *, tm, seq):
    t = pl.program_id(1)
    nt = pl.num_programs(1)
    d = x_ref.shape[-1]
    shift = mod_ref[0, :, 0:d]
    scale = mod_ref[0, :, d:2 * d]
    a = ng_ref[...] * (1.0 + scale)

    def hnorm(xv):
        ms = jnp.mean(xv * xv, axis=-1, keepdims=True)
        return xv * lax.rsqrt(ms + NORM_EPS) * a + shift

    h_scr[0:HALO, :] = jnp.where(t > 0, hnorm(xp_ref[0]), 0.0).astype(BF16)
    h_scr[HALO:HALO + tm, :] = hnorm(x_ref[0]).astype(BF16)
    h_scr[HALO + tm:, :] = jnp.where(t < nt - 1, hnorm(xn_ref[0]), 0.0).astype(BF16)

    h_main = h_scr[HALO:HALO + tm, :]
    cs = cs_ref[...]
    low = _low_mask()

    u_a = jnp.dot(h_main, w_in_ref[:, C_CQ:C_GA], preferred_element_type=F32)
    cq_n = _rms_rows(u_a[:, C_CQ:C_CKV], qlg_ref[...]).astype(BF16)
    qa = jnp.dot(cq_n, w_uq_ref[...], preferred_element_type=F32)
    gq_nope = gq_ref[:, 0:LANES]
    gq_rr = gq_ref[:, LANES:2 * LANES]
    for h in range(MLA_HEADS):
        qn = qa[:, h * HEAD_COLS:h * HEAD_COLS + LANES]
        rr = qa[:, h * HEAD_COLS + LANES:(h + 1) * HEAD_COLS]
        ss = (jnp.sum(qn * qn, axis=-1, keepdims=True)
              + jnp.sum(rr * rr * low, axis=-1, keepdims=True))
        r = lax.rsqrt(ss * (1.0 / QK_HEAD_DIM) + NORM_EPS) * Q_SCALE
        q_ref[0, h, :, 0:LANES] = (qn * r * gq_nope).astype(BF16)
        q_rope = _rope_pair(rr, gq_rr, cs) * r
        q_ref[0, h, :, LANES:QK_HEAD_DIM] = q_rope[:, 0:QK_ROPE_DIM].astype(BF16)

    _kv_heads(u_a[:, C_CKV:C_KR], u_a[:, C_KR:C_GA], kvlg_ref[...], w_ukv_ref[...],
              gk_ref, cs, k_ref, v_ref)

    u_ga = jnp.dot(h_main, w_in_ref[:, C_GA:C_PIN], preferred_element_type=F32)
    sga_ref[0] = _silu(u_ga).astype(BF16)

    pin_scr[...] = jnp.dot(h_scr[...], w_in_ref[:, C_PIN:C_GP], preferred_element_type=F32)
    u_gp = jnp.dot(h_main, w_in_ref[:, C_GP:C_END], preferred_element_type=F32)
    tok = t * tm + lax.broadcasted_iota(jnp.int32, (tm, LANES), 0)
    for g, w in enumerate(POOL_WINDOWS):
        c0 = g * POOL_GROUP_DIM
        c1 = c0 + POOL_GROUP_DIM
        start = HALO - w // 2
        ws = pin_scr[start:start + tm, c0:c1]
        for j in range(1, w):
            ws = ws + pin_scr[start + j:start + j + tm, c0:c1]
        lo = jnp.maximum(tok - w // 2, 0)
        hi = jnp.minimum(tok - w // 2 + w, seq)
        cnt = (hi - lo).astype(F32)
        pooled = ws / cnt - pin_scr[HALO:HALO + tm, c0:c1]
        y = jnp.dot(pooled.astype(BF16), w_pool_ref[g], preferred_element_type=F32)
        brp_ref[0, :, c0:c1] = (_silu(u_gp[:, c0:c1]) * (y * ps_ref[:, c0:c1])).astype(BF16)


def _proj(x, mod3, norm_g, w_in_r, qlg, w_uq_r, kvlg, w_ukv_b, gq, gk, cs, w_pool_b, ps, *, tm):
    b, s, d = x.shape
    nt = s // tm
    hb = tm // HALO
    nhb = s // HALO
    const2 = lambda bi, ti: (0, 0)
    kern = functools.partial(_proj_kernel, tm=tm, seq=s)
    return pl.pallas_call(
        kern,
        grid=(b, nt),
        in_specs=[
            pl.BlockSpec((1, tm, d), lambda bi, ti: (bi, ti, 0)),
            pl.BlockSpec((1, HALO, d), lambda bi, ti: (bi, jnp.maximum(ti * hb - 1, 0), 0)),
            pl.BlockSpec((1, HALO, d), lambda bi, ti: (bi, jnp.minimum((ti + 1) * hb, nhb - 1), 0)),
            pl.BlockSpec((1, 1, 3 * d), lambda bi, ti: (bi, 0, 0)),
            pl.BlockSpec((1, d), const2),
            pl.BlockSpec((d, C_END), const2),
            pl.BlockSpec((1, Q_LORA_RANK), const2),
            pl.BlockSpec((Q_LORA_RANK, MLA_HEADS * HEAD_COLS), const2),
            pl.BlockSpec((1, KV_LORA_RANK), const2),
            pl.BlockSpec((KV_LORA_RANK, MLA_HEADS * HEAD_COLS), const2),
            pl.BlockSpec((1, 2 * LANES), const2),
            pl.BlockSpec((1, 2 * LANES), const2),
            pl.BlockSpec((tm, LANES), lambda bi, ti: (ti, 0)),
            pl.BlockSpec((len(POOL_WINDOWS), POOL_GROUP_DIM, POOL_GROUP_DIM), lambda bi, ti: (0, 0, 0)),
            pl.BlockSpec((1, D_POOL), const2),
        ],
        out_specs=[
            pl.BlockSpec((1, MLA_HEADS, tm, QK_HEAD_DIM), lambda bi, ti: (bi, 0, ti, 0)),
            pl.BlockSpec((1, MLA_HEADS, tm, QK_HEAD_DIM), lambda bi, ti: (bi, 0, ti, 0)),
            pl.BlockSpec((1, MLA_HEADS, tm, V_AUG_DIM), lambda bi, ti: (bi, 0, ti, 0)),
            pl.BlockSpec((1, tm, D_ATTN), lambda bi, ti: (bi, ti, 0)),
            pl.BlockSpec((1, tm, D_POOL), lambda bi, ti: (bi, ti, 0)),
        ],
        out_shape=[
            jax.ShapeDtypeStruct((b, MLA_HEADS, s, QK_HEAD_DIM), BF16),
            jax.ShapeDtypeStruct((b, MLA_HEADS, s, QK_HEAD_DIM), BF16),
            jax.ShapeDtypeStruct((b, MLA_HEADS, s, V_AUG_DIM), BF16),
            jax.ShapeDtypeStruct((b, s, D_ATTN), BF16),
            jax.ShapeDtypeStruct((b, s, D_POOL), BF16),
        ],
        scratch_shapes=[
            pltpu.VMEM((tm + 2 * HALO, d), BF16),
            pltpu.VMEM((tm + 2 * HALO, D_POOL), F32),
        ],
        compiler_params=pltpu.CompilerParams(
            dimension_semantics=("parallel", "parallel"),
            vmem_limit_bytes=VMEM_LIMIT),
        name="proj",
    )(x, x, x, mod3, norm_g, w_in_r, qlg, w_uq_r, kvlg, w_ukv_b, gq, gk, cs, w_pool_b, ps)


def _ctxkv_kernel(x_ref, mod_ref, ng_ref, w_kv_ref, kvlg_ref, w_ukv_ref, gk_ref, cs_ref,
                  k_ref, v_ref):
    d = x_ref.shape[-1]
    shift = mod_ref[0, :, 0:d]
    scale = mod_ref[0, :, d:2 * d]
    a = ng_ref[...] * (1.0 + scale)
    xv = x_ref[0]
    ms = jnp.mean(xv * xv, axis=-1, keepdims=True)
    hc = (xv * lax.rsqrt(ms + NORM_EPS) * a + shift).astype(BF16)
    u = jnp.dot(hc, w_kv_ref[...], preferred_element_type=F32)
    _kv_heads(u[:, 0:KV_LORA_RANK], u[:, KV_LORA_RANK:], kvlg_ref[...], w_ukv_ref[...],
              gk_ref, cs_ref[...], k_ref, v_ref)


def _ctxkv(ctx, mod3, norm_g, w_kv, kvlg, w_ukv_b, gk, cs_ctx):
    b, lc, d = ctx.shape
    const2 = lambda bi: (0, 0)
    nkv = KV_LORA_RANK + 2 * QK_ROPE_DIM
    return pl.pallas_call(
        _ctxkv_kernel,
        grid=(b,),
        in_specs=[
            pl.BlockSpec((1, lc, d), lambda bi: (bi, 0, 0)),
            pl.BlockSpec((1, 1, 3 * d), lambda bi: (b, 0, 0)),
            pl.BlockSpec((1, d), const2),
            pl.BlockSpec((d, nkv), const2),
            pl.BlockSpec((1, KV_LORA_RANK), const2),
            pl.BlockSpec((KV_LORA_RANK, MLA_HEADS * HEAD_COLS), const2),
            pl.BlockSpec((1, 2 * LANES), const2),
            pl.BlockSpec((lc, LANES), const2),
        ],
        out_specs=[
            pl.BlockSpec((1, MLA_HEADS, lc, QK_HEAD_DIM), lambda bi: (bi, 0, 0, 0)),
            pl.BlockSpec((1, MLA_HEADS, lc, V_AUG_DIM), lambda bi: (bi, 0, 0, 0)),
        ],
        out_shape=[
            jax.ShapeDtypeStruct((b, MLA_HEADS, lc, QK_HEAD_DIM), BF16),
            jax.ShapeDtypeStruct((b, MLA_HEADS, lc, V_AUG_DIM), BF16),
        ],
        compiler_params=pltpu.CompilerParams(
            dimension_semantics=("parallel",), vmem_limit_bytes=VMEM_LIMIT),
        name="ctxkv",
    )(ctx, mod3, norm_g, w_kv, kvlg, w_ukv_b, gk, cs_ctx)


def _attn_kernel(q_ref, k_ref, v_ref, kc_ref, vc_ref, o_ref, m_scr, acc_scr, *, tk, unroll):
    nk = k_ref.shape[2] // tk
    nqb = q_ref.shape[2] // Q_BLOCK
    dv = V_HEAD_DIM

    def step(h, kb, vb):
        s = lax.dot_general(q_ref[0, h], kb, (((1,), (1,)), ((), ())),
                            preferred_element_type=F32)
        m_prev = m_scr[h]
        m_next = jnp.maximum(m_prev, jnp.max(s, axis=-1, keepdims=True))
        alpha = jnp.exp2(m_prev - m_next)
        p = jnp.exp2(s - pltpu.repeat(m_next, s.shape[1] // LANES, 1))
        pv = jnp.dot(p.astype(BF16), vb, preferred_element_type=F32)
        acc_scr[h] = pltpu.repeat(alpha, V_AUG_DIM // LANES, 1) * acc_scr[h] + pv
        m_scr[h] = m_next

    m_scr[...] = jnp.full(m_scr.shape, -jnp.inf, F32)
    acc_scr[...] = jnp.zeros(acc_scr.shape, F32)
    for h in range(MLA_HEADS):
        step(h, kc_ref[0, h], vc_ref[0, h])

    def body(j, carry):
        off = pl.multiple_of(j * tk, tk)
        for h in range(MLA_HEADS):
            step(h, k_ref[0, h, pl.ds(off, tk), :], v_ref[0, h, pl.ds(off, tk), :])
        return carry

    lax.fori_loop(0, nk, body, 0, unroll=unroll)
    for h in range(MLA_HEADS):
        res =(acc_scr[h, :, 0:dv] / acc_scr[h, :, dv:2 * dv]).astype(o_ref.dtype)
        for n in range(nqb):
            c0 = (n * MLA_HEADS + h) * dv
            o_ref[0, :, c0:c0 + dv] = res[n * Q_BLOCK:(n + 1) * Q_BLOCK, :]


def _attention(q, k, v, kc, vc, *, tq, tk, unroll):
    b, h, s, dk = q.shape
    lc = kc.shape[2]
    dva = v.shape[-1]
    dv = V_HEAD_DIM
    kern = functools.partial(_attn_kernel, tk=tk, unroll=unroll)
    resident = dict(pipeline_mode=pl.Buffered(1))
    return pl.pallas_call(
        kern,
        grid=(b, s // tq),
        in_specs=[
            pl.BlockSpec((1, h, tq, dk), lambda bi, qi: (bi, 0, qi, 0)),
            pl.BlockSpec((1, h, s, dk), lambda bi, qi: (bi, 0, 0, 0), **resident),
            pl.BlockSpec((1, h, s, dva), lambda bi, qi: (bi, 0, 0, 0), **resident),
            pl.BlockSpec((1, h, lc, dk), lambda bi, qi: (bi, 0, 0, 0), **resident),
            pl.BlockSpec((1, h, lc, dva), lambda bi, qi: (bi, 0, 0, 0), **resident),
        ],
        out_specs=pl.BlockSpec((1, Q_BLOCK, (tq // Q_BLOCK) * h * dv), lambda bi, qi: (bi, 0, qi)),
        out_shape=jax.ShapeDtypeStruct((b, Q_BLOCK, (s // Q_BLOCK) * h * dv), BF16),
        scratch_shapes=[
            pltpu.VMEM((h, tq, LANES), F32),
            pltpu.VMEM((h, tq, dva), F32),
        ],
        compiler_params=pltpu.CompilerParams(
            dimension_semantics=("parallel", "arbitrary"),
            vmem_limit_bytes=VMEM_LIMIT),
        name="attn",
    )(q, k, v, kc, vc)


def _out_kernel(attn_ref, sga_ref, brp_ref, x_ref, mod_ref, w_ref, o_ref):
    d = x_ref.shape[-1]
    gate = mod_ref[0, :, 2 * d:3 * d]
    br_a = (sga_ref[0].astype(F32) * attn_ref[0].astype(F32)).astype(BF16)
    y = jnp.dot(br_a, w_ref[0:D_ATTN, :], preferred_element_type=F32)
    y = y + jnp.dot(brp_ref[0], w_ref[D_ATTN:, :], preferred_element_type=F32)
    o_ref[0] = x_ref[0] + gate * y


def _out_proj(attn, sga, brp, x, mod3, w_out_b, *, tm):
    b, s, d = x.shape
    return pl.pallas_call(
        _out_kernel,
        grid=(b, s // tm),
        in_specs=[
            pl.BlockSpec((1, tm, D_ATTN), lambda bi, ti: (bi, ti, 0)),
            pl.BlockSpec((1, tm, D_ATTN), lambda bi, ti: (bi, ti, 0)),
            pl.BlockSpec((1, tm, D_POOL), lambda bi, ti: (bi, ti, 0)),
            pl.BlockSpec((1, tm, d), lambda bi, ti: (bi, ti, 0)),
            pl.BlockSpec((1, 1, 3 * d), lambda bi, ti: (bi, 0, 0)),
            pl.BlockSpec((D_ATTN + D_POOL, d), lambda bi, ti: (0, 0)),
        ],
        out_specs=pl.BlockSpec((1, tm, d), lambda bi, ti: (bi, ti, 0)),
        out_shape=jax.ShapeDtypeStruct((b, s, d), F32),
        compiler_params=pltpu.CompilerParams(
            dimension_semantics=("parallel", "parallel"), vmem_limit_bytes=VMEM_LIMIT),
        name="outproj",
    )(attn, sga, brp, x, mod3, w_out_b)


def _rot_perm():
    q = QK_ROPE_DIM // 4
    perm = np.concatenate([np.arange(q, 2 * q), np.arange(0, q),
                           np.arange(3 * q, 4 * q), np.arange(2 * q, 3 * q)])
    sign = np.concatenate([-np.ones(q), np.ones(q), -np.ones(q), np.ones(q)]).astype(np.float32)
    return perm, sign


def _rope_table(seq):
    rows = seq // GRID_W
    row = jnp.repeat(jnp.arange(rows, dtype=F32), GRID_W)
    col = jnp.tile(jnp.arange(GRID_W, dtype=F32), rows)
    n_freq = QK_ROPE_DIM // 4
    inv = ROPE_BASE ** (-jnp.arange(n_freq, dtype=F32) / n_freq)
    ang_r = row[:, None] * inv
    ang_c = col[:, None] * inv
    ang = jnp.concatenate([ang_r, ang_r, ang_c, ang_c], axis=-1)
    return jnp.concatenate([jnp.cos(ang), jnp.sin(ang)], axis=-1)


def kernel(x, c, ctx, c_ctx, w_mod, b_mod, norm_g, w_in, q_lora_g, w_uq, kv_lora_g, w_ukv,
           q_norm_g, k_norm_g, w_pool, pool_scale, w_out):
    b, s, d = x.shape
    lc = ctx.shape[1]
    depth = w_mod.shape[0]
    assert depth == 1, "single-layer block"
    perm, sign = _rot_perm()
    l = 0

    cc = jnp.concatenate([c, c_ctx[None, :], jnp.zeros((MOD_ROWS - b - 1, d), F32)], axis=0)
    mod = _modulation(cc, w_mod[l], b_mod[l][None, :])
    mod3 = mod[:, None, :]

    wi = w_in[l]
    o_cq, o_ckv, o_kr = 0, Q_LORA_RANK, Q_LORA_RANK + KV_LORA_RANK
    o_ga = o_kr + QK_ROPE_DIM
    o_pin = o_ga + D_ATTN
    o_gp = o_pin + D_POOL
    w_kr = wi[:, o_kr:o_ga]
    w_in_r = jnp.concatenate(
        [wi[:, o_cq:o_kr], w_kr, w_kr[:, perm] * sign, wi[:, o_ga:o_pin],
         wi[:, o_pin:o_gp], wi[:, o_gp:]], axis=1).astype(BF16)
    wq = w_uq[l].reshape(Q_LORA_RANK, MLA_HEADS, QK_HEAD_DIM)
    wq_rope = wq[:, :, QK_NOPE_DIM:]
    w_uq_r = jnp.concatenate([wq, wq_rope[:, :, perm] * sign], axis=-1)
    w_uq_r = w_uq_r.reshape(Q_LORA_RANK, MLA_HEADS * HEAD_COLS).astype(BF16)
    w_ukv_b = w_ukv[l].astype(BF16)
    w_pool_b = w_pool[l].astype(BF16)
    w_out_b = w_out[l].astype(BF16)

    def gain_row(g):
        return jnp.concatenate([g, g[QK_NOPE_DIM:][perm]])[None, :]

    gq = gain_row(q_norm_g[l])
    gk = gain_row(k_norm_g[l])
    cs = _rope_table(s)
    cs_ctx = jnp.concatenate([jnp.ones((lc, QK_ROPE_DIM), F32),
                              jnp.zeros((lc, QK_ROPE_DIM), F32)], axis=1)

    ng = norm_g[l][None, :]
    qlg = q_lora_g[l][None, :]
    kvlg = kv_lora_g[l][None, :]
    ps = pool_scale[l][None, :]

    q, k, v, sga, brp = _proj(x, mod3, ng, w_in_r, qlg, w_uq_r, kvlg, w_ukv_b, gq, gk, cs,
                              w_pool_b, ps, tm=512)
    kc, vc = _ctxkv(ctx, mod3, ng, w_in_r[:, C_CKV:C_GA], kvlg, w_ukv_b, gk, cs_ctx)
    attn = _attention(q, k, v, kc, vc, tq=512, tk=512, unroll=8).reshape(b, s, D_ATTN)
    return _out_proj(attn, sga, brp, x, mod3, w_out_b, tm=512)
```

```python
import functools
import math

import jax
import jax.numpy as jnp
import numpy as np
from jax import lax
from jax.experimental import pallas as pl
from jax.experimental.pallas import tpu as pltpu

F32 = jnp.float32
BF16 = jnp.bfloat16

GRID_W = 64
MLA_HEADS = 4
QK_NOPE_DIM = 128
QK_ROPE_DIM = 64
QK_HEAD_DIM = QK_NOPE_DIM + QK_ROPE_DIM
V_HEAD_DIM = 128
V_AUG_DIM = 2 * V_HEAD_DIM
Q_LORA_RANK = 256
KV_LORA_RANK = 128
ROPE_BASE = 10000.0
POOL_WINDOWS = (2, 4, 8, 16)
POOL_GROUP_DIM = 128
NORM_EPS = 1e-6
Q_BLOCK = 128
D_ATTN = MLA_HEADS * V_HEAD_DIM
D_POOL = len(POOL_WINDOWS) * POOL_GROUP_DIM

LANES = 128
HEAD_COLS = 2 * LANES
HALO = 16
MOD_ROWS = 16

C_CQ = 0
C_CKV = C_CQ + Q_LORA_RANK
C_KR = C_CKV + KV_LORA_RANK
C_GA = C_KR + 2 * QK_ROPE_DIM
C_PIN = C_GA + D_ATTN
C_GP = C_PIN + D_POOL
C_END = C_GP + D_POOL

VMEM_LIMIT = 56 * 1024 * 1024

Q_SCALE = QK_HEAD_DIM ** -0.5 * math.log2(math.e)


def _silu(x):
    return x / (1.0 + jnp.exp(-x))


def _rms_rows(x, g):
    ms = jnp.mean(x * x, axis=-1, keepdims=True)
    return x * lax.rsqrt(ms + NORM_EPS) * g


def _mod_kernel(cc_ref, w_ref, b_ref, o_ref):
    a = _silu(cc_ref[...])
    o_ref[...] = jnp.dot(a, w_ref[...], preferred_element_type=F32,
                         precision=lax.Precision.HIGHEST) + b_ref[...]


def _modulation(cc, w_mod, b_mod):
    d = w_mod.shape[0]
    n = w_mod.shape[1]
    tn = 768
    return pl.pallas_call(
        _mod_kernel,
        grid=(n // tn,),
        in_specs=[
            pl.BlockSpec((MOD_ROWS, d), lambda j: (0, 0)),
            pl.BlockSpec((d, tn), lambda j: (0, j)),
            pl.BlockSpec((1, tn), lambda j: (0, j)),
        ],
        out_specs=pl.BlockSpec((MOD_ROWS, tn), lambda j: (0, j)),
        out_shape=jax.ShapeDtypeStruct((MOD_ROWS, n), F32),
        name="mod",
    )(cc, w_mod, b_mod)


def _low_mask():
    lane = lax.broadcasted_iota(jnp.int32, (1, LANES), 1)
    return (lane < QK_ROPE_DIM).astype(F32)


def _rope_pair(rr, g_rr, cs):
    pr = rr * g_rr * cs
    return pr + pltpu.roll(pr, QK_ROPE_DIM, axis=1)


def _kv_heads(ckv, kr2, kvlg, w_ukv, gk_ref, cs, k_ref, v_ref):
    low = _low_mask()
    ckv_n = _rms_rows(ckv, kvlg).astype(BF16)
    kv = jnp.dot(ckv_n, w_ukv, preferred_element_type=F32)
    g_nope = gk_ref[:, 0:LANES]
    g_rr = gk_ref[:, LANES:2 * LANES]
    k_rope = _rope_pair(kr2, g_rr, cs)
    ss_rope = jnp.sum(kr2 * kr2 * low, axis=-1, keepdims=True)
    for h in range(MLA_HEADS):
        kn = kv[:, h * HEAD_COLS:h * HEAD_COLS + LANES]
        vv = kv[:, h * HEAD_COLS + LANES:(h + 1) * HEAD_COLS]
        ss = jnp.sum(kn * kn, axis=-1, keepdims=True) + ss_rope
        r = lax.rsqrt(ss * (1.0 / QK_HEAD_DIM) + NORM_EPS)
        k_ref[0, h, :, 0:LANES] = (kn * r * g_nope).astype(BF16)
        k_ref[0, h, :, LANES:QK_HEAD_DIM] = (k_rope * r)[:, 0:QK_ROPE_DIM].astype(BF16)
        v_ref[0, h, :, 0:V_HEAD_DIM] = vv.astype(BF16)
        v_ref[0, h, :, V_HEAD_DIM:] = jnp.ones((vv.shape[0], V_AUG_DIM - V_HEAD_DIM), BF16)


def _proj_kernel(x_ref, xp_ref, xn_ref, mod_ref, ng_ref, w_in_ref, qlg_ref, w_uq_ref,
                 kvlg_ref, w_ukv_ref, gq_ref, gk_ref, cs_ref, w_pool_ref, ps_ref,
                 q_ref, k_ref, v_ref, sga_ref, brp_ref,
                 h_scr, pin_scr, *, tm, seq):
    t = pl.program_id(1)
    nt = pl.num_programs(1)
    d = x_ref.shape[-1]
    shift = mod_ref[0, :, 0:d]
    scale = mod_ref[0, :, d:2 * d]
    a = ng_ref[...] * (1.0 + scale)

    def hnorm(xv):
        ms = jnp.mean(xv * xv, axis=-1, keepdims=True)
        return xv * lax.rsqrt(ms + NORM_EPS) * a + shift

    h_scr[0:HALO, :] = jnp.where(t > 0, hnorm(xp_ref[0]), 0.0).astype(BF16)
    h_scr[HALO:HALO + tm, :] = hnorm(x_ref[0]).astype(BF16)
    h_scr[HALO + tm:, :] = jnp.where(t < nt - 1, hnorm(xn_ref[0]), 0.0).astype(BF16)

    h_main = h_scr[HALO:HALO + tm, :]
    cs = cs_ref[...]
    low = _low_mask()

    u_a = jnp.dot(h_main, w_in_ref[:, C_CQ:C_GA], preferred_element_type=F32)
    cq_n = _rms_rows(u_a[:, C_CQ:C_CKV], qlg_ref[...]).astype(BF16)
    qa = jnp.dot(cq_n, w_uq_ref[...], preferred_element_type=F32)
    gq_nope = gq_ref[:, 0:LANES]
    gq_rr = gq_ref[:, LANES:2 * LANES]
    for h in range(MLA_HEADS):
        qn = qa[:, h * HEAD_COLS:h * HEAD_COLS + LANES]
        rr = qa[:, h * HEAD_COLS + LANES:(h + 1) * HEAD_COLS]
        ss = (jnp.sum(qn * qn, axis=-1, keepdims=True)
              + jnp.sum(rr * rr * low, axis=-1, keepdims=True))
        r = lax.rsqrt(ss * (1.0 / QK_HEAD_DIM) + NORM_EPS) * Q_SCALE
        q_ref[0, h, :, 0:LANES] = (qn * r * gq_nope).astype(BF16)
        q_rope = _rope_pair(rr, gq_rr, cs) * r
        q_ref[0, h, :, LANES:QK_HEAD_DIM] = q_rope[:, 0:QK_ROPE_DIM].astype(BF16)

    _kv_heads(u_a[:, C_CKV:C_KR], u_a[:, C_KR:C_GA], kvlg_ref[...], w_ukv_ref[...],
              gk_ref, cs, k_ref, v_ref)

    u_ga = jnp.dot(h_main, w_in_ref[:, C_GA:C_PIN], preferred_element_type=F32)
    sga_ref[0] = _silu(u_ga).astype(BF16)

    pin_scr[...] = jnp.dot(h_scr[...], w_in_ref[:, C_PIN:C_GP], preferred_element_type=F32)
    u_gp = jnp.dot(h_main, w_in_ref[:, C_GP:C_END], preferred_element_type=F32)
    tok = t * tm + lax.broadcasted_iota(jnp.int32, (tm, LANES), 0)
    for g, w in enumerate(POOL_WINDOWS):
        c0 = g * POOL_GROUP_DIM
        c1 = c0 + POOL_GROUP_DIM
        start = HALO - w // 2
        ws = pin_scr[start:start + tm, c0:c1]
        for j in range(1, w):
            ws = ws + pin_scr[start + j:start + j + tm, c0:c1]
        lo = jnp.maximum(tok - w // 2, 0)
        hi = jnp.minimum(tok - w // 2 + w, seq)
        cnt = (hi - lo).astype(F32)
        pooled = ws / cnt - pin_scr[HALO:HALO + tm, c0:c1]
        y = jnp.dot(pooled.astype(BF16), w_pool_ref[g], preferred_element_type=F32)
        brp_ref[0, :, c0:c1] = (_silu(u_gp[:, c0:c1]) * (y * ps_ref[:, c0:c1])).astype(BF16)


def _proj(x, mod3, norm_g, w_in_r, qlg, w_uq_r, kvlg, w_ukv_b, gq, gk, cs, w_pool_b, ps, *, tm):
    b, s, d = x.shape
    nt = s // tm
    hb = tm // HALO
    nhb = s // HALO
    const2 = lambda bi, ti: (0, 0)
    kern = functools.partial(_proj_kernel, tm=tm, seq=s)
    return pl.pallas_call(
        kern,
        grid=(b, nt),
        in_specs=[
            pl.BlockSpec((1, tm, d), lambda bi, ti: (bi, ti, 0)),
            pl.BlockSpec((1, HALO, d), lambda bi, ti: (bi, jnp.maximum(ti * hb - 1, 0), 0)),
            pl.BlockSpec((1, HALO, d), lambda bi, ti: (bi, jnp.minimum((ti + 1) * hb, nhb - 1), 0)),
            pl.BlockSpec((1, 1, 3 * d), lambda bi, ti: (bi, 0, 0)),
            pl.BlockSpec((1, d), const2),
            pl.BlockSpec((d, C_END), const2),
            pl.BlockSpec((1, Q_LORA_RANK), const2),
            pl.BlockSpec((Q_LORA_RANK, MLA_HEADS * HEAD_COLS), const2),
            pl.BlockSpec((1, KV_LORA_RANK), const2),
            pl.BlockSpec((KV_LORA_RANK, MLA_HEADS * HEAD_COLS), const2),
            pl.BlockSpec((1, 2 * LANES), const2),
            pl.BlockSpec((1, 2 * LANES), const2),
            pl.BlockSpec((tm, LANES), lambda bi, ti: (ti, 0)),
            pl.BlockSpec((len(POOL_WINDOWS), POOL_GROUP_DIM, POOL_GROUP_DIM), lambda bi, ti: (0, 0, 0)),
            pl.BlockSpec((1, D_POOL), const2),
        ],
        out_specs=[
            pl.BlockSpec((1, MLA_HEADS, tm, QK_HEAD_DIM), lambda bi, ti: (bi, 0, ti, 0)),
            pl.BlockSpec((1, MLA_HEADS, tm, QK_HEAD_DIM), lambda bi, ti: (bi, 0, ti, 0)),
            pl.BlockSpec((1, MLA_HEADS, tm, V_AUG_DIM), lambda bi, ti: (bi, 0, ti, 0)),
            pl.BlockSpec((1, tm, D_ATTN), lambda bi, ti: (bi, ti, 0)),
            pl.BlockSpec((1, tm, D_POOL), lambda bi, ti: (bi, ti, 0)),
        ],
        out_shape=[
            jax.ShapeDtypeStruct((b, MLA_HEADS, s, QK_HEAD_DIM), BF16),
            jax.ShapeDtypeStruct((b, MLA_HEADS, s, QK_HEAD_DIM), BF16),
            jax.ShapeDtypeStruct((b, MLA_HEADS, s, V_AUG_DIM), BF16),
            jax.ShapeDtypeStruct((b, s, D_ATTN), BF16),
            jax.ShapeDtypeStruct((b, s, D_POOL), BF16),
        ],
        scratch_shapes=[
            pltpu.VMEM((tm + 2 * HALO, d), BF16),
            pltpu.VMEM((tm + 2 * HALO, D_POOL), F32),
        ],
        compiler_params=pltpu.CompilerParams(
            dimension_semantics=("parallel", "parallel"),
            vmem_limit_bytes=VMEM_LIMIT),
        name="proj",
    )(x, x, x, mod3, norm_g, w_in_r, qlg, w_uq_r, kvlg, w_ukv_b, gq, gk, cs, w_pool_b, ps)


def _ctxkv_kernel(x_ref, mod_ref, ng_ref, w_kv_ref, kvlg_ref, w_ukv_ref, gk_ref, cs_ref,
                  k_ref, v_ref):
    d = x_ref.shape[-1]
    shift = mod_ref[0, :, 0:d]
    scale = mod_ref[0, :, d:2 * d]
    a = ng_ref[...] * (1.0 + scale)
    xv = x_ref[0]
    ms = jnp.mean(xv * xv, axis=-1, keepdims=True)
    hc = (xv * lax.rsqrt(ms + NORM_EPS) * a + shift).astype(BF16)
    u = jnp.dot(hc, w_kv_ref[...], preferred_element_type=F32)
    _kv_heads(u[:, 0:KV_LORA_RANK], u[:, KV_LORA_RANK:], kvlg_ref[...], w_ukv_ref[...],
              gk_ref, cs_ref[...], k_ref, v_ref)


def _ctxkv(ctx, mod3, norm_g, w_kv, kvlg, w_ukv_b, gk, cs_ctx):
    b, lc, d = ctx.shape
    const2 = lambda bi: (0, 0)
    nkv = KV_LORA_RANK + 2 * QK_ROPE_DIM
    return pl.pallas_call(
        _ctxkv_kernel,
        grid=(b,),
        in_specs=[
            pl.BlockSpec((1, lc, d), lambda bi: (bi, 0, 0)),
            pl.BlockSpec((1, 1, 3 * d), lambda bi: (b, 0, 0)),
            pl.BlockSpec((1, d), const2),
            pl.BlockSpec((d, nkv), const2),
            pl.BlockSpec((1, KV_LORA_RANK), const2),
            pl.BlockSpec((KV_LORA_RANK, MLA_HEADS * HEAD_COLS), const2),
            pl.BlockSpec((1, 2 * LANES), const2),
            pl.BlockSpec((lc, LANES), const2),
        ],
        out_specs=[
            pl.BlockSpec((1, MLA_HEADS, lc, QK_HEAD_DIM), lambda bi: (bi, 0, 0, 0)),
            pl.BlockSpec((1, MLA_HEADS, lc, V_AUG_DIM), lambda bi: (bi, 0, 0, 0)),
        ],
        out_shape=[
            jax.ShapeDtypeStruct((b, MLA_HEADS, lc, QK_HEAD_DIM), BF16),
            jax.ShapeDtypeStruct((b, MLA_HEADS, lc, V_AUG_DIM), BF16),
        ],
        compiler_params=pltpu.CompilerParams(
            dimension_semantics=("parallel",), vmem_limit_bytes=VMEM_LIMIT),
        name="ctxkv",
    )(ctx, mod3, norm_g, w_kv, kvlg, w_ukv_b, gk, cs_ctx)


def _lane_tile(a, n):
    return a if n == 1 else jnp.concatenate([a] * n, axis=1)


def _attn_out_kernel(q_ref, k_ref, v_ref, kc_ref, vc_ref, sga_ref, brp_ref, x_ref, mod_ref,
                     w_ref, o_ref, m_scr, acc_scr, attn_scr, *, tk):
    nb, ib, dk = q_ref.shape[2:]
    rows = nb * ib
    nk = k_ref.shape[2] // tk
    dv = V_HEAD_DIM
    d = x_ref.shape[-1]

    def step(h, kb, vb):
        q = q_ref[0, h].reshape(rows, dk)
        s = lax.dot_general(q, kb, (((1,), (1,)), ((), ())), preferred_element_type=F32)
        m_prev = m_scr[h]
        m_next = jnp.maximum(m_prev, jnp.max(s, axis=-1, keepdims=True))
        alpha = jnp.exp2(m_prev - m_next)
        p = jnp.exp2(s - _lane_tile(m_next, s.shape[1] // LANES))
        pv = jnp.dot(p.astype(BF16), vb, preferred_element_type=F32)
        acc_scr[h] = _lane_tile(alpha, V_AUG_DIM // LANES) * acc_scr[h] + pv
        m_scr[h] = m_next

    m_scr[...] = jnp.full(m_scr.shape, -jnp.inf, F32)
    acc_scr[...] = jnp.zeros(acc_scr.shape, F32)
    for h in range(MLA_HEADS):
        step(h, kc_ref[0, h], vc_ref[0, h])
    for j in range(nk):
        for h in range(MLA_HEADS):
            step(h, k_ref[0, h, j * tk:(j + 1) * tk, :], v_ref[0, h, j * tk:(j + 1) * tk, :])

    for h in range(MLA_HEADS):
        attn_scr[h] = acc_scr[h, :, 0:dv] / acc_scr[h, :, dv:2 * dv]
    attn = jnp.concatenate(
        [jnp.concatenate([attn_scr[h, pl.ds(il, nb, stride=ib), :] for il in range(ib)], axis=0)
         for h in range(MLA_HEADS)], axis=1)
    gate = mod_ref[0, :, 2 * d:3 * d]
    br_a = (sga_ref[0].astype(F32) * attn).astype(BF16)
    y = jnp.dot(brp_ref[0], w_ref[D_ATTN:, :], preferred_element_type=F32)
    y = y + jnp.dot(br_a, w_ref[0:D_ATTN, :], preferred_element_type=F32)
    o_ref[0] = x_ref[0] + gate * y


def _attn_out(q, k, v, kc, vc, sga, brp, x, mod3, w_out_b, *, ib, tk):
    b, h, s, dk = q.shape
    _, _, d = x.shape
    lc = kc.shape[2]
    dva = v.shape[-1]
    nb = s // Q_BLOCK
    rows = nb * ib
    q5 = q.reshape(b, h, nb, Q_BLOCK, dk)
    kern = functools.partial(_attn_out_kernel, tk=tk)
    resident = dict(pipeline_mode=pl.Buffered(1))
    row_tile = lambda bi, ti: (bi, ti, 0)
    return pl.pallas_call(
        kern,
        grid=(b, Q_BLOCK // ib),
        in_specs=[
            pl.BlockSpec((1, h, nb, ib, dk), lambda bi, ti: (bi, 0, 0, ti, 0)),
            pl.BlockSpec((1, h, s, dk), lambda bi, ti: (bi, 0, 0, 0), **resident),
            pl.BlockSpec((1, h, s, dva), lambda bi, ti: (bi, 0, 0, 0), **resident),
            pl.BlockSpec((1, h, lc, dk), lambda bi, ti: (bi, 0, 0, 0), **resident),
            pl.BlockSpec((1, h, lc, dva), lambda bi, ti: (bi, 0, 0, 0), **resident),
            pl.BlockSpec((1, rows, D_ATTN), row_tile),
            pl.BlockSpec((1, rows, D_POOL), row_tile),
            pl.BlockSpec((1, rows, d), row_tile),
            pl.BlockSpec((1, 1, 3 * d), lambda bi, ti: (bi, 0, 0)),
            pl.BlockSpec((D_ATTN + D_POOL, d), lambda bi, ti: (0, 0), **resident),
        ],
        out_specs=pl.BlockSpec((1, rows, d), row_tile),
        out_shape=jax.ShapeDtypeStruct((b, s, d), F32),
        scratch_shapes=[
            pltpu.VMEM((h, rows, LANES), F32),
            pltpu.VMEM((h, rows, dva), F32),
            pltpu.VMEM((h, rows, V_HEAD_DIM), F32),
        ],
        compiler_params=pltpu.CompilerParams(
            dimension_semantics=("parallel", "arbitrary"),
            vmem_limit_bytes=VMEM_LIMIT),
        name="attn_out",
    )(q5, k, v, kc, vc, sga, brp, x, mod3, w_out_b)


def _rot_perm():
    q = QK_ROPE_DIM // 4
    perm = np.concatenate([np.arange(q, 2 * q), np.arange(0, q),
                           np.arange(3 * q, 4 * q), np.arange(2 * q, 3 * q)])
    sign = np.concatenate([-np.ones(q), np.ones(q), -np.ones(q), np.ones(q)]).astype(np.float32)
    return perm, sign


def _rope_table(seq):
    rows = seq // GRID_W
    row = jnp.repeat(jnp.arange(rows, dtype=F32), GRID_W)
    col = jnp.tile(jnp.arange(GRID_W, dtype=F32), rows)
    n_freq = QK_ROPE_DIM // 4
    inv = ROPE_BASE ** (-jnp.arange(n_freq, dtype=F32) / n_freq)
    ang_r = row[:, None] * inv
    ang_c = col[:, None] * inv
    ang = jnp.concatenate([ang_r, ang_r, ang_c, ang_c], axis=-1)
    return jnp.concatenate([jnp.cos(ang), jnp.sin(ang)], axis=-1)


def kernel(x, c, ctx, c_ctx, w_mod, b_mod, norm_g, w_in, q_lora_g, w_uq, kv_lora_g, w_ukv,
           q_norm_g, k_norm_g, w_pool, pool_scale, w_out):
    b, s, d = x.shape
    lc = ctx.shape[1]
    depth = w_mod.shape[0]
    assert depth == 1, "single-layer block"
    perm, sign = _rot_perm()
    l = 0

    cc = jnp.concatenate([c, c_ctx[None, :], jnp.zeros((MOD_ROWS - b - 1, d), F32)], axis=0)
    mod = _modulation(cc, w_mod[l], b_mod[l][None, :])
    mod3 = mod[:, None, :]

    wi = w_in[l]
    o_cq, o_ckv, o_kr = 0, Q_LORA_RANK, Q_LORA_RANK + KV_LORA_RANK
    o_ga = o_kr + QK_ROPE_DIM
    o_pin = o_ga + D_ATTN
    o_gp = o_pin + D_POOL
    w_kr = wi[:, o_kr:o_ga]
    w_in_r = jnp.concatenate(
        [wi[:, o_cq:o_kr], w_kr, w_kr[:, perm] * sign, wi[:, o_ga:o_pin],
         wi[:, o_pin:o_gp], wi[:, o_gp:]], axis=1).astype(BF16)
    wq = w_uq[l].reshape(Q_LORA_RANK, MLA_HEADS, QK_HEAD_DIM)
    wq_rope = wq[:, :, QK_NOPE_DIM:]
    w_uq_r = jnp.concatenate([wq, wq_rope[:, :, perm] * sign], axis=-1)
    w_uq_r = w_uq_r.reshape(Q_LORA_RANK, MLA_HEADS * HEAD_COLS).astype(BF16)
    w_ukv_b = w_ukv[l].astype(BF16)
    w_pool_b = w_pool[l].astype(BF16)
    w_out_b = w_out[l].astype(BF16)

    def gain_row(g):
        return jnp.concatenate([g, g[QK_NOPE_DIM:][perm]])[None, :]

    gq = gain_row(q_norm_g[l])
    gk = gain_row(k_norm_g[l])
    cs = _rope_table(s)
    cs_ctx = jnp.concatenate([jnp.ones((lc, QK_ROPE_DIM), F32),
                              jnp.zeros((lc, QK_ROPE_DIM), F32)], axis=1)

    ng = norm_g[l][None, :]
    qlg = q_lora_g[l][None, :]
    kvlg = kv_lora_g[l][None, :]
    ps = pool_scale[l][None, :]

    q, k, v, sga, brp = _proj(x, mod3, ng, w_in_r, qlg, w_uq_r, kvlg, w_ukv_b, gq, gk, cs,
                              w_pool_b, ps, tm=512)
    kc, vc = _ctxkv(ctx, mod3, ng, w_in_r[:, C_CKV:C_GA], kvlg, w_ukv_b, gk, cs_ctx)
    return _attn_out(q, k, v, kc, vc, sga, brp, x, mod3, w_out_b, ib=16, tk=512)
```

```python
import functools
import math

import jax
import jax.numpy as jnp
import numpy as np
from jax import lax
from jax.experimental import pallas as pl
from jax.experimental.pallas import tpu as pltpu

F32 = jnp.float32
BF16 = jnp.bfloat16

GRID_W = 64
MLA_HEADS = 4
QK_NOPE_DIM = 128
QK_ROPE_DIM = 64
QK_HEAD_DIM = QK_NOPE_DIM + QK_ROPE_DIM
V_HEAD_DIM = 128
V_AUG_DIM = 2 * V_HEAD_DIM
Q_LORA_RANK = 256
KV_LORA_RANK = 128
ROPE_BASE = 10000.0
POOL_WINDOWS = (2, 4, 8, 16)
POOL_GROUP_DIM = 128
NORM_EPS = 1e-6
Q_BLOCK = 128
D_ATTN = MLA_HEADS * V_HEAD_DIM
D_POOL = len(POOL_WINDOWS) * POOL_GROUP_DIM

LANES = 128
HEAD_COLS = 2 * LANES
HALO = 16
EDGE = 8
assert max(POOL_WINDOWS) // 2 <= EDGE <= HALO
MOD_ROWS = 16

C_CQ = 0
C_CKV = C_CQ + Q_LORA_RANK
C_KR = C_CKV + KV_LORA_RANK
C_GA = C_KR + 2 * QK_ROPE_DIM
C_PIN = C_GA + D_ATTN
C_GP = C_PIN + D_POOL
C_END = C_GP + D_POOL

VMEM_LIMIT = 56 * 1024 * 1024

Q_SCALE = QK_HEAD_DIM ** -0.5 * math.log2(math.e)


def _silu(x):
    hx = 0.5 * x
    return hx + hx * jnp.tanh(hx)


def _window_sum(p, w):
    n = p.shape[0]
    up = lambda a, k: pltpu.roll(a, n - k, axis=0)
    down = lambda a, k: pltpu.roll(a, k, axis=0)
    if w == 2:
        return p + down(p, 1)
    acc = p + up(p, 1)
    span = 2
    while span < w // 2:
        acc = acc + up(acc, span)
        span *= 2
    return acc + down(acc, w // 2)


def _rms_rows(x, g):
    ms = jnp.mean(x * x, axis=-1, keepdims=True)
    return x * lax.rsqrt(ms + NORM_EPS) * g


def _mod_kernel(cc_ref, w_ref, b_ref, o_ref):
    a = _silu(cc_ref[...])
    o_ref[...] = jnp.dot(a, w_ref[...], preferred_element_type=F32,
                         precision=lax.Precision.HIGHEST) + b_ref[...]


def _modulation(cc, w_mod, b_mod):
    d = w_mod.shape[0]
    n = w_mod.shape[1]
    tn = 768
    return pl.pallas_call(
        _mod_kernel,
        grid=(n // tn,),
        in_specs=[
            pl.BlockSpec((MOD_ROWS, d), lambda j: (0, 0)),
            pl.BlockSpec((d, tn), lambda j: (0, j)),
            pl.BlockSpec((1, tn), lambda j: (0, j)),
        ],
        out_specs=pl.BlockSpec((MOD_ROWS, tn), lambda j: (0, j)),
        out_shape=jax.ShapeDtypeStruct((MOD_ROWS, n), F32),
        name="mod",
    )(cc, w_mod, b_mod)


HEAD_SS_EPS = QK_HEAD_DIM * NORM_EPS


def _rope_pair(rr, csg):
    pr = rr * csg
    return pr + pltpu.roll(pr, QK_ROPE_DIM, axis=1)


def _half_rope_sq(rr):
    return 0.5 * (rr * rr)


def _kv_heads(ckv, kr2, kvlg, w_ukv, gk_ref, cs, k_ref, v_ref, rows):
    ckv_n = _rms_rows(ckv, kvlg).astype(BF16)
    kv = jnp.dot(ckv_n, w_ukv, preferred_element_type=F32)
    g_nope = gk_ref[:, 0:LANES]
    k_rope = _rope_pair(kr2, cs * gk_ref[:, LANES:2 * LANES])
    sq_rope = _half_rope_sq(kr2)
    for h in range(MLA_HEADS):
        kn = kv[:, h * HEAD_COLS:h * HEAD_COLS + LANES]
        vv = kv[:, h * HEAD_COLS + LANES:(h + 1) * HEAD_COLS]
        ss = jnp.sum(kn * kn + sq_rope, axis=-1, keepdims=True)
        r = lax.rsqrt(ss + HEAD_SS_EPS)
        k_ref[0, h, rows, 0:LANES] = (kn * r * g_nope).astype(BF16)
        k_ref[0, h, rows, LANES:QK_HEAD_DIM] = (k_rope * r)[:, 0:QK_ROPE_DIM].astype(BF16)
        v_ref[0, h, rows, 0:V_HEAD_DIM] = vv.astype(BF16)
        v_ref[0, h, rows, V_HEAD_DIM:] = jnp.ones((vv.shape[0], V_AUG_DIM - V_HEAD_DIM), BF16)


def _proj_kernel(x_ref, xp_ref, xn_ref, mod_ref, ng_ref, w_in_ref, qlg_ref, w_uq_ref,
                 kvlg_ref, w_ukv_ref, gq_ref, gk_ref, cs_ref, w_pool_ref, ps_ref,
                 q_ref, k_ref, v_ref, sga_ref, brp_ref,
                 h_scr, *, tm, seq, nsub):
    t = pl.program_id(1)
    nt = pl.num_programs(1)
    d = x_ref.shape[-1]
    shift = mod_ref[0, :, 0:d]
    scale = mod_ref[0, :, d:2 * d]
    a = ng_ref[...] * (1.0 + scale)

    def hnorm(xv):
        ms = jnp.mean(xv * xv, axis=-1, keepdims=True)
        return xv * lax.rsqrt(ms + NORM_EPS) * a + shift

    nh = tm // nsub
    h_scr[0:HALO, :] = jnp.where(t > 0, hnorm(xp_ref[0]), 0.0).astype(BF16)
    for i in range(nsub):
        h_scr[HALO + i * nh:HALO + (i + 1) * nh, :] = hnorm(x_ref[0, i * nh:(i + 1) * nh, :]).astype(BF16)
    h_scr[HALO + tm:, :] = jnp.where(t < nt - 1, hnorm(xn_ref[0]), 0.0).astype(BF16)

    cs_all = cs_ref[...]
    gq_nope = gq_ref[:, 0:LANES]
    gq_rr = gq_ref[:, LANES:2 * LANES]

    for i in range(nsub):
        rows = slice(i * nh, (i + 1) * nh)
        h_main = h_scr[HALO + i * nh:HALO + (i + 1) * nh, :]
        cs = cs_all[rows]

        u_a = jnp.dot(h_main, w_in_ref[:, C_CQ:C_GA], preferred_element_type=F32)
        cq_n = _rms_rows(u_a[:, C_CQ:C_CKV], qlg_ref[...]).astype(BF16)
        qa = jnp.dot(cq_n, w_uq_ref[...], preferred_element_type=F32)
        csq = cs * gq_rr
        for h in range(MLA_HEADS):
            qn = qa[:, h * HEAD_COLS:h * HEAD_COLS + LANES]
            rr = qa[:, h * HEAD_COLS + LANES:(h + 1) * HEAD_COLS]
            ss = jnp.sum(qn * qn + _half_rope_sq(rr), axis=-1, keepdims=True)
            r = lax.rsqrt(ss + HEAD_SS_EPS)
            q_ref[0, h, rows, 0:LANES] = (qn * r * gq_nope).astype(BF16)
            q_rope = _rope_pair(rr, csq) * r
            q_ref[0, h, rows, LANES:QK_HEAD_DIM] = q_rope[:, 0:QK_ROPE_DIM].astype(BF16)

        _kv_heads(u_a[:, C_CKV:C_KR], u_a[:, C_KR:C_GA], kvlg_ref[...], w_ukv_ref[...],
                  gk_ref, cs, k_ref, v_ref, rows)

        u_ga = jnp.dot(h_main, w_in_ref[:, C_GA:C_PIN], preferred_element_type=F32)
        sga_ref[0, rows, :] = _silu(u_ga).astype(BF16)

        pin = jnp.dot(h_scr[i * nh:(i + 1) * nh + 2 * HALO, :], w_in_ref[:, C_PIN:C_GP],
                      preferred_element_type=F32)
        u_gp = jnp.dot(h_main, w_in_ref[:, C_GP:C_END], preferred_element_type=F32)
        tok_head = t * tm + i * nh + lax.broadcasted_iota(jnp.int32, (EDGE, LANES), 0)
        tok_tail = tok_head + (nh - EDGE)
        for g, w in enumerate(POOL_WINDOWS):
            c0 = g * POOL_GROUP_DIM
            c1 = c0 + POOL_GROUP_DIM
            pg = pin[:, c0:c1]
            ws = _window_sum(pg, w)[HALO:HALO + nh]

            def inv_count(tok):
                lo = jnp.maximum(tok - w // 2, 0)
                hi = jnp.minimum(tok - w // 2 + w, seq)
                return 1.0 / (hi - lo).astype(F32)

            inv_cnt = jnp.concatenate(
                [inv_count(tok_head), jnp.full((nh - 2 * EDGE, LANES), 1.0 / w, F32),
                 inv_count(tok_tail)], axis=0)
            pooled = ws * inv_cnt - pg[HALO:HALO + nh]
            y = jnp.dot(pooled.astype(BF16), w_pool_ref[g], preferred_element_type=F32)
            brp_ref[0, rows, c0:c1] = (_silu(u_gp[:, c0:c1]) * (y * ps_ref[:, c0:c1])).astype(BF16)


def _proj(x, mod3, norm_g, w_in_r, qlg, w_uq_r, kvlg, w_ukv_b, gq, gk, cs, w_pool_b, ps, *,
          tm, nsub):
    b, s, d = x.shape
    nt = s // tm
    hb = tm // HALO
    nhb = s // HALO
    const2 = lambda bi, ti: (0, 0)
    kern = functools.partial(_proj_kernel, tm=tm, seq=s, nsub=nsub)
    return pl.pallas_call(
        kern,
        grid=(b, nt),
        in_specs=[
            pl.BlockSpec((1, tm, d), lambda bi, ti: (bi, ti, 0)),
            pl.BlockSpec((1, HALO, d), lambda bi, ti: (bi, jnp.maximum(ti * hb - 1, 0), 0)),
            pl.BlockSpec((1, HALO, d), lambda bi, ti: (bi, jnp.minimum((ti + 1) * hb, nhb - 1), 0)),
            pl.BlockSpec((1, 1, 3 * d), lambda bi, ti: (bi, 0, 0)),
            pl.BlockSpec((1, d), const2),
            pl.BlockSpec((d, C_END), const2),
            pl.BlockSpec((1, Q_LORA_RANK), const2),
            pl.BlockSpec((Q_LORA_RANK, MLA_HEADS * HEAD_COLS), const2),
            pl.BlockSpec((1, KV_LORA_RANK), const2),
            pl.BlockSpec((KV_LORA_RANK, MLA_HEADS * HEAD_COLS), const2),
            pl.BlockSpec((1, 2 * LANES), const2),
            pl.BlockSpec((1, 2 * LANES), const2),
            pl.BlockSpec((tm, LANES), lambda bi, ti: (ti, 0)),
            pl.BlockSpec((len(POOL_WINDOWS), POOL_GROUP_DIM, POOL_GROUP_DIM), lambda bi, ti: (0, 0, 0)),
            pl.BlockSpec((1, D_POOL), const2),
        ],
        out_specs=[
            pl.BlockSpec((1, MLA_HEADS, tm, QK_HEAD_DIM), lambda bi, ti: (bi, 0, ti, 0)),
            pl.BlockSpec((1, MLA_HEADS, tm, QK_HEAD_DIM), lambda bi, ti: (bi, 0, ti, 0)),
            pl.BlockSpec((1, MLA_HEADS, tm, V_AUG_DIM), lambda bi, ti: (bi, 0, ti, 0)),
            pl.BlockSpec((1, tm, D_ATTN), lambda bi, ti: (bi, ti, 0)),
            pl.BlockSpec((1, tm, D_POOL), lambda bi, ti: (bi, ti, 0)),
        ],
        out_shape=[
            jax.ShapeDtypeStruct((b, MLA_HEADS, s, QK_HEAD_DIM), BF16),
            jax.ShapeDtypeStruct((b, MLA_HEADS, s, QK_HEAD_DIM), BF16),
            jax.ShapeDtypeStruct((b, MLA_HEADS, s, V_AUG_DIM), BF16),
            jax.ShapeDtypeStruct((b, s, D_ATTN), BF16),
            jax.ShapeDtypeStruct((b, s, D_POOL), BF16),
        ],
        scratch_shapes=[
            pltpu.VMEM((tm + 2 * HALO, d), BF16),
        ],
        compiler_params=pltpu.CompilerParams(
            dimension_semantics=("parallel", "parallel"),
            vmem_limit_bytes=VMEM_LIMIT),
        name="proj",
    )(x, x, x, mod3, norm_g, w_in_r, qlg, w_uq_r, kvlg, w_ukv_b, gq, gk, cs, w_pool_b, ps)


def _ctxkv_kernel(x_ref, mod_ref, ng_ref, w_kv_ref, kvlg_ref, w_ukv_ref, gk_ref, cs_ref,
                  k_ref, v_ref):
    d = x_ref.shape[-1]
    shift = mod_ref[0, :, 0:d]
    scale = mod_ref[0, :, d:2 * d]
    a = ng_ref[...] * (1.0 + scale)
    xv = x_ref[0]
    ms = jnp.mean(xv * xv, axis=-1, keepdims=True)
    hc = (xv * lax.rsqrt(ms + NORM_EPS) * a + shift).astype(BF16)
    u = jnp.dot(hc, w_kv_ref[...], preferred_element_type=F32)
    _kv_heads(u[:, 0:KV_LORA_RANK], u[:, KV_LORA_RANK:], kvlg_ref[...], w_ukv_ref[...],
              gk_ref, cs_ref[...], k_ref, v_ref, slice(None))


def _ctxkv(ctx, mod3, norm_g, w_kv, kvlg, w_ukv_b, gk, cs_ctx):
    b, lc, d = ctx.shape
    const2 = lambda bi: (0, 0)
    nkv = KV_LORA_RANK + 2 * QK_ROPE_DIM
    return pl.pallas_call(
        _ctxkv_kernel,
        grid=(b,),
        in_specs=[
            pl.BlockSpec((1, lc, d), lambda bi: (bi, 0, 0)),
            pl.BlockSpec((1, 1, 3 * d), lambda bi: (b, 0, 0)),
            pl.BlockSpec((1, d), const2),
            pl.BlockSpec((d, nkv), const2),
            pl.BlockSpec((1, KV_LORA_RANK), const2),
            pl.BlockSpec((KV_LORA_RANK, MLA_HEADS * HEAD_COLS), const2),
            pl.BlockSpec((1, 2 * LANES), const2),
            pl.BlockSpec((lc, LANES), const2),
        ],
        out_specs=[
            pl.BlockSpec((1, MLA_HEADS, lc, QK_HEAD_DIM), lambda bi: (bi, 0, 0, 0)),
            pl.BlockSpec((1, MLA_HEADS, lc, V_AUG_DIM), lambda bi: (bi, 0, 0, 0)),
        ],
        out_shape=[
            jax.ShapeDtypeStruct((b, MLA_HEADS, lc, QK_HEAD_DIM), BF16),
            jax.ShapeDtypeStruct((b, MLA_HEADS, lc, V_AUG_DIM), BF16),
        ],
        compiler_params=pltpu.CompilerParams(
            dimension_semantics=("parallel",), vmem_limit_bytes=VMEM_LIMIT),
        name="ctxkv",
    )(ctx, mod3, norm_g, w_kv, kvlg, w_ukv_b, gk, cs_ctx)


def _lane_tile(a, n):
    return a if n == 1 else jnp.concatenate([a] * n, axis=1)


def _attn_out_kernel(q_ref, k_ref, v_ref, kc_ref, vc_ref, sga_ref, brp_ref, x_ref, mod_ref,
                     w_ref, o_ref, m_scr, acc_scr, attn_scr, *, tk):
    nb, ib, dk = q_ref.shape[2:]
    rows = nb * ib
    nk = k_ref.shape[2] // tk
    dv = V_HEAD_DIM
    d = x_ref.shape[-1]

    def step(h, kb, vb):
        q = q_ref[0, h].reshape(rows, dk)
        s = lax.dot_general(q, kb, (((1,), (1,)), ((), ())), preferred_element_type=F32)
        m_prev = m_scr[h]
        m_next = jnp.maximum(m_prev, jnp.max(s, axis=-1, keepdims=True))
        alpha = jnp.exp2(m_prev - m_next)
        p = jnp.exp2(s - _lane_tile(m_next, s.shape[1] // LANES))
        pv = jnp.dot(p.astype(BF16), vb, preferred_element_type=F32)
        acc_scr[h] = _lane_tile(alpha, V_AUG_DIM // LANES) * acc_scr[h] + pv
        m_scr[h] = m_next

    m_scr[...] = jnp.full(m_scr.shape, -jnp.inf, F32)
    acc_scr[...] = jnp.zeros(acc_scr.shape, F32)
    for h in range(MLA_HEADS):
        step(h, kc_ref[0, h], vc_ref[0, h])
    for j in range(nk):
        for h in range(MLA_HEADS):
            step(h, k_ref[0, h, j * tk:(j + 1) * tk, :], v_ref[0, h, j * tk:(j + 1) * tk, :])

    for h in range(MLA_HEADS):
        attn_scr[h] = acc_scr[h, :, 0:dv] / acc_scr[h, :, dv:2 * dv]
    attn = jnp.concatenate(
        [jnp.concatenate([attn_scr[h, pl.ds(il, nb, stride=ib), :] for il in range(ib)], axis=0)
         for h in range(MLA_HEADS)], axis=1)
    gate = mod_ref[0, :, 2 * d:3 * d]
    br_a = (sga_ref[0].astype(F32) * attn).astype(BF16)
    y = jnp.dot(brp_ref[0], w_ref[D_ATTN:, :], preferred_element_type=F32)
    y = y + jnp.dot(br_a, w_ref[0:D_ATTN, :], preferred_element_type=F32)
    o_ref[0] = x_ref[0] + gate * y


def _attn_out(q, k, v, kc, vc, sga, brp, x, mod3, w_out_b, *, ib, tk):
    b, h, s, dk = q.shape
    _, _, d = x.shape
    lc = kc.shape[2]
    dva = v.shape[-1]
    nb = s // Q_BLOCK
    rows = nb * ib
    q5 = q.reshape(b, h, nb, Q_BLOCK, dk)
    kern = functools.partial(_attn_out_kernel, tk=tk)
    resident = dict(pipeline_mode=pl.Buffered(1))
    row_tile = lambda bi, ti: (bi, ti, 0)
    return pl.pallas_call(
        kern,
        grid=(b, Q_BLOCK // ib),
        in_specs=[
            pl.BlockSpec((1, h, nb, ib, dk), lambda bi, ti: (bi, 0, 0, ti, 0)),
            pl.BlockSpec((1, h, s, dk), lambda bi, ti: (bi, 0, 0, 0), **resident),
            pl.BlockSpec((1, h, s, dva), lambda bi, ti: (bi, 0, 0, 0), **resident),
            pl.BlockSpec((1, h, lc, dk), lambda bi, ti: (bi, 0, 0, 0), **resident),
            pl.BlockSpec((1, h, lc, dva), lambda bi, ti: (bi, 0, 0, 0), **resident),
            pl.BlockSpec((1, rows, D_ATTN), row_tile),
            pl.BlockSpec((1, rows, D_POOL), row_tile),
            pl.BlockSpec((1, rows, d), row_tile),
            pl.BlockSpec((1, 1, 3 * d), lambda bi, ti: (bi, 0, 0)),
            pl.BlockSpec((D_ATTN + D_POOL, d), lambda bi, ti: (0, 0), **resident),
        ],
        out_specs=pl.BlockSpec((1, rows, d), row_tile),
        out_shape=jax.ShapeDtypeStruct((b, s, d), F32),
        scratch_shapes=[
            pltpu.VMEM((h, rows, LANES), F32),
            pltpu.VMEM((h, rows, dva), F32),
            pltpu.VMEM((h, rows, V_HEAD_DIM), F32),
        ],
        compiler_params=pltpu.CompilerParams(
            dimension_semantics=("parallel", "arbitrary"),
            vmem_limit_bytes=VMEM_LIMIT),
        name="attn_out",
    )(q5, k, v, kc, vc, sga, brp, x, mod3, w_out_b)


def _rot_perm():
    q = QK_ROPE_DIM // 4
    perm = np.concatenate([np.arange(q, 2 * q), np.arange(0, q),
                           np.arange(3 * q, 4 * q), np.arange(2 * q, 3 * q)])
    sign = np.concatenate([-np.ones(q), np.ones(q), -np.ones(q), np.ones(q)]).astype(np.float32)
    return perm, sign


def _rope_table(seq):
    rows = seq // GRID_W
    row = jnp.repeat(jnp.arange(rows, dtype=F32), GRID_W)
    col = jnp.tile(jnp.arange(GRID_W, dtype=F32), rows)
    n_freq = QK_ROPE_DIM // 4
    inv = ROPE_BASE ** (-jnp.arange(n_freq, dtype=F32) / n_freq)
    ang_r = row[:, None] * inv
    ang_c = col[:, None] * inv
    ang = jnp.concatenate([ang_r, ang_r, ang_c, ang_c], axis=-1)
    return jnp.concatenate([jnp.cos(ang), jnp.sin(ang)], axis=-1)


def kernel(x, c, ctx, c_ctx, w_mod, b_mod, norm_g, w_in, q_lora_g, w_uq, kv_lora_g, w_ukv,
           q_norm_g, k_norm_g, w_pool, pool_scale, w_out):
    b, s, d = x.shape
    lc = ctx.shape[1]
    depth = w_mod.shape[0]
    assert depth == 1, "single-layer block"
    perm, sign = _rot_perm()
    l = 0

    cc = jnp.concatenate([c, c_ctx[None, :], jnp.zeros((MOD_ROWS - b - 1, d), F32)], axis=0)
    mod = _modulation(cc, w_mod[l], b_mod[l][None, :])
    mod3 = mod[:, None, :]

    wi = w_in[l]
    o_cq, o_ckv, o_kr = 0, Q_LORA_RANK, Q_LORA_RANK + KV_LORA_RANK
    o_ga = o_kr + QK_ROPE_DIM
    o_pin = o_ga + D_ATTN
    o_gp = o_pin + D_POOL
    w_kr = wi[:, o_kr:o_ga]
    w_in_r = jnp.concatenate(
        [wi[:, o_cq:o_kr], w_kr, w_kr[:, perm] * sign, wi[:, o_ga:o_pin],
         wi[:, o_pin:o_gp], wi[:, o_gp:]], axis=1).astype(BF16)
    wq = w_uq[l].reshape(Q_LORA_RANK, MLA_HEADS, QK_HEAD_DIM)
    wq_rope = wq[:, :, QK_NOPE_DIM:]
    w_uq_r = jnp.concatenate([wq, wq_rope[:, :, perm] * sign], axis=-1)
    w_uq_r = w_uq_r.reshape(Q_LORA_RANK, MLA_HEADS * HEAD_COLS).astype(BF16)
    w_ukv_b = w_ukv[l].astype(BF16)
    w_pool_b = w_pool[l].astype(BF16)
    w_out_b = w_out[l].astype(BF16)

    def gain_row(g, const):
        return (jnp.concatenate([g, g[QK_NOPE_DIM:][perm]]) * const)[None, :]

    gq = gain_row(q_norm_g[l], Q_SCALE * math.sqrt(QK_HEAD_DIM))
    gk = gain_row(k_norm_g[l], math.sqrt(QK_HEAD_DIM))
    cs = _rope_table(s)
    cs_ctx = jnp.concatenate([jnp.ones((lc, QK_ROPE_DIM), F32),
                              jnp.zeros((lc, QK_ROPE_DIM), F32)], axis=1)

    ng = norm_g[l][None, :]
    qlg = q_lora_g[l][None, :]
    kvlg = kv_lora_g[l][None, :]
    ps = pool_scale[l][None, :]

    q, k, v, sga, brp = _proj(x, mod3, ng, w_in_r, qlg, w_uq_r, kvlg, w_ukv_b, gq, gk, cs,
                              w_pool_b, ps, tm=1024, nsub=4)
    kc, vc = _ctxkv(ctx, mod3, ng, w_in_r[:, C_CKV:C_GA], kvlg, w_ukv_b, gk, cs_ctx)
    return _attn_out(q, k, v, kc, vc, sga, brp, x, mod3, w_out_b, ib=16, tk=512)
```

```python
import functools
import math

import jax
import jax.numpy as jnp
import numpy as np
from jax import lax
from jax.experimental import pallas as pl
from jax.experimental.pallas import tpu as pltpu

F32 = jnp.float32
BF16 = jnp.bfloat16

GRID_W = 64
MLA_HEADS = 4
QK_NOPE_DIM = 128
QK_ROPE_DIM = 64
QK_HEAD_DIM = QK_NOPE_DIM + QK_ROPE_DIM
V_HEAD_DIM = 128
V_AUG_DIM = 2 * V_HEAD_DIM
Q_LORA_RANK = 256
KV_LORA_RANK = 128
ROPE_BASE = 10000.0
POOL_WINDOWS = (2, 4, 8, 16)
POOL_GROUP_DIM = 128
NORM_EPS = 1e-6
Q_BLOCK = 128
D_ATTN = MLA_HEADS * V_HEAD_DIM
D_POOL = len(POOL_WINDOWS) * POOL_GROUP_DIM

LANES = 128
HEAD_COLS = 2 * LANES
HALO = 16
EDGE = 8
assert max(POOL_WINDOWS) // 2 <= EDGE <= HALO
MOD_ROWS = 16

C_CQ = 0
C_CKV = C_CQ + Q_LORA_RANK
C_KR = C_CKV + KV_LORA_RANK
C_GA = C_KR + 2 * QK_ROPE_DIM
C_PIN = C_GA + D_ATTN
C_GP = C_PIN + D_POOL
C_END = C_GP + D_POOL

VMEM_LIMIT = 56 * 1024 * 1024

Q_SCALE = QK_HEAD_DIM ** -0.5 * math.log2(math.e)


def _silu(x):
    hx = 0.5 * x
    return hx + hx * jnp.tanh(hx)


def _window_sum(p, w):
    n = p.shape[0]
    up = lambda a, k: pltpu.roll(a, n - k, axis=0)
    down = lambda a, k: pltpu.roll(a, k, axis=0)
    if w == 2:
        return p + down(p, 1)
    acc = p + up(p, 1)
    span = 2
    while span < w // 2:
        acc = acc + up(acc, span)
        span *= 2
    return acc + down(acc, w // 2)


def _rms_rows(x, g):
    ms = jnp.mean(x * x, axis=-1, keepdims=True)
    return x * lax.rsqrt(ms + NORM_EPS) * g


def _mod_kernel(cc_ref, w_ref, b_ref, o_ref):
    a = _silu(cc_ref[...])
    o_ref[...] = jnp.dot(a, w_ref[...], preferred_element_type=F32,
                         precision=lax.Precision.HIGHEST) + b_ref[...]


def _modulation(cc, w_mod, b_mod):
    d = w_mod.shape[0]
    n = w_mod.shape[1]
    tn = 768
    return pl.pallas_call(
        _mod_kernel,
        grid=(n // tn,),
        in_specs=[
            pl.BlockSpec((MOD_ROWS, d), lambda j: (0, 0)),
            pl.BlockSpec((d, tn), lambda j: (0, j)),
            pl.BlockSpec((1, tn), lambda j: (0, j)),
        ],
        out_specs=pl.BlockSpec((MOD_ROWS, tn), lambda j: (0, j)),
        out_shape=jax.ShapeDtypeStruct((MOD_ROWS, n), F32),
        name="mod",
    )(cc, w_mod, b_mod)


HEAD_SS_EPS = QK_HEAD_DIM * NORM_EPS


def _rope_pair(rr, csg):
    pr = rr * csg
    return pr + pltpu.roll(pr, QK_ROPE_DIM, axis=1)


def _half_rope_sq(rr):
    return 0.5 * (rr * rr)


def _kv_heads(ckv, kr2, kvlg, w_ukv, gk_ref, cs, k_ref, v_ref, rows):
    ckv_n = _rms_rows(ckv, kvlg).astype(BF16)
    kv = jnp.dot(ckv_n, w_ukv, preferred_element_type=F32)
    g_nope = gk_ref[:, 0:LANES]
    k_rope = _rope_pair(kr2, cs * gk_ref[:, LANES:2 * LANES])
    sq_rope = _half_rope_sq(kr2)
    for h in range(MLA_HEADS):
        kn = kv[:, h * HEAD_COLS:h * HEAD_COLS + LANES]
        vv = kv[:, h * HEAD_COLS + LANES:(h + 1) * HEAD_COLS]
        ss = jnp.sum(kn * kn + sq_rope, axis=-1, keepdims=True)
        r = lax.rsqrt(ss + HEAD_SS_EPS)
        k_ref[0, h, rows, 0:LANES] = (kn * r * g_nope).astype(BF16)
        k_ref[0, h, rows, LANES:QK_HEAD_DIM] = (k_rope * r)[:, 0:QK_ROPE_DIM].astype(BF16)
        v_ref[0, h, rows, 0:V_HEAD_DIM] = vv.astype(BF16)
        v_ref[0, h, rows, V_HEAD_DIM:] = jnp.ones((vv.shape[0], V_AUG_DIM - V_HEAD_DIM), BF16)


def _proj_kernel(x_ref, xp_ref, xn_ref, mod_ref, ng_ref, w_in_ref, qlg_ref, w_uq_ref,
                 kvlg_ref, w_ukv_ref, gq_ref, gk_ref, cs_ref, w_pool_ref, ps_ref,
                 q_ref, k_ref, v_ref, sga_ref, brp_ref,
                 h_scr, *, tm, seq, nsub):
    t = pl.program_id(1)
    nt = pl.num_programs(1)
    d = x_ref.shape[-1]
    shift = mod_ref[0, :, 0:d]
    scale = mod_ref[0, :, d:2 * d]
    a = ng_ref[...] * (1.0 + scale)

    def hnorm(xv):
        ms = jnp.mean(xv * xv, axis=-1, keepdims=True)
        return xv * lax.rsqrt(ms + NORM_EPS) * a + shift

    nh = tm // nsub
    h_scr[0:HALO, :] = jnp.where(t > 0, hnorm(xp_ref[0]), 0.0).astype(BF16)
    for i in range(nsub):
        h_scr[HALO + i * nh:HALO + (i + 1) * nh, :] = hnorm(x_ref[0, i * nh:(i + 1) * nh, :]).astype(BF16)
    h_scr[HALO + tm:, :] = jnp.where(t < nt - 1, hnorm(xn_ref[0]), 0.0).astype(BF16)

    cs_all = cs_ref[...]
    gq_nope = gq_ref[:, 0:LANES]
    gq_rr = gq_ref[:, LANES:2 * LANES]

    for i in range(nsub):
        rows = slice(i * nh, (i + 1) * nh)
        h_main = h_scr[HALO + i * nh:HALO + (i + 1) * nh, :]
        cs = cs_all[rows]

        u_a = jnp.dot(h_main, w_in_ref[:, C_CQ:C_GA], preferred_element_type=F32)
        cq_n = _rms_rows(u_a[:, C_CQ:C_CKV], qlg_ref[...]).astype(BF16)
        qa = jnp.dot(cq_n, w_uq_ref[...], preferred_element_type=F32)
        csq = cs * gq_rr
        for h in range(MLA_HEADS):
            qn = qa[:, h * HEAD_COLS:h * HEAD_COLS + LANES]
            rr = qa[:, h * HEAD_COLS + LANES:(h + 1) * HEAD_COLS]
            ss = jnp.sum(qn * qn + _half_rope_sq(rr), axis=-1, keepdims=True)
            r = lax.rsqrt(ss + HEAD_SS_EPS)
            q_ref[0, h, rows, 0:LANES] = (qn * r * gq_nope).astype(BF16)
            q_rope = _rope_pair(rr, csq) * r
            q_ref[0, h, rows, LANES:QK_HEAD_DIM] = q_rope[:, 0:QK_ROPE_DIM].astype(BF16)

        _kv_heads(u_a[:, C_CKV:C_KR], u_a[:, C_KR:C_GA], kvlg_ref[...], w_ukv_ref[...],
                  gk_ref, cs, k_ref, v_ref, rows)

        u_ga = jnp.dot(h_main, w_in_ref[:, C_GA:C_PIN], preferred_element_type=F32)
        sga_ref[0, rows, :] = _silu(u_ga).astype(BF16)

        pin = jnp.dot(h_scr[i * nh:(i + 1) * nh + 2 * HALO, :], w_in_ref[:, C_PIN:C_GP],
                      preferred_element_type=F32)
        u_gp = jnp.dot(h_main, w_in_ref[:, C_GP:C_END], preferred_element_type=F32)
        tok_head = t * tm + i * nh + lax.broadcasted_iota(jnp.int32, (EDGE, LANES), 0)
        tok_tail = tok_head + (nh - EDGE)
        for g, w in enumerate(POOL_WINDOWS):
            c0 = g * POOL_GROUP_DIM
            c1 = c0 + POOL_GROUP_DIM
            pg = pin[:, c0:c1]
            ws = _window_sum(pg, w)[HALO:HALO + nh]

            def inv_count(tok):
                lo = jnp.maximum(tok - w // 2, 0)
                hi = jnp.minimum(tok - w // 2 + w, seq)
                return 1.0 / (hi - lo).astype(F32)

            inv_cnt = jnp.concatenate(
                [inv_count(tok_head), jnp.full((nh - 2 * EDGE, LANES), 1.0 / w, F32),
                 inv_count(tok_tail)], axis=0)
            pooled = ws * inv_cnt - pg[HALO:HALO + nh]
            y = jnp.dot(pooled.astype(BF16), w_pool_ref[g], preferred_element_type=F32)
            brp_ref[0, rows, c0:c1] = (_silu(u_gp[:, c0:c1]) * (y * ps_ref[:, c0:c1])).astype(BF16)


def _proj(x, mod3, norm_g, w_in_r, qlg, w_uq_r, kvlg, w_ukv_b, gq, gk, cs, w_pool_b, ps, *,
          tm, nsub):
    b, s, d = x.shape
    nt = s // tm
    hb = tm // HALO
    nhb = s // HALO
    const2 = lambda bi, ti: (0, 0)
    kern = functools.partial(_proj_kernel, tm=tm, seq=s, nsub=nsub)
    return pl.pallas_call(
        kern,
        grid=(b, nt),
        in_specs=[
            pl.BlockSpec((1, tm, d), lambda bi, ti: (bi, ti, 0)),
            pl.BlockSpec((1, HALO, d), lambda bi, ti: (bi, jnp.maximum(ti * hb - 1, 0), 0)),
            pl.BlockSpec((1, HALO, d), lambda bi, ti: (bi, jnp.minimum((ti + 1) * hb, nhb - 1), 0)),
            pl.BlockSpec((1, 1, 3 * d), lambda bi, ti: (bi, 0, 0)),
            pl.BlockSpec((1, d), const2),
            pl.BlockSpec((d, C_END), const2),
            pl.BlockSpec((1, Q_LORA_RANK), const2),
            pl.BlockSpec((Q_LORA_RANK, MLA_HEADS * HEAD_COLS), const2),
            pl.BlockSpec((1, KV_LORA_RANK), const2),
            pl.BlockSpec((KV_LORA_RANK, MLA_HEADS * HEAD_COLS), const2),
            pl.BlockSpec((1, 2 * LANES), const2),
            pl.BlockSpec((1, 2 * LANES), const2),
            pl.BlockSpec((tm, LANES), lambda bi, ti: (ti, 0)),
            pl.BlockSpec((len(POOL_WINDOWS), POOL_GROUP_DIM, POOL_GROUP_DIM), lambda bi, ti: (0, 0, 0)),
            pl.BlockSpec((1, D_POOL), const2),
        ],
        out_specs=[
            pl.BlockSpec((1, MLA_HEADS, tm, QK_HEAD_DIM), lambda bi, ti: (bi, 0, ti, 0)),
            pl.BlockSpec((1, MLA_HEADS, tm, QK_HEAD_DIM), lambda bi, ti: (bi, 0, ti, 0)),
            pl.BlockSpec((1, MLA_HEADS, tm, V_AUG_DIM), lambda bi, ti: (bi, 0, ti, 0)),
            pl.BlockSpec((1, tm, D_ATTN), lambda bi, ti: (bi, ti, 0)),
            pl.BlockSpec((1, tm, D_POOL), lambda bi, ti: (bi, ti, 0)),
        ],
        out_shape=[
            jax.ShapeDtypeStruct((b, MLA_HEADS, s, QK_HEAD_DIM), BF16),
            jax.ShapeDtypeStruct((b, MLA_HEADS, s, QK_HEAD_DIM), BF16),
            jax.ShapeDtypeStruct((b, MLA_HEADS, s, V_AUG_DIM), BF16),
            jax.ShapeDtypeStruct((b, s, D_ATTN), BF16),
            jax.ShapeDtypeStruct((b, s, D_POOL), BF16),
        ],
        scratch_shapes=[
            pltpu.VMEM((tm + 2 * HALO, d), BF16),
        ],
        compiler_params=pltpu.CompilerParams(
            dimension_semantics=("parallel", "parallel"),
            vmem_limit_bytes=VMEM_LIMIT),
        name="proj",
    )(x, x, x, mod3, norm_g, w_in_r, qlg, w_uq_r, kvlg, w_ukv_b, gq, gk, cs, w_pool_b, ps)


def _ctxkv_kernel(x_ref, mod_ref, ng_ref, w_kv_ref, kvlg_ref, w_ukv_ref, gk_ref, cs_ref,
                  k_ref, v_ref):
    d = x_ref.shape[-1]
    shift = mod_ref[0, :, 0:d]
    scale = mod_ref[0, :, d:2 * d]
    a = ng_ref[...] * (1.0 + scale)
    xv = x_ref[0]
    ms = jnp.mean(xv * xv, axis=-1, keepdims=True)
    hc = (xv * lax.rsqrt(ms + NORM_EPS) * a + shift).astype(BF16)
    u = jnp.dot(hc, w_kv_ref[...], preferred_element_type=F32)
    _kv_heads(u[:, 0:KV_LORA_RANK], u[:, KV_LORA_RANK:], kvlg_ref[...], w_ukv_ref[...],
              gk_ref, cs_ref[...], k_ref, v_ref, slice(None))


def _ctxkv(ctx, mod3, norm_g, w_kv, kvlg, w_ukv_b, gk, cs_ctx):
    b, lc, d = ctx.shape
    const2 = lambda bi: (0, 0)
    nkv = KV_LORA_RANK + 2 * QK_ROPE_DIM
    return pl.pallas_call(
        _ctxkv_kernel,
        grid=(b,),
        in_specs=[
            pl.BlockSpec((1, lc, d), lambda bi: (bi, 0, 0)),
            pl.BlockSpec((1, 1, 3 * d), lambda bi: (b, 0, 0)),
            pl.BlockSpec((1, d), const2),
            pl.BlockSpec((d, nkv), const2),
            pl.BlockSpec((1, KV_LORA_RANK), const2),
            pl.BlockSpec((KV_LORA_RANK, MLA_HEADS * HEAD_COLS), const2),
            pl.BlockSpec((1, 2 * LANES), const2),
            pl.BlockSpec((lc, LANES), const2),
        ],
        out_specs=[
            pl.BlockSpec((1, MLA_HEADS, lc, QK_HEAD_DIM), lambda bi: (bi, 0, 0, 0)),
            pl.BlockSpec((1, MLA_HEADS, lc, V_AUG_DIM), lambda bi: (bi, 0, 0, 0)),
        ],
        out_shape=[
            jax.ShapeDtypeStruct((b, MLA_HEADS, lc, QK_HEAD_DIM), BF16),
            jax.ShapeDtypeStruct((b, MLA_HEADS, lc, V_AUG_DIM), BF16),
        ],
        compiler_params=pltpu.CompilerParams(
            dimension_semantics=("parallel",), vmem_limit_bytes=VMEM_LIMIT),
        name="ctxkv",
    )(ctx, mod3, norm_g, w_kv, kvlg, w_ukv_b, gk, cs_ctx)


def _lane_tile(a, n):
    return a if n == 1 else jnp.concatenate([a] * n, axis=1)


def _attn_out_kernel(q_ref, k_ref, v_ref, kc_ref, vc_ref, sga_ref, brp_ref, x_ref, mod_ref,
                     w_ref, o_ref, m_scr, acc_scr, attn_scr, *, tk):
    nb, ib, dk = q_ref.shape[2:]
    rows = nb * ib
    nk = k_ref.shape[2] // tk
    dv = V_HEAD_DIM
    d = x_ref.shape[-1]

    def step(h, kb, vb):
        q = q_ref[0, h].reshape(rows, dk)
        s = lax.dot_general(q, kb, (((1,), (1,)), ((), ())), preferred_element_type=F32)
        m_prev = m_scr[h]
        m_next = jnp.maximum(m_prev, jnp.max(s, axis=-1, keepdims=True))
        alpha = jnp.exp2(m_prev - m_next)
        p = jnp.exp2(s - _lane_tile(m_next, s.shape[1] // LANES))
        pv = jnp.dot(p.astype(BF16), vb, preferred_element_type=F32)
        acc_scr[h] = _lane_tile(alpha, V_AUG_DIM // LANES) * acc_scr[h] + pv
        m_scr[h] = m_next

    m_scr[...] = jnp.full(m_scr.shape, -jnp.inf, F32)
    acc_scr[...] = jnp.zeros(acc_scr.shape, F32)
    for h in range(MLA_HEADS):
        step(h, kc_ref[0, h], vc_ref[0, h])
    for j in range(nk):
        for h in range(MLA_HEADS):
            step(h, k_ref[0, h, j * tk:(j + 1) * tk, :], v_ref[0, h, j * tk:(j + 1) * tk, :])

    for h in range(MLA_HEADS):
        attn_scr[h] = acc_scr[h, :, 0:dv] / acc_scr[h, :, dv:2 * dv]
    attn = jnp.concatenate(
        [jnp.concatenate([attn_scr[h, pl.ds(il, nb, stride=ib), :] for il in range(ib)], axis=0)
         for h in range(MLA_HEADS)], axis=1)
    gate = mod_ref[0, :, 2 * d:3 * d]
    br_a = (sga_ref[0].astype(F32) * attn).astype(BF16)
    y = jnp.dot(brp_ref[0], w_ref[D_ATTN:, :], preferred_element_type=F32)
    y = y + jnp.dot(br_a, w_ref[0:D_ATTN, :], preferred_element_type=F32)
    o_ref[0] = x_ref[0] + gate * y


def _attn_out(q, k, v, kc, vc, sga, brp, x, mod3, w_out_b, *, ib, tk):
    b, h, s, dk = q.shape
    _, _, d = x.shape
    lc = kc.shape[2]
    dva = v.shape[-1]
    nb = s // Q_BLOCK
    rows = nb * ib
    q5 = q.reshape(b, h, nb, Q_BLOCK, dk)
    kern = functools.partial(_attn_out_kernel, tk=tk)
    resident = dict(pipeline_mode=pl.Buffered(1))
    row_tile = lambda bi, ti: (bi, ti, 0)
    return pl.pallas_call(
        kern,
        grid=(b, Q_BLOCK // ib),
        in_specs=[
            pl.BlockSpec((1, h, nb, ib, dk), lambda bi, ti: (bi, 0, 0, ti, 0)),
            pl.BlockSpec((1, h, s, dk), lambda bi, ti: (bi, 0, 0, 0)),
            pl.BlockSpec((1, h, s, dva), lambda bi, ti: (bi, 0, 0, 0)),
            pl.BlockSpec((1, h, lc, dk), lambda bi, ti: (bi, 0, 0, 0), **resident),
            pl.BlockSpec((1, h, lc, dva), lambda bi, ti: (bi, 0, 0, 0), **resident),
            pl.BlockSpec((1, rows, D_ATTN), row_tile),
            pl.BlockSpec((1, rows, D_POOL), row_tile),
            pl.BlockSpec((1, rows, d), row_tile),
            pl.BlockSpec((1, 1, 3 * d), lambda bi, ti: (bi, 0, 0)),
            pl.BlockSpec((D_ATTN + D_POOL, d), lambda bi, ti: (0, 0), **resident),
        ],
        out_specs=pl.BlockSpec((1, rows, d), row_tile),
        out_shape=jax.ShapeDtypeStruct((b, s, d), F32),
        scratch_shapes=[
            pltpu.VMEM((h, rows, LANES), F32),
            pltpu.VMEM((h, rows, dva), F32),
            pltpu.VMEM((h, rows, V_HEAD_DIM), F32),
        ],
        compiler_params=pltpu.CompilerParams(
            dimension_semantics=("parallel", "arbitrary"),
            vmem_limit_bytes=VMEM_LIMIT),
        name="attn_out",
    )(q5, k, v, kc, vc, sga, brp, x, mod3, w_out_b)


def _rot_perm():
    q = QK_ROPE_DIM // 4
    perm = np.concatenate([np.arange(q, 2 * q), np.arange(0, q),
                           np.arange(3 * q, 4 * q), np.arange(2 * q, 3 * q)])
    sign = np.concatenate([-np.ones(q), np.ones(q), -np.ones(q), np.ones(q)]).astype(np.float32)
    return perm, sign


def _rope_table(seq):
    rows = seq // GRID_W
    row = jnp.repeat(jnp.arange(rows, dtype=F32), GRID_W)
    col = jnp.tile(jnp.arange(GRID_W, dtype=F32), rows)
    n_freq = QK_ROPE_DIM // 4
    inv = ROPE_BASE ** (-jnp.arange(n_freq, dtype=F32) / n_freq)
    ang_r = row[:, None] * inv
    ang_c = col[:, None] * inv
    ang = jnp.concatenate([ang_r, ang_r, ang_c, ang_c], axis=-1)
    return jnp.concatenate([jnp.cos(ang), jnp.sin(ang)], axis=-1)


def kernel(x, c, ctx, c_ctx, w_mod, b_mod, norm_g, w_in, q_lora_g, w_uq, kv_lora_g, w_ukv,
           q_norm_g, k_norm_g, w_pool, pool_scale, w_out):
    b, s, d = x.shape
    lc = ctx.shape[1]
    depth = w_mod.shape[0]
    assert depth == 1, "single-layer block"
    perm, sign = _rot_perm()
    l = 0

    cc = jnp.concatenate([c, c_ctx[None, :], jnp.zeros((MOD_ROWS - b - 1, d), F32)], axis=0)
    mod = _modulation(cc, w_mod[l], b_mod[l][None, :])
    mod3 = mod[:, None, :]

    wi = w_in[l]
    o_cq, o_ckv, o_kr = 0, Q_LORA_RANK, Q_LORA_RANK + KV_LORA_RANK
    o_ga = o_kr + QK_ROPE_DIM
    o_pin = o_ga + D_ATTN
    o_gp = o_pin + D_POOL
    w_kr = wi[:, o_kr:o_ga]
    w_in_r = jnp.concatenate(
        [wi[:, o_cq:o_kr], w_kr, w_kr[:, perm] * sign, wi[:, o_ga:o_pin],
         wi[:, o_pin:o_gp], wi[:, o_gp:]], axis=1).astype(BF16)
    wq = w_uq[l].reshape(Q_LORA_RANK, MLA_HEADS, QK_HEAD_DIM)
    wq_rope = wq[:, :, QK_NOPE_DIM:]
    w_uq_r = jnp.concatenate([wq, wq_rope[:, :, perm] * sign], axis=-1)
    w_uq_r = w_uq_r.reshape(Q_LORA_RANK, MLA_HEADS * HEAD_COLS).astype(BF16)
    w_ukv_b = w_ukv[l].astype(BF16)
    w_pool_b = w_pool[l].astype(BF16)
    w_out_b = w_out[l].astype(BF16)

    def gain_row(g, const):
        return (jnp.concatenate([g, g[QK_NOPE_DIM:][perm]]) * const)[None, :]

    gq = gain_row(q_norm_g[l], Q_SCALE * math.sqrt(QK_HEAD_DIM))
    gk = gain_row(k_norm_g[l], math.sqrt(QK_HEAD_DIM))
    cs = _rope_table(s)
    cs_ctx = jnp.concatenate([jnp.ones((lc, QK_ROPE_DIM), F32),
                              jnp.zeros((lc, QK_ROPE_DIM), F32)], axis=1)

    ng = norm_g[l][None, :]
    qlg = q_lora_g[l][None, :]
    kvlg = kv_lora_g[l][None, :]
    ps = pool_scale[l][None, :]

    q, k, v, sga, brp = _proj(x, mod3, ng, w_in_r, qlg, w_uq_r, kvlg, w_ukv_b, gq, gk, cs,
                              w_pool_b, ps, tm=1024, nsub=4)
    kc, vc = _ctxkv(ctx, mod3, ng, w_in_r[:, C_CKV:C_GA], kvlg, w_ukv_b, gk, cs_ctx)
    return _attn_out(q, k, v, kc, vc, sga, brp, x, mod3, w_out_b, ib=16, tk=512)
```

```python
import functools
import math

import jax
import jax.numpy as jnp
import numpy as np
from jax import lax
from jax.experimental import pallas as pl
from jax.experimental.pallas import tpu as pltpu

F32 = jnp.float32
BF16 = jnp.bfloat16

GRID_W = 64
MLA_HEADS = 4
QK_NOPE_DIM = 128
QK_ROPE_DIM = 64
QK_HEAD_DIM = QK_NOPE_DIM + QK_ROPE_DIM
V_HEAD_DIM = 128
V_AUG_DIM = 2 * V_HEAD_DIM
QK_AUG_DIM = 256
MAX_SHIFT_RANGE = 110.0
Q_LORA_RANK = 256
KV_LORA_RANK = 128
ROPE_BASE = 10000.0
POOL_WINDOWS = (2, 4, 8, 16)
POOL_GROUP_DIM = 128
NORM_EPS = 1e-6
Q_BLOCK = 128
D_ATTN = MLA_HEADS * V_HEAD_DIM
D_POOL = len(POOL_WINDOWS) * POOL_GROUP_DIM

LANES = 128
HEAD_COLS = 2 * LANES
HALO = 16
EDGE = 8
assert max(POOL_WINDOWS) // 2 <= EDGE <= HALO
MOD_ROWS = 16

C_CQ = 0
C_CKV = C_CQ + Q_LORA_RANK
C_KR = C_CKV + KV_LORA_RANK
C_GA = C_KR + 2 * QK_ROPE_DIM
C_PIN = C_GA + D_ATTN
C_GP = C_PIN + D_POOL
C_END = C_GP + D_POOL

VMEM_LIMIT = 56 * 1024 * 1024

Q_SCALE = QK_HEAD_DIM ** -0.5 * math.log2(math.e)


def _silu(x):
    hx = 0.5 * x
    return hx + hx * jnp.tanh(hx)


def _window_sum(p, w):
    n = p.shape[0]
    up = lambda a, k: pltpu.roll(a, n - k, axis=0)
    down = lambda a, k: pltpu.roll(a, k, axis=0)
    if w == 2:
        return p + down(p, 1)
    acc = p + up(p, 1)
    span = 2
    while span < w // 2:
        acc = acc + up(acc, span)
        span *= 2
    return acc + down(acc, w // 2)


def _rms_rows(x, g):
    ms = jnp.mean(x * x, axis=-1, keepdims=True)
    return x * lax.rsqrt(ms + NORM_EPS) * g


def _mod_kernel(cc_ref, w_ref, b_ref, o_ref):
    a = _silu(cc_ref[...])
    o_ref[...] = jnp.dot(a, w_ref[...], preferred_element_type=F32,
                         precision=lax.Precision.HIGHEST) + b_ref[...]


def _modulation(cc, w_mod, b_mod):
    d = w_mod.shape[0]
    n = w_mod.shape[1]
    tn = 768
    return pl.pallas_call(
        _mod_kernel,
        grid=(n // tn,),
        in_specs=[
            pl.BlockSpec((MOD_ROWS, d), lambda j: (0, 0)),
            pl.BlockSpec((d, tn), lambda j: (0, j)),
            pl.BlockSpec((1, tn), lambda j: (0, j)),
        ],
        out_specs=pl.BlockSpec((MOD_ROWS, tn), lambda j: (0, j)),
        out_shape=jax.ShapeDtypeStruct((MOD_ROWS, n), F32),
        name="mod",
    )(cc, w_mod, b_mod)


HEAD_SS_EPS = QK_HEAD_DIM * NORM_EPS


def _rope_pair(rr, csg):
    pr = rr * csg
    return pr + pltpu.roll(pr, QK_ROPE_DIM, axis=1)


def _half_rope_sq(rr):
    return 0.5 * (rr * rr)


def _rope_lanes():
    return lax.broadcasted_iota(jnp.int32, (1, LANES), 1) < QK_ROPE_DIM


def _kv_heads(ckv, kr2, kvlg, w_ukv, gk_ref, cs, k_ref, v_ref, rows):
    ckv_n = _rms_rows(ckv, kvlg).astype(BF16)
    kv = jnp.dot(ckv_n, w_ukv, preferred_element_type=F32)
    g_nope = gk_ref[:, 0:LANES]
    tail = gk_ref[:, 2 * LANES:3 * LANES]
    rope_lanes = _rope_lanes()
    k_rope = _rope_pair(kr2, cs * gk_ref[:, LANES:2 * LANES])
    sq_rope = _half_rope_sq(kr2)
    for h in range(MLA_HEADS):
        kn = kv[:, h * HEAD_COLS:h * HEAD_COLS + LANES]
        vv = kv[:, h * HEAD_COLS + LANES:(h + 1) * HEAD_COLS]
        ss = jnp.sum(kn * kn + sq_rope, axis=-1, keepdims=True)
        r = lax.rsqrt(ss + HEAD_SS_EPS)
        k_ref[0, h, rows, 0:LANES] = (kn * r * g_nope).astype(BF16)
        k_ref[0, h, rows, LANES:] = jnp.where(rope_lanes, k_rope * r, tail).astype(BF16)
        v_ref[0, h, rows, 0:V_HEAD_DIM] = vv.astype(BF16)
        v_ref[0, h, rows, V_HEAD_DIM:] = jnp.ones((vv.shape[0], V_AUG_DIM - V_HEAD_DIM), BF16)


def _proj_kernel(x_ref, xp_ref, xn_ref, mod_ref, ng_ref, w_in_ref, qlg_ref, w_uq_ref,
                 kvlg_ref, w_ukv_ref, gq_ref, gk_ref, cs_ref, w_pool_ref, ps_ref,
                 q_ref, k_ref, v_ref, sga_ref, brp_ref,
                 h_scr, *, tm, seq, nsub):
    t = pl.program_id(1)
    nt = pl.num_programs(1)
    d = x_ref.shape[-1]
    shift = mod_ref[0, :, 0:d]
    scale = mod_ref[0, :, d:2 * d]
    a = ng_ref[...] * (1.0 + scale)

    def hnorm(xv):
        ms = jnp.mean(xv * xv, axis=-1, keepdims=True)
        return xv * lax.rsqrt(ms + NORM_EPS) * a + shift

    nh = tm // nsub
    h_scr[0:HALO, :] = jnp.where(t > 0, hnorm(xp_ref[0]), 0.0).astype(BF16)
    for i in range(nsub):
        h_scr[HALO + i * nh:HALO + (i + 1) * nh, :] = hnorm(x_ref[0, i * nh:(i + 1) * nh, :]).astype(BF16)
    h_scr[HALO + tm:, :] = jnp.where(t < nt - 1, hnorm(xn_ref[0]), 0.0).astype(BF16)

    cs_all = cs_ref[...]
    gq_nope = gq_ref[:, 0:LANES]
    gq_rr = gq_ref[:, LANES:2 * LANES]
    gq_tail = gq_ref[:, 2 * LANES:3 * LANES]
    rope_lanes = _rope_lanes()
    assert (tm // nsub) % Q_BLOCK == 0

    for i in range(nsub):
        rows = slice(i * nh, (i + 1) * nh)
        h_main = h_scr[HALO + i * nh:HALO + (i + 1) * nh, :]
        cs = cs_all[rows]

        u_a = jnp.dot(h_main, w_in_ref[:, C_CQ:C_GA], preferred_element_type=F32)
        cq_n = _rms_rows(u_a[:, C_CQ:C_CKV], qlg_ref[...]).astype(BF16)
        qa = jnp.dot(cq_n, w_uq_ref[...], preferred_element_type=F32)
        csq = cs * gq_rr
        for h in range(MLA_HEADS):
            qn = qa[:, h * HEAD_COLS:h * HEAD_COLS + LANES]
            rr = qa[:, h * HEAD_COLS + LANES:(h + 1) * HEAD_COLS]
            ss = jnp.sum(qn * qn + _half_rope_sq(rr), axis=-1, keepdims=True)
            r = lax.rsqrt(ss + HEAD_SS_EPS)
            q_lo = (qn * r * gq_nope).astype(BF16)
            q_hi = jnp.where(rope_lanes, _rope_pair(rr, csq) * r, gq_tail).astype(BF16)
            for j in range(nh // Q_BLOCK):
                blk = i * (nh // Q_BLOCK) + j
                q_ref[0, h, blk, :, 0:LANES] = q_lo[j * Q_BLOCK:(j + 1) * Q_BLOCK]
                q_ref[0, h, blk, :, LANES:] = q_hi[j * Q_BLOCK:(j + 1) * Q_BLOCK]

        _kv_heads(u_a[:, C_CKV:C_KR], u_a[:, C_KR:C_GA], kvlg_ref[...], w_ukv_ref[...],
                  gk_ref, cs, k_ref, v_ref, rows)

        u_ga = jnp.dot(h_main, w_in_ref[:, C_GA:C_PIN], preferred_element_type=F32)
        sga_ref[0, rows, :] = _silu(u_ga).astype(BF16)

        pin = jnp.dot(h_scr[i * nh:(i + 1) * nh + 2 * HALO, :], w_in_ref[:, C_PIN:C_GP],
                      preferred_element_type=F32)
        u_gp = jnp.dot(h_main, w_in_ref[:, C_GP:C_END], preferred_element_type=F32)
        tok_head = t * tm + i * nh + lax.broadcasted_iota(jnp.int32, (EDGE, LANES), 0)
        tok_tail = tok_head + (nh - EDGE)
        for g, w in enumerate(POOL_WINDOWS):
            c0 = g * POOL_GROUP_DIM
            c1 = c0 + POOL_GROUP_DIM
            pg = pin[:, c0:c1]
            ws = _window_sum(pg, w)[HALO:HALO + nh]

            def inv_count(tok):
                lo = jnp.maximum(tok - w // 2, 0)
                hi = jnp.minimum(tok - w // 2 + w, seq)
                return 1.0 / (hi - lo).astype(F32)

            inv_cnt = jnp.concatenate(
                [inv_count(tok_head), jnp.full((nh - 2 * EDGE, LANES), 1.0 / w, F32),
                 inv_count(tok_tail)], axis=0)
            pooled = ws * inv_cnt - pg[HALO:HALO + nh]
            y = jnp.dot(pooled.astype(BF16), w_pool_ref[g], preferred_element_type=F32)
            brp_ref[0, rows, c0:c1] = (_silu(u_gp[:, c0:c1]) * (y * ps_ref[:, c0:c1])).astype(BF16)


def _proj(x, mod3, norm_g, w_in_r, qlg, w_uq_r, kvlg, w_ukv_b, gq, gk, cs, w_pool_b, ps, *,
          tm, nsub):
    b, s, d = x.shape
    nt = s // tm
    hb = tm // HALO
    nhb = s // HALO
    const2 = lambda bi, ti: (0, 0)
    kern = functools.partial(_proj_kernel, tm=tm, seq=s, nsub=nsub)
    return pl.pallas_call(
        kern,
        grid=(b, nt),
        in_specs=[
            pl.BlockSpec((1, tm, d), lambda bi, ti: (bi, ti, 0)),
            pl.BlockSpec((1, HALO, d), lambda bi, ti: (bi, jnp.maximum(ti * hb - 1, 0), 0)),
            pl.BlockSpec((1, HALO, d), lambda bi, ti: (bi, jnp.minimum((ti + 1) * hb, nhb - 1), 0)),
            pl.BlockSpec((1, 1, 3 * d), lambda bi, ti: (bi, 0, 0)),
            pl.BlockSpec((1, d), const2),
            pl.BlockSpec((d, C_END), const2),
            pl.BlockSpec((1, Q_LORA_RANK), const2),
            pl.BlockSpec((Q_LORA_RANK, MLA_HEADS * HEAD_COLS), const2),
            pl.BlockSpec((1, KV_LORA_RANK), const2),
            pl.BlockSpec((KV_LORA_RANK, MLA_HEADS * HEAD_COLS), const2),
            pl.BlockSpec((1, 3 * LANES), const2),
            pl.BlockSpec((1, 3 * LANES), const2),
            pl.BlockSpec((tm, LANES), lambda bi, ti: (ti, 0)),
            pl.BlockSpec((len(POOL_WINDOWS), POOL_GROUP_DIM, POOL_GROUP_DIM), lambda bi, ti: (0, 0, 0)),
            pl.BlockSpec((1, D_POOL), const2),
        ],
        out_specs=[
            pl.BlockSpec((1, MLA_HEADS, tm // Q_BLOCK, Q_BLOCK, QK_AUG_DIM),
                         lambda bi, ti: (bi, 0, ti, 0, 0)),
            pl.BlockSpec((1, MLA_HEADS, tm, QK_AUG_DIM), lambda bi, ti: (bi, 0, ti, 0)),
            pl.BlockSpec((1, MLA_HEADS, tm, V_AUG_DIM), lambda bi, ti: (bi, 0, ti, 0)),
            pl.BlockSpec((1, tm, D_ATTN), lambda bi, ti: (bi, ti, 0)),
            pl.BlockSpec((1, tm, D_POOL), lambda bi, ti: (bi, ti, 0)),
        ],
        out_shape=[
            jax.ShapeDtypeStruct((b, MLA_HEADS, s // Q_BLOCK, Q_BLOCK, QK_AUG_DIM), BF16),
            jax.ShapeDtypeStruct((b, MLA_HEADS, s, QK_AUG_DIM), BF16),
            jax.ShapeDtypeStruct((b, MLA_HEADS, s, V_AUG_DIM), BF16),
            jax.ShapeDtypeStruct((b, s, D_ATTN), BF16),
            jax.ShapeDtypeStruct((b, s, D_POOL), BF16),
        ],
        scratch_shapes=[
            pltpu.VMEM((tm + 2 * HALO, d), BF16),
        ],
        compiler_params=pltpu.CompilerParams(
            dimension_semantics=("parallel", "parallel"),
            vmem_limit_bytes=VMEM_LIMIT),
        name="proj",
    )(x, x, x, mod3, norm_g, w_in_r, qlg, w_uq_r, kvlg, w_ukv_b, gq, gk, cs, w_pool_b, ps)


def _ctxkv_kernel(x_ref, mod_ref, ng_ref, w_kv_ref, kvlg_ref, w_ukv_ref, gk_ref, cs_ref,
                  k_ref, v_ref):
    d = x_ref.shape[-1]
    shift = mod_ref[0, :, 0:d]
    scale = mod_ref[0, :, d:2 * d]
    a = ng_ref[...] * (1.0 + scale)
    xv = x_ref[0]
    ms = jnp.mean(xv * xv, axis=-1, keepdims=True)
    hc = (xv * lax.rsqrt(ms + NORM_EPS) * a + shift).astype(BF16)
    u = jnp.dot(hc, w_kv_ref[...], preferred_element_type=F32)
    _kv_heads(u[:, 0:KV_LORA_RANK], u[:, KV_LORA_RANK:], kvlg_ref[...], w_ukv_ref[...],
              gk_ref, cs_ref[...], k_ref, v_ref, slice(None))


def _ctxkv(ctx, mod3, norm_g, w_kv, kvlg, w_ukv_b, gk, cs_ctx):
    b, lc, d = ctx.shape
    const2 = lambda bi: (0, 0)
    nkv = KV_LORA_RANK + 2 * QK_ROPE_DIM
    return pl.pallas_call(
        _ctxkv_kernel,
        grid=(b,),
        in_specs=[
            pl.BlockSpec((1, lc, d), lambda bi: (bi, 0, 0)),
            pl.BlockSpec((1, 1, 3 * d), lambda bi: (b, 0, 0)),
            pl.BlockSpec((1, d), const2),
            pl.BlockSpec((d, nkv), const2),
            pl.BlockSpec((1, KV_LORA_RANK), const2),
            pl.BlockSpec((KV_LORA_RANK, MLA_HEADS * HEAD_COLS), const2),
            pl.BlockSpec((1, 3 * LANES), const2),
            pl.BlockSpec((lc, LANES), const2),
        ],
        out_specs=[
            pl.BlockSpec((1, MLA_HEADS, lc, QK_AUG_DIM), lambda bi: (bi, 0, 0, 0)),
            pl.BlockSpec((1, MLA_HEADS, lc, V_AUG_DIM), lambda bi: (bi, 0, 0, 0)),
        ],
        out_shape=[
            jax.ShapeDtypeStruct((b, MLA_HEADS, lc, QK_AUG_DIM), BF16),
            jax.ShapeDtypeStruct((b, MLA_HEADS, lc, V_AUG_DIM), BF16),
        ],
        compiler_params=pltpu.CompilerParams(
            dimension_semantics=("parallel",), vmem_limit_bytes=VMEM_LIMIT),
        name="ctxkv",
    )(ctx, mod3, norm_g, w_kv, kvlg, w_ukv_b, gk, cs_ctx)


def _lane_tile(a, n):
    return a if n == 1 else jnp.concatenate([a] * n, axis=1)


def _attn_out_kernel(q_ref, k_ref, v_ref, kc_ref, vc_ref, sga_ref, brp_ref, x_ref, mod_ref,
                     w_ref, o_ref, attn_scr, *stat_scr, tk, running_max):
    nb, ib, dk = q_ref.shape[2:]
    rows = nb * ib
    nk = k_ref.shape[2] // tk
    dv = V_HEAD_DIM
    d = x_ref.shape[-1]

    def keys(h, c):
        return kc_ref[0, h] if c == 0 else k_ref[0, h, (c - 1) * tk:c * tk, :]

    def values(h, c):
        return vc_ref[0, h] if c == 0 else v_ref[0, h, (c - 1) * tk:c * tk, :]

    def scores(h, c):
        q = q_ref[0, h].reshape(rows, dk)
        return lax.dot_general(q, keys(h, c), (((1,), (1,)), ((), ())),
                               preferred_element_type=F32)

    gate = mod_ref[0, :, 2 * d:3 * d]
    o_ref[0] = x_ref[0] + gate * jnp.dot(brp_ref[0], w_ref[D_ATTN:, :],
                                         preferred_element_type=F32)

    if running_max:
        m_scr, acc_scr = stat_scr

        def accumulate(h, s, vb):
            m_prev = m_scr[h]
            m_next = jnp.maximum(m_prev, jnp.max(s, axis=-1, keepdims=True))
            alpha = jnp.exp2(m_prev - m_next)
            p = jnp.exp2(s - _lane_tile(m_next, s.shape[1] // LANES))
            pv = jnp.dot(p.astype(BF16), vb, preferred_element_type=F32)
            acc_scr[h] = _lane_tile(alpha, V_AUG_DIM // LANES) * acc_scr[h] + pv
            m_scr[h] = m_next

        m_scr[...] = jnp.full(m_scr.shape, -jnp.inf, F32)
        acc_scr[...] = jnp.zeros(acc_scr.shape, F32)
        for c in range(nk + 1):
            for h in range(MLA_HEADS):
                accumulate(h, scores(h, c), values(h, c))
        for h in range(MLA_HEADS):
            attn_scr[h] = acc_scr[h, :, 0:dv] / acc_scr[h, :, dv:2 * dv]
    else:
        acc = [None] * MLA_HEADS
        for c in range(nk + 1):
            for h in range(MLA_HEADS):
                p = jnp.exp2(scores(h, c)).astype(BF16)
                pv = jnp.dot(p, values(h, c), preferred_element_type=F32)
                acc[h] = pv if acc[h] is None else acc[h] + pv
        for h in range(MLA_HEADS):
            attn_scr[h] = acc[h][:, 0:dv] / acc[h][:, dv:2 * dv]
    attn = jnp.concatenate(
        [jnp.concatenate([attn_scr[h, pl.ds(il, nb, stride=ib), :] for il in range(ib)], axis=0)
         for h in range(MLA_HEADS)], axis=1)
    br_a = (sga_ref[0].astype(F32) * attn).astype(BF16)
    o_ref[0] = o_ref[0] + gate * jnp.dot(br_a, w_ref[0:D_ATTN, :], preferred_element_type=F32)


def _attn_out(q5, k, v, kc, vc, sga, brp, x, mod3, w_out_b, *, ib, tk, running_max):
    b, h, nb, _, dk = q5.shape
    _, s, d = x.shape
    lc = kc.shape[2]
    dva = v.shape[-1]
    rows = nb * ib
    kern = functools.partial(_attn_out_kernel, tk=tk, running_max=running_max)
    stat_scr = [pltpu.VMEM((h, rows, LANES), F32), pltpu.VMEM((h, rows, dva), F32)]
    resident = dict(pipeline_mode=pl.Buffered(1))
    row_tile = lambda bi, ti: (bi, ti, 0)
    return pl.pallas_call(
        kern,
        grid=(b, Q_BLOCK // ib),
        in_specs=[
            pl.BlockSpec((1, h, nb, ib, dk), lambda bi, ti: (bi, 0, 0, ti, 0)),
            pl.BlockSpec((1, h, s, dk), lambda bi, ti: (bi, 0, 0, 0)),
            pl.BlockSpec((1, h, s, dva), lambda bi, ti: (bi, 0, 0, 0)),
            pl.BlockSpec((1, h, lc, dk), lambda bi, ti: (bi, 0, 0, 0), **resident),
            pl.BlockSpec((1, h, lc, dva), lambda bi, ti: (bi, 0, 0, 0), **resident),
            pl.BlockSpec((1, rows, D_ATTN), row_tile),
            pl.BlockSpec((1, rows, D_POOL), row_tile),
            pl.BlockSpec((1, rows, d), row_tile),
            pl.BlockSpec((1, 1, 3 * d), lambda bi, ti: (bi, 0, 0)),
            pl.BlockSpec((D_ATTN + D_POOL, d), lambda bi, ti: (0, 0), **resident),
        ],
        out_specs=pl.BlockSpec((1, rows, d), row_tile),
        out_shape=jax.ShapeDtypeStruct((b, s, d), F32),
        scratch_shapes=[pltpu.VMEM((h, rows, V_HEAD_DIM), F32)] + (stat_scr if running_max else []),
        compiler_params=pltpu.CompilerParams(
            dimension_semantics=("parallel", "arbitrary"),
            vmem_limit_bytes=VMEM_LIMIT),
        name="attn_out_rmax" if running_max else "attn_out",
    )(q5, k, v, kc, vc, sga, brp, x, mod3, w_out_b)


def _rot_perm():
    q = QK_ROPE_DIM // 4
    perm = np.concatenate([np.arange(q, 2 * q), np.arange(0, q),
                           np.arange(3 * q, 4 * q), np.arange(2 * q, 3 * q)])
    sign = np.concatenate([-np.ones(q), np.ones(q), -np.ones(q), np.ones(q)]).astype(np.float32)
    return perm, sign


def _rope_table(seq):
    rows = seq // GRID_W
    row = jnp.repeat(jnp.arange(rows, dtype=F32), GRID_W)
    col = jnp.tile(jnp.arange(GRID_W, dtype=F32), rows)
    n_freq = QK_ROPE_DIM // 4
    inv = ROPE_BASE ** (-jnp.arange(n_freq, dtype=F32) / n_freq)
    ang_r = row[:, None] * inv
    ang_c = col[:, None] * inv
    ang = jnp.concatenate([ang_r, ang_r, ang_c, ang_c], axis=-1)
    return jnp.concatenate([jnp.cos(ang), jnp.sin(ang)], axis=-1)


def kernel(x, c, ctx, c_ctx, w_mod, b_mod, norm_g, w_in, q_lora_g, w_uq, kv_lora_g, w_ukv,
           q_norm_g, k_norm_g, w_pool, pool_scale, w_out):
    b, s, d = x.shape
    lc = ctx.shape[1]
    depth = w_mod.shape[0]
    assert depth == 1, "single-layer block"
    perm, sign = _rot_perm()
    l = 0

    cc = jnp.concatenate([c, c_ctx[None, :], jnp.zeros((MOD_ROWS - b - 1, d), F32)], axis=0)
    mod = _modulation(cc, w_mod[l], b_mod[l][None, :])
    mod3 = mod[:, None, :]

    wi = w_in[l]
    o_cq, o_ckv, o_kr = 0, Q_LORA_RANK, Q_LORA_RANK + KV_LORA_RANK
    o_ga = o_kr + QK_ROPE_DIM
    o_pin = o_ga + D_ATTN
    o_gp = o_pin + D_POOL
    w_kr = wi[:, o_kr:o_ga]
    w_in_r = jnp.concatenate(
        [wi[:, o_cq:o_kr], w_kr, w_kr[:, perm] * sign, wi[:, o_ga:o_pin],
         wi[:, o_pin:o_gp], wi[:, o_gp:]], axis=1).astype(BF16)
    wq = w_uq[l].reshape(Q_LORA_RANK, MLA_HEADS, QK_HEAD_DIM)
    wq_rope = wq[:, :, QK_NOPE_DIM:]
    w_uq_r = jnp.concatenate([wq, wq_rope[:, :, perm] * sign], axis=-1)
    w_uq_r = w_uq_r.reshape(Q_LORA_RANK, MLA_HEADS * HEAD_COLS).astype(BF16)
    w_ukv_b = w_ukv[l].astype(BF16)
    w_pool_b = w_pool[l].astype(BF16)
    w_out_b = w_out[l].astype(BF16)

    bound = (Q_SCALE * QK_HEAD_DIM) * jnp.max(jnp.abs(q_norm_g[l])) * jnp.max(jnp.abs(k_norm_g[l]))

    def gain_row(g, const, tail0):
        gains = jnp.concatenate([g, g[QK_NOPE_DIM:][perm]]) * const
        tail = jnp.zeros((LANES,), F32).at[QK_ROPE_DIM].set(tail0)
        return jnp.concatenate([gains, tail])[None, :]

    gq = gain_row(q_norm_g[l], Q_SCALE * math.sqrt(QK_HEAD_DIM), -bound)
    gk = gain_row(k_norm_g[l], math.sqrt(QK_HEAD_DIM), 1.0)
    cs = _rope_table(s)
    cs_ctx = jnp.concatenate([jnp.ones((lc, QK_ROPE_DIM), F32),
                              jnp.zeros((lc, QK_ROPE_DIM), F32)], axis=1)

    ng = norm_g[l][None, :]
    qlg = q_lora_g[l][None, :]
    kvlg = kv_lora_g[l][None, :]
    ps = pool_scale[l][None, :]

    q, k, v, sga, brp = _proj(x, mod3, ng, w_in_r, qlg, w_uq_r, kvlg, w_ukv_b, gq, gk, cs,
                              w_pool_b, ps, tm=1024, nsub=4)
    kc, vc = _ctxkv(ctx, mod3, ng, w_in_r[:, C_CKV:C_GA], kvlg, w_ukv_b, gk, cs_ctx)
    args = (q, k, v, kc, vc, sga, brp, x, mod3, w_out_b)
    attn_out = functools.partial(_attn_out, ib=16, tk=512)
    return lax.cond(2.0 * bound < MAX_SHIFT_RANGE,
                    lambda a: attn_out(*a, running_max=False),
                    lambda a: attn_out(*a, running_max=True), args)
```

```python
import functools
import math

import jax
import jax.numpy as jnp
import numpy as np
from jax import lax
from jax.experimental import pallas as pl
from jax.experimental.pallas import tpu as pltpu

F32 = jnp.float32
BF16 = jnp.bfloat16

GRID_W = 64
MLA_HEADS = 4
QK_NOPE_DIM = 128
QK_ROPE_DIM = 64
QK_HEAD_DIM = QK_NOPE_DIM + QK_ROPE_DIM
V_HEAD_DIM = 128
V_AUG_DIM = 2 * V_HEAD_DIM
QK_AUG_DIM = 256
MAX_SHIFT_RANGE = 110.0
Q_LORA_RANK = 256
KV_LORA_RANK = 128
ROPE_BASE = 10000.0
POOL_WINDOWS = (2, 4, 8, 16)
POOL_GROUP_DIM = 128
NORM_EPS = 1e-6
Q_BLOCK = 128
D_ATTN = MLA_HEADS * V_HEAD_DIM
D_POOL = len(POOL_WINDOWS) * POOL_GROUP_DIM

LANES = 128
HEAD_COLS = 2 * LANES
HALO = 16
EDGE = 8
assert max(POOL_WINDOWS) // 2 <= EDGE <= HALO
MOD_ROWS = 16

C_CQ = 0
C_CKV = C_CQ + Q_LORA_RANK
C_KR = C_CKV + KV_LORA_RANK
C_GA = C_KR + 2 * QK_ROPE_DIM
C_PIN = C_GA + D_ATTN
C_GP = C_PIN + D_POOL
C_END = C_GP + D_POOL

VMEM_LIMIT = 56 * 1024 * 1024

Q_SCALE = QK_HEAD_DIM ** -0.5 * math.log2(math.e)


def _silu(x):
    hx = 0.5 * x
    return hx + hx * jnp.tanh(hx)


def _window_sum(p, w):
    n = p.shape[0]
    up = lambda a, k: pltpu.roll(a, n - k, axis=0)
    down = lambda a, k: pltpu.roll(a, k, axis=0)
    if w == 2:
        return p + down(p, 1)
    acc = p + up(p, 1)
    span = 2
    while span < w // 2:
        acc = acc + up(acc, span)
        span *= 2
    return acc + down(acc, w // 2)


def _rms_rows(x, g):
    ms = jnp.mean(x * x, axis=-1, keepdims=True)
    return x * lax.rsqrt(ms + NORM_EPS) * g


def _mod_kernel(cc_ref, w_ref, b_ref, o_ref):
    a = _silu(cc_ref[...]).astype(BF16)
    o_ref[...] = jnp.dot(a, w_ref[0].astype(BF16), preferred_element_type=F32) + b_ref[...]


def _modulation(cc, w_mod, b_mod, layer):
    _, d, n = w_mod.shape
    tn = 768
    return pl.pallas_call(
        _mod_kernel,
        grid=(n // tn,),
        in_specs=[
            pl.BlockSpec((MOD_ROWS, d), lambda j: (0, 0)),
            pl.BlockSpec((1, d, tn), lambda j: (layer, 0, j)),
            pl.BlockSpec((1, tn), lambda j: (0, j)),
        ],
        out_specs=pl.BlockSpec((MOD_ROWS, tn), lambda j: (0, j)),
        out_shape=jax.ShapeDtypeStruct((MOD_ROWS, n), F32),
        name="mod",
    )(cc, w_mod, b_mod)


HEAD_SS_EPS = QK_HEAD_DIM * NORM_EPS


def _rope_pair(rr, csg):
    pr = rr * csg
    return pr + pltpu.roll(pr, QK_ROPE_DIM, axis=1)


def _half_rope_sq(rr):
    return 0.5 * (rr * rr)


def _rope_lanes():
    return lax.broadcasted_iota(jnp.int32, (1, LANES), 1) < QK_ROPE_DIM


def _kv_heads(ckv, kr2, kvlg, w_ukv, gk_ref, cs, k_ref, v_ref, rows):
    ckv_n = _rms_rows(ckv, kvlg).astype(BF16)
    kv = jnp.dot(ckv_n, w_ukv, preferred_element_type=F32)
    g_nope = gk_ref[:, 0:LANES]
    tail = gk_ref[:, 2 * LANES:3 * LANES]
    rope_lanes = _rope_lanes()
    k_rope = _rope_pair(kr2, cs * gk_ref[:, LANES:2 * LANES])
    sq_rope = _half_rope_sq(kr2)
    for h in range(MLA_HEADS):
        kn = kv[:, h * HEAD_COLS:h * HEAD_COLS + LANES]
        vv = kv[:, h * HEAD_COLS + LANES:(h + 1) * HEAD_COLS]
        ss = jnp.sum(kn * kn + sq_rope, axis=-1, keepdims=True)
        r = lax.rsqrt(ss + HEAD_SS_EPS)
        k_ref[0, h, rows, 0:LANES] = (kn * r * g_nope).astype(BF16)
        k_ref[0, h, rows, LANES:] = jnp.where(rope_lanes, k_rope * r, tail).astype(BF16)
        v_ref[0, h, rows, 0:V_HEAD_DIM] = vv.astype(BF16)
        v_ref[0, h, rows, V_HEAD_DIM:] = jnp.ones((vv.shape[0], V_AUG_DIM - V_HEAD_DIM), BF16)


def _proj_kernel(x_ref, xp_ref, xn_ref, mod_ref, ng_ref, w_in_ref, qlg_ref, w_uq_ref,
                 kvlg_ref, w_ukv_ref, gq_ref, gk_ref, cs_ref, w_pool_ref, ps_ref,
                 q_ref, k_ref, v_ref, sga_ref, brp_ref,
                 h_scr, *, tm, seq, nsub):
    t = pl.program_id(1)
    nt = pl.num_programs(1)
    d = x_ref.shape[-1]
    shift = mod_ref[0, :, 0:d]
    scale = mod_ref[0, :, d:2 * d]
    a = ng_ref[...] * (1.0 + scale)

    def hnorm(xv):
        ms = jnp.mean(xv * xv, axis=-1, keepdims=True)
        return xv * lax.rsqrt(ms + NORM_EPS) * a + shift

    nh = tm // nsub
    h_scr[0:HALO, :] = jnp.where(t > 0, hnorm(xp_ref[0]), 0.0).astype(BF16)
    for i in range(nsub):
        h_scr[HALO + i * nh:HALO + (i + 1) * nh, :] = hnorm(x_ref[0, i * nh:(i + 1) * nh, :]).astype(BF16)
    h_scr[HALO + tm:, :] = jnp.where(t < nt - 1, hnorm(xn_ref[0]), 0.0).astype(BF16)

    cs_all = cs_ref[...]
    gq_nope = gq_ref[:, 0:LANES]
    gq_rr = gq_ref[:, LANES:2 * LANES]
    gq_tail = gq_ref[:, 2 * LANES:3 * LANES]
    rope_lanes = _rope_lanes()
    assert (tm // nsub) % Q_BLOCK == 0

    for i in range(nsub):
        rows = slice(i * nh, (i + 1) * nh)
        h_main = h_scr[HALO + i * nh:HALO + (i + 1) * nh, :]
        cs = cs_all[rows]

        u_a = jnp.dot(h_main, w_in_ref[:, C_CQ:C_GA], preferred_element_type=F32)
        cq_n = _rms_rows(u_a[:, C_CQ:C_CKV], qlg_ref[...]).astype(BF16)
        qa = jnp.dot(cq_n, w_uq_ref[...], preferred_element_type=F32)
        csq = cs * gq_rr
        for h in range(MLA_HEADS):
            qn = qa[:, h * HEAD_COLS:h * HEAD_COLS + LANES]
            rr = qa[:, h * HEAD_COLS + LANES:(h + 1) * HEAD_COLS]
            ss = jnp.sum(qn * qn + _half_rope_sq(rr), axis=-1, keepdims=True)
            r = lax.rsqrt(ss + HEAD_SS_EPS)
            q_lo = (qn * r * gq_nope).astype(BF16)
            q_hi = jnp.where(rope_lanes, _rope_pair(rr, csq) * r, gq_tail).astype(BF16)
            for j in range(nh // Q_BLOCK):
                blk = i * (nh // Q_BLOCK) + j
                q_ref[0, h, blk, :, 0:LANES] = q_lo[j * Q_BLOCK:(j + 1) * Q_BLOCK]
                q_ref[0, h, blk, :, LANES:] = q_hi[j * Q_BLOCK:(j + 1) * Q_BLOCK]

        _kv_heads(u_a[:, C_CKV:C_KR], u_a[:, C_KR:C_GA], kvlg_ref[...], w_ukv_ref[...],
                  gk_ref, cs, k_ref, v_ref, rows)

        u_ga = jnp.dot(h_main, w_in_ref[:, C_GA:C_PIN], preferred_element_type=F32)
        sga_ref[0, rows, :] = _silu(u_ga).astype(BF16)

        pin = jnp.dot(h_scr[i * nh:(i + 1) * nh + 2 * HALO, :], w_in_ref[:, C_PIN:C_GP],
                      preferred_element_type=F32)
        u_gp = jnp.dot(h_main, w_in_ref[:, C_GP:C_END], preferred_element_type=F32)
        tok_head = t * tm + i * nh + lax.broadcasted_iota(jnp.int32, (EDGE, LANES), 0)
        tok_tail = tok_head + (nh - EDGE)
        for g, w in enumerate(POOL_WINDOWS):
            c0 = g * POOL_GROUP_DIM
            c1 = c0 + POOL_GROUP_DIM
            pg = pin[:, c0:c1]
            ws = _window_sum(pg, w)[HALO:HALO + nh]

            def inv_count(tok):
                lo = jnp.maximum(tok - w // 2, 0)
                hi = jnp.minimum(tok - w // 2 + w, seq)
                return 1.0 / (hi - lo).astype(F32)

            inv_cnt = jnp.concatenate(
                [inv_count(tok_head), jnp.full((nh - 2 * EDGE, LANES), 1.0 / w, F32),
                 inv_count(tok_tail)], axis=0)
            pooled = ws * inv_cnt - pg[HALO:HALO + nh]
            y = jnp.dot(pooled.astype(BF16), w_pool_ref[g], preferred_element_type=F32)
            brp_ref[0, rows, c0:c1] = (_silu(u_gp[:, c0:c1]) * (y * ps_ref[:, c0:c1])).astype(BF16)


def _proj(x, mod3, norm_g, w_in_r, qlg, w_uq_r, kvlg, w_ukv_b, gq, gk, cs, w_pool_b, ps, *,
          tm, nsub):
    b, s, d = x.shape
    nt = s // tm
    hb = tm // HALO
    nhb = s // HALO
    const2 = lambda bi, ti: (0, 0)
    kern = functools.partial(_proj_kernel, tm=tm, seq=s, nsub=nsub)
    return pl.pallas_call(
        kern,
        grid=(b, nt),
        in_specs=[
            pl.BlockSpec((1, tm, d), lambda bi, ti: (bi, ti, 0)),
            pl.BlockSpec((1, HALO, d), lambda bi, ti: (bi, jnp.maximum(ti * hb - 1, 0), 0)),
            pl.BlockSpec((1, HALO, d), lambda bi, ti: (bi, jnp.minimum((ti + 1) * hb, nhb - 1), 0)),
            pl.BlockSpec((1, 1, 3 * d), lambda bi, ti: (bi, 0, 0)),
            pl.BlockSpec((1, d), const2),
            pl.BlockSpec((d, C_END), const2),
            pl.BlockSpec((1, Q_LORA_RANK), const2),
            pl.BlockSpec((Q_LORA_RANK, MLA_HEADS * HEAD_COLS), const2),
            pl.BlockSpec((1, KV_LORA_RANK), const2),
            pl.BlockSpec((KV_LORA_RANK, MLA_HEADS * HEAD_COLS), const2),
            pl.BlockSpec((1, 3 * LANES), const2),
            pl.BlockSpec((1, 3 * LANES), const2),
            pl.BlockSpec((tm, LANES), lambda bi, ti: (ti, 0)),
            pl.BlockSpec((len(POOL_WINDOWS), POOL_GROUP_DIM, POOL_GROUP_DIM), lambda bi, ti: (0, 0, 0)),
            pl.BlockSpec((1, D_POOL), const2),
        ],
        out_specs=[
            pl.BlockSpec((1, MLA_HEADS, tm // Q_BLOCK, Q_BLOCK, QK_AUG_DIM),
                         lambda bi, ti: (bi, 0, ti, 0, 0)),
            pl.BlockSpec((1, MLA_HEADS, tm, QK_AUG_DIM), lambda bi, ti: (bi, 0, ti, 0)),
            pl.BlockSpec((1, MLA_HEADS, tm, V_AUG_DIM), lambda bi, ti: (bi, 0, ti, 0)),
            pl.BlockSpec((1, tm, D_ATTN), lambda bi, ti: (bi, ti, 0)),
            pl.BlockSpec((1, tm, D_POOL), lambda bi, ti: (bi, ti, 0)),
        ],
        out_shape=[
            jax.ShapeDtypeStruct((b, MLA_HEADS, s // Q_BLOCK, Q_BLOCK, QK_AUG_DIM), BF16),
            jax.ShapeDtypeStruct((b, MLA_HEADS, s, QK_AUG_DIM), BF16),
            jax.ShapeDtypeStruct((b, MLA_HEADS, s, V_AUG_DIM), BF16),
            jax.ShapeDtypeStruct((b, s, D_ATTN), BF16),
            jax.ShapeDtypeStruct((b, s, D_POOL), BF16),
        ],
        scratch_shapes=[
            pltpu.VMEM((tm + 2 * HALO, d), BF16),
        ],
        compiler_params=pltpu.CompilerParams(
            dimension_semantics=("parallel", "parallel"),
            vmem_limit_bytes=VMEM_LIMIT),
        name="proj",
    )(x, x, x, mod3, norm_g, w_in_r, qlg, w_uq_r, kvlg, w_ukv_b, gq, gk, cs, w_pool_b, ps)


def _ctxkv_kernel(x_ref, mod_ref, ng_ref, w_kv_ref, kvlg_ref, w_ukv_ref, gk_ref, cs_ref,
                  k_ref, v_ref):
    d = x_ref.shape[-1]
    shift = mod_ref[0, :, 0:d]
    scale = mod_ref[0, :, d:2 * d]
    a = ng_ref[...] * (1.0 + scale)
    for bi in range(x_ref.shape[0]):
        xv = x_ref[bi]
        ms = jnp.mean(xv * xv, axis=-1, keepdims=True)
        hc = (xv * lax.rsqrt(ms + NORM_EPS) * a + shift).astype(BF16)
        u = jnp.dot(hc, w_kv_ref[...], preferred_element_type=F32)
        _kv_heads(u[:, 0:KV_LORA_RANK], u[:, KV_LORA_RANK:], kvlg_ref[...], w_ukv_ref[...],
                  gk_ref, cs_ref[...], k_ref.at[pl.ds(bi, 1)], v_ref.at[pl.ds(bi, 1)],
                  slice(None))


def _ctxkv(ctx, mod3, norm_g, w_kv, kvlg, w_ukv_b, gk, cs_ctx, *, bb):
    b, lc, d = ctx.shape
    const2 = lambda bi: (0, 0)
    nkv = KV_LORA_RANK + 2 * QK_ROPE_DIM
    return pl.pallas_call(
        _ctxkv_kernel,
        grid=(b // bb,),
        in_specs=[
            pl.BlockSpec((bb, lc, d), lambda bi: (bi, 0, 0)),
            pl.BlockSpec((1, 1, 3 * d), lambda bi: (b, 0, 0)),
            pl.BlockSpec((1, d), const2),
            pl.BlockSpec((d, nkv), const2),
            pl.BlockSpec((1, KV_LORA_RANK), const2),
            pl.BlockSpec((KV_LORA_RANK, MLA_HEADS * HEAD_COLS), const2),
            pl.BlockSpec((1, 3 * LANES), const2),
            pl.BlockSpec((lc, LANES), const2),
        ],
        out_specs=[
            pl.BlockSpec((bb, MLA_HEADS, lc, QK_AUG_DIM), lambda bi: (bi, 0, 0, 0)),
            pl.BlockSpec((bb, MLA_HEADS, lc, V_AUG_DIM), lambda bi: (bi, 0, 0, 0)),
        ],
        out_shape=[
            jax.ShapeDtypeStruct((b, MLA_HEADS, lc, QK_AUG_DIM), BF16),
            jax.ShapeDtypeStruct((b, MLA_HEADS, lc, V_AUG_DIM), BF16),
        ],
        compiler_params=pltpu.CompilerParams(
            dimension_semantics=("parallel",), vmem_limit_bytes=VMEM_LIMIT),
        name="ctxkv",
    )(ctx, mod3, norm_g, w_kv, kvlg, w_ukv_b, gk, cs_ctx)


def _lane_tile(a, n):
    return a if n == 1 else jnp.concatenate([a] * n, axis=1)


def _attn_out_kernel(q_ref, k_ref, v_ref, kc_ref, vc_ref, sga_ref, brp_ref, x_ref, mod_ref,
                     w_ref, o_ref, attn_scr, *stat_scr, tk, running_max):
    nb, ib, dk = q_ref.shape[2:]
    rows = nb * ib
    nk = k_ref.shape[2] // tk
    dv = V_HEAD_DIM
    d = x_ref.shape[-1]

    def keys(h, c):
        return kc_ref[0, h] if c == 0 else k_ref[0, h, (c - 1) * tk:c * tk, :]

    def values(h, c):
        return vc_ref[0, h] if c == 0 else v_ref[0, h, (c - 1) * tk:c * tk, :]

    def scores(h, c):
        q = q_ref[0, h].reshape(rows, dk)
        return lax.dot_general(q, keys(h, c), (((1,), (1,)), ((), ())),
                               preferred_element_type=F32)

    gate = mod_ref[0, :, 2 * d:3 * d]
    o_ref[0] = x_ref[0] + gate * jnp.dot(brp_ref[0], w_ref[D_ATTN:, :],
                                         preferred_element_type=F32)

    if running_max:
        m_scr, acc_scr = stat_scr

        def accumulate(h, s, vb):
            m_prev = m_scr[h]
            m_next = jnp.maximum(m_prev, jnp.max(s, axis=-1, keepdims=True))
            alpha = jnp.exp2(m_prev - m_next)
            p = jnp.exp2(s - _lane_tile(m_next, s.shape[1] // LANES))
            pv = jnp.dot(p.astype(BF16), vb, preferred_element_type=F32)
            acc_scr[h] = _lane_tile(alpha, V_AUG_DIM // LANES) * acc_scr[h] + pv
            m_scr[h] = m_next

        m_scr[...] = jnp.full(m_scr.shape, -jnp.inf, F32)
        acc_scr[...] = jnp.zeros(acc_scr.shape, F32)
        for c in range(nk + 1):
            for h in range(MLA_HEADS):
                accumulate(h, scores(h, c), values(h, c))
        for h in range(MLA_HEADS):
            attn_scr[h] = acc_scr[h, :, 0:dv] / acc_scr[h, :, dv:2 * dv]
    else:
        acc = [None] * MLA_HEADS
        for c in range(nk + 1):
            for h in range(MLA_HEADS):
                p = jnp.exp2(scores(h, c)).astype(BF16)
                pv = jnp.dot(p, values(h, c), preferred_element_type=F32)
                acc[h] = pv if acc[h] is None else acc[h] + pv
        for h in range(MLA_HEADS):
            attn_scr[h] = acc[h][:, 0:dv] / acc[h][:, dv:2 * dv]
    pair = 2 * dv
    for h0 in range(0, MLA_HEADS, 2):
        attn = jnp.concatenate(
            [jnp.concatenate([attn_scr[h, pl.ds(il, nb, stride=ib), :] for il in range(ib)],
                             axis=0) for h in (h0, h0 + 1)], axis=1)
        cols = slice(h0 * dv, h0 * dv + pair)
        br_a = (sga_ref[0, :, cols].astype(F32) * attn).astype(BF16)
        o_ref[0] = o_ref[0] + gate * jnp.dot(br_a, w_ref[cols, :], preferred_element_type=F32)


def _attn_out(q5, k, v, kc, vc, sga, brp, x, mod3, w_out_b, *, ib, tk, running_max):
    b, h, nb, _, dk = q5.shape
    _, s, d = x.shape
    lc = kc.shape[2]
    dva = v.shape[-1]
    rows = nb * ib
    kern = functools.partial(_attn_out_kernel, tk=tk, running_max=running_max)
    stat_scr = [pltpu.VMEM((h, rows, LANES), F32), pltpu.VMEM((h, rows, dva), F32)]
    resident = dict(pipeline_mode=pl.Buffered(1))
    row_tile = lambda bi, ti: (bi, ti, 0)
    return pl.pallas_call(
        kern,
        grid=(b, Q_BLOCK // ib),
        in_specs=[
            pl.BlockSpec((1, h, nb, ib, dk), lambda bi, ti: (bi, 0, 0, ti, 0)),
            pl.BlockSpec((1, h, s, dk), lambda bi, ti: (bi, 0, 0, 0)),
            pl.BlockSpec((1, h, s, dva), lambda bi, ti: (bi, 0, 0, 0)),
            pl.BlockSpec((1, h, lc, dk), lambda bi, ti: (bi, 0, 0, 0), **resident),
            pl.BlockSpec((1, h, lc, dva), lambda bi, ti: (bi, 0, 0, 0), **resident),
            pl.BlockSpec((1, rows, D_ATTN), row_tile),
            pl.BlockSpec((1, rows, D_POOL), row_tile),
            pl.BlockSpec((1, rows, d), row_tile),
            pl.BlockSpec((1, 1, 3 * d), lambda bi, ti: (bi, 0, 0)),
            pl.BlockSpec((D_ATTN + D_POOL, d), lambda bi, ti: (0, 0), **resident),
        ],
        out_specs=pl.BlockSpec((1, rows, d), row_tile),
        out_shape=jax.ShapeDtypeStruct((b, s, d), F32),
        scratch_shapes=[pltpu.VMEM((h, rows, V_HEAD_DIM), F32)] + (stat_scr if running_max else []),
        compiler_params=pltpu.CompilerParams(
            dimension_semantics=("parallel", "arbitrary"),
            vmem_limit_bytes=VMEM_LIMIT),
        name="attn_out_rmax" if running_max else "attn_out",
    )(q5, k, v, kc, vc, sga, brp, x, mod3, w_out_b)


def _rotate_half_cols(w, signed):
    q = QK_ROPE_DIM // 4
    a, b, c, d = (w[..., i * q:(i + 1) * q] for i in range(4))
    if signed:
        return jnp.concatenate([-b, a, -d, c], axis=-1)
    return jnp.concatenate([b, a, d, c], axis=-1)


def _rope_table(seq):
    rows = seq // GRID_W
    row = np.repeat(np.arange(rows, dtype=np.float32), GRID_W)
    col = np.tile(np.arange(GRID_W, dtype=np.float32), rows)
    n_freq = QK_ROPE_DIM // 4
    inv = np.float32(ROPE_BASE) ** (-np.arange(n_freq, dtype=np.float32) / np.float32(n_freq))
    ang_r = row[:, None] * inv
    ang_c = col[:, None] * inv
    ang = np.concatenate([ang_r, ang_r, ang_c, ang_c], axis=-1).astype(np.float64)
    table = np.concatenate([np.cos(ang), np.sin(ang)], axis=-1)
    return jnp.asarray(table, dtype=F32)


def kernel(x, c, ctx, c_ctx, w_mod, b_mod, norm_g, w_in, q_lora_g, w_uq, kv_lora_g, w_ukv,
           q_norm_g, k_norm_g, w_pool, pool_scale, w_out):
    b, s, d = x.shape
    lc = ctx.shape[1]
    depth = w_mod.shape[0]
    assert depth == 1, "single-layer block"
    l = 0

    cc = jnp.concatenate([c, c_ctx[None, :], jnp.zeros((MOD_ROWS - b - 1, d), F32)], axis=0)
    mod = _modulation(cc, w_mod, b_mod[l][None, :], l)
    mod3 = mod[:, None, :]

    wi = w_in[l]
    o_cq, o_ckv, o_kr = 0, Q_LORA_RANK, Q_LORA_RANK + KV_LORA_RANK
    o_ga = o_kr + QK_ROPE_DIM
    o_pin = o_ga + D_ATTN
    o_gp = o_pin + D_POOL
    w_kr = wi[:, o_kr:o_ga]
    w_in_r = jnp.concatenate(
        [wi[:, o_cq:o_kr], w_kr, _rotate_half_cols(w_kr, True), wi[:, o_ga:o_pin],
         wi[:, o_pin:o_gp], wi[:, o_gp:]], axis=1).astype(BF16)
    wq = w_uq[l].reshape(Q_LORA_RANK, MLA_HEADS, QK_HEAD_DIM)
    wq_rope = wq[:, :, QK_NOPE_DIM:]
    w_uq_r = jnp.concatenate([wq, _rotate_half_cols(wq_rope, True)], axis=-1)
    w_uq_r = w_uq_r.reshape(Q_LORA_RANK, MLA_HEADS * HEAD_COLS).astype(BF16)
    w_ukv_b = w_ukv[l].astype(BF16)
    w_pool_b = w_pool[l].astype(BF16)
    w_out_b = w_out[l].astype(BF16)

    bound = (Q_SCALE * QK_HEAD_DIM) * jnp.max(jnp.abs(q_norm_g[l])) * jnp.max(jnp.abs(k_norm_g[l]))

    def gain_row(g, const, tail0):
        gains = jnp.concatenate([g, _rotate_half_cols(g[QK_NOPE_DIM:], False)]) * const
        tail = jnp.zeros((LANES,), F32).at[QK_ROPE_DIM].set(tail0)
        return jnp.concatenate([gains, tail])[None, :]

    gq = gain_row(q_norm_g[l], Q_SCALE * math.sqrt(QK_HEAD_DIM), -bound)
    gk = gain_row(k_norm_g[l], math.sqrt(QK_HEAD_DIM), 1.0)
    cs = _rope_table(s)
    cs_ctx = jnp.concatenate([jnp.ones((lc, QK_ROPE_DIM), F32),
                              jnp.zeros((lc, QK_ROPE_DIM), F32)], axis=1)

    ng = norm_g[l][None, :]
    qlg = q_lora_g[l][None, :]
    kvlg = kv_lora_g[l][None, :]
    ps = pool_scale[l][None, :]

    q, k, v, sga, brp = _proj(x, mod3, ng, w_in_r, qlg, w_uq_r, kvlg, w_ukv_b, gq, gk, cs,
                              w_pool_b, ps, tm=1024, nsub=4)
    kc, vc = _ctxkv(ctx, mod3, ng, w_in_r[:, C_CKV:C_GA], kvlg, w_ukv_b, gk, cs_ctx,
                    bb=b // 2 if b % 2 == 0 else b)
    args = (q, k, v, kc, vc, sga, brp, x, mod3, w_out_b)
    attn_out = functools.partial(_attn_out, ib=16, tk=512)
    return lax.cond(2.0 * bound < MAX_SHIFT_RANGE,
                    lambda a: attn_out(*a, running_max=False),
                    lambda a: attn_out(*a, running_max=True), args)
```

```python
import functools
import math

import jax
import jax.numpy as jnp
import numpy as np
from jax import lax
from jax.experimental import pallas as pl
from jax.experimental.pallas import tpu as pltpu

F32 = jnp.float32
BF16 = jnp.bfloat16

GRID_W = 64
MLA_HEADS = 4
QK_NOPE_DIM = 128
QK_ROPE_DIM = 64
QK_HEAD_DIM = QK_NOPE_DIM + QK_ROPE_DIM
V_HEAD_DIM = 128
V_AUG_DIM = 2 * V_HEAD_DIM
QK_AUG_DIM = 256
MAX_SHIFT_RANGE = 110.0
Q_LORA_RANK = 256
KV_LORA_RANK = 128
ROPE_BASE = 10000.0
POOL_WINDOWS = (2, 4, 8, 16)
POOL_GROUP_DIM = 128
NORM_EPS = 1e-6
Q_BLOCK = 128
D_ATTN = MLA_HEADS * V_HEAD_DIM
D_POOL = len(POOL_WINDOWS) * POOL_GROUP_DIM

LANES = 128
HEAD_COLS = 2 * LANES
HALO = 16
EDGE = 8
assert max(POOL_WINDOWS) // 2 <= EDGE <= HALO
MOD_ROWS = 16

C_CQ = 0
C_CKV = C_CQ + Q_LORA_RANK
C_KR = C_CKV + KV_LORA_RANK
C_GA = C_KR + 2 * QK_ROPE_DIM
C_PIN = C_GA + D_ATTN
C_GP = C_PIN + D_POOL
C_END = C_GP + D_POOL

VMEM_LIMIT = 56 * 1024 * 1024

Q_SCALE = QK_HEAD_DIM ** -0.5 * math.log2(math.e)


def _silu(x):
    hx = 0.5 * x
    return hx + hx * jnp.tanh(hx)


def _window_sum(p, w):
    n = p.shape[0]
    up = lambda a, k: pltpu.roll(a, n - k, axis=0)
    down = lambda a, k: pltpu.roll(a, k, axis=0)
    if w == 2:
        return p + down(p, 1)
    acc = p + up(p, 1)
    span = 2
    while span < w // 2:
        acc = acc + up(acc, span)
        span *= 2
    return acc + down(acc, w // 2)


def _rms_rows(x, g):
    ms = jnp.mean(x * x, axis=-1, keepdims=True)
    return x * lax.rsqrt(ms + NORM_EPS) * g


def _mod_kernel(cc_ref, w_ref, b_ref, o_ref):
    a = _silu(cc_ref[...]).astype(BF16)
    o_ref[...] = jnp.dot(a, w_ref[0].astype(BF16), preferred_element_type=F32) + b_ref[...]


def _modulation(cc, w_mod, b_mod, layer):
    _, d, n = w_mod.shape
    tn = 768
    return pl.pallas_call(
        _mod_kernel,
        grid=(n // tn,),
        in_specs=[
            pl.BlockSpec((MOD_ROWS, d), lambda j: (0, 0)),
            pl.BlockSpec((1, d, tn), lambda j: (layer, 0, j)),
            pl.BlockSpec((1, tn), lambda j: (0, j)),
        ],
        out_specs=pl.BlockSpec((MOD_ROWS, tn), lambda j: (0, j)),
        out_shape=jax.ShapeDtypeStruct((MOD_ROWS, n), F32),
        name="mod",
    )(cc, w_mod, b_mod)


HEAD_SS_EPS = QK_HEAD_DIM * NORM_EPS


def _rope_pair(rr, csg):
    pr = rr * csg
    return pr + pltpu.roll(pr, QK_ROPE_DIM, axis=1)


def _half_rope_sq(rr):
    return 0.5 * (rr * rr)


def _rope_lanes():
    return lax.broadcasted_iota(jnp.int32, (1, LANES), 1) < QK_ROPE_DIM


def _kv_heads(ckv, kr2, kvlg, w_ukv, gk_ref, cs, k_ref, v_ref, rows):
    ckv_n = _rms_rows(ckv, kvlg).astype(BF16)
    kv = jnp.dot(ckv_n, w_ukv, preferred_element_type=F32)
    g_nope = gk_ref[:, 0:LANES]
    tail = gk_ref[:, 2 * LANES:3 * LANES]
    rope_lanes = _rope_lanes()
    k_rope = _rope_pair(kr2, cs * gk_ref[:, LANES:2 * LANES])
    sq_rope = _half_rope_sq(kr2)
    for h in range(MLA_HEADS):
        kn = kv[:, h * HEAD_COLS:h * HEAD_COLS + LANES]
        vv = kv[:, h * HEAD_COLS + LANES:(h + 1) * HEAD_COLS]
        ss = jnp.sum(kn * kn + sq_rope, axis=-1, keepdims=True)
        r = lax.rsqrt(ss + HEAD_SS_EPS)
        k_ref[0, h, rows, 0:LANES] = (kn * r * g_nope).astype(BF16)
        k_ref[0, h, rows, LANES:] = jnp.where(rope_lanes, k_rope * r, tail).astype(BF16)
        v_ref[0, h, rows, 0:V_HEAD_DIM] = vv.astype(BF16)
        v_ref[0, h, rows, V_HEAD_DIM:] = jnp.ones((vv.shape[0], V_AUG_DIM - V_HEAD_DIM), BF16)


def _proj_kernel(x_ref, xp_ref, xn_ref, mod_ref, ng_ref, w_in_ref, qlg_ref, w_uq_ref,
                 kvlg_ref, w_ukv_ref, gq_ref, gk_ref, cs_ref, w_pool_ref, ps_ref,
                 q_ref, k_ref, v_ref, sga_ref, brp_ref,
                 h_scr, *, tm, seq, nsub):
    t = pl.program_id(1)
    nt = pl.num_programs(1)
    d = x_ref.shape[-1]
    shift = mod_ref[0, :, 0:d]
    scale = mod_ref[0, :, d:2 * d]
    a = ng_ref[...] * (1.0 + scale)

    def hnorm(xv):
        ms = jnp.mean(xv * xv, axis=-1, keepdims=True)
        return xv * lax.rsqrt(ms + NORM_EPS) * a + shift

    nh = tm // nsub
    h_scr[0:HALO, :] = jnp.where(t > 0, hnorm(xp_ref[0]), 0.0).astype(BF16)
    for i in range(nsub):
        h_scr[HALO + i * nh:HALO + (i + 1) * nh, :] = hnorm(x_ref[0, i * nh:(i + 1) * nh, :]).astype(BF16)
    h_scr[HALO + tm:, :] = jnp.where(t < nt - 1, hnorm(xn_ref[0]), 0.0).astype(BF16)

    cs_all = cs_ref[...]
    gq_nope = gq_ref[:, 0:LANES]
    gq_rr = gq_ref[:, LANES:2 * LANES]
    gq_tail = gq_ref[:, 2 * LANES:3 * LANES]
    rope_lanes = _rope_lanes()
    assert (tm // nsub) % Q_BLOCK == 0

    for i in range(nsub):
        rows = slice(i * nh, (i + 1) * nh)
        h_main = h_scr[HALO + i * nh:HALO + (i + 1) * nh, :]
        cs = cs_all[rows]

        u_a = jnp.dot(h_main, w_in_ref[:, C_CQ:C_GA], preferred_element_type=F32)
        cq_n = _rms_rows(u_a[:, C_CQ:C_CKV], qlg_ref[...]).astype(BF16)
        qa = jnp.dot(cq_n, w_uq_ref[...], preferred_element_type=F32)
        csq = cs * gq_rr
        for h in range(MLA_HEADS):
            qn = qa[:, h * HEAD_COLS:h * HEAD_COLS + LANES]
            rr = qa[:, h * HEAD_COLS + LANES:(h + 1) * HEAD_COLS]
            ss = jnp.sum(qn * qn + _half_rope_sq(rr), axis=-1, keepdims=True)
            r = lax.rsqrt(ss + HEAD_SS_EPS)
            q_lo = (qn * r * gq_nope).astype(BF16)
            q_hi = jnp.where(rope_lanes, _rope_pair(rr, csq) * r, gq_tail).astype(BF16)
            for j in range(nh // Q_BLOCK):
                blk = i * (nh // Q_BLOCK) + j
                q_ref[0, h, blk, :, 0:LANES] = q_lo[j * Q_BLOCK:(j + 1) * Q_BLOCK]
                q_ref[0, h, blk, :, LANES:] = q_hi[j * Q_BLOCK:(j + 1) * Q_BLOCK]

        _kv_heads(u_a[:, C_CKV:C_KR], u_a[:, C_KR:C_GA], kvlg_ref[...], w_ukv_ref[...],
                  gk_ref, cs, k_ref, v_ref, rows)

        u_ga = jnp.dot(h_main, w_in_ref[:, C_GA:C_PIN], preferred_element_type=F32)
        sga_ref[0, rows, :] = _silu(u_ga).astype(BF16)

        pin = jnp.dot(h_scr[i * nh:(i + 1) * nh + 2 * HALO, :], w_in_ref[:, C_PIN:C_GP],
                      preferred_element_type=F32)
        u_gp = jnp.dot(h_main, w_in_ref[:, C_GP:C_END], preferred_element_type=F32)
        tok_head = t * tm + i * nh + lax.broadcasted_iota(jnp.int32, (EDGE, LANES), 0)
        tok_tail = tok_head + (nh - EDGE)
        for g, w in enumerate(POOL_WINDOWS):
            c0 = g * POOL_GROUP_DIM
            c1 = c0 + POOL_GROUP_DIM
            pg = pin[:, c0:c1]
            ws = _window_sum(pg, w)[HALO:HALO + nh]

            def inv_count(tok):
                lo = jnp.maximum(tok - w // 2, 0)
                hi = jnp.minimum(tok - w // 2 + w, seq)
                return 1.0 / (hi - lo).astype(F32)

            inv_cnt = jnp.concatenate(
                [inv_count(tok_head), jnp.full((nh - 2 * EDGE, LANES), 1.0 / w, F32),
                 inv_count(tok_tail)], axis=0)
            pooled = ws * inv_cnt - pg[HALO:HALO + nh]
            y = jnp.dot(pooled.astype(BF16), w_pool_ref[g], preferred_element_type=F32)
            brp_ref[0, rows, c0:c1] = (_silu(u_gp[:, c0:c1]) * (y * ps_ref[:, c0:c1])).astype(BF16)


def _proj(x, mod3, norm_g, w_in_r, qlg, w_uq_r, kvlg, w_ukv_b, gq, gk, cs, w_pool_b, ps, *,
          tm, nsub):
    b, s, d = x.shape
    nt = s // tm
    hb = tm // HALO
    nhb = s // HALO
    const2 = lambda bi, ti: (0, 0)
    kern = functools.partial(_proj_kernel, tm=tm, seq=s, nsub=nsub)
    return pl.pallas_call(
        kern,
        grid=(b, nt),
        in_specs=[
            pl.BlockSpec((1, tm, d), lambda bi, ti: (bi, ti, 0)),
            pl.BlockSpec((1, HALO, d), lambda bi, ti: (bi, jnp.maximum(ti * hb - 1, 0), 0)),
            pl.BlockSpec((1, HALO, d), lambda bi, ti: (bi, jnp.minimum((ti + 1) * hb, nhb - 1), 0)),
            pl.BlockSpec((1, 1, 3 * d), lambda bi, ti: (bi, 0, 0)),
            pl.BlockSpec((1, d), const2),
            pl.BlockSpec((d, C_END), const2),
            pl.BlockSpec((1, Q_LORA_RANK), const2),
            pl.BlockSpec((Q_LORA_RANK, MLA_HEADS * HEAD_COLS), const2),
            pl.BlockSpec((1, KV_LORA_RANK), const2),
            pl.BlockSpec((KV_LORA_RANK, MLA_HEADS * HEAD_COLS), const2),
            pl.BlockSpec((1, 3 * LANES), const2),
            pl.BlockSpec((1, 3 * LANES), const2),
            pl.BlockSpec((tm, LANES), lambda bi, ti: (ti, 0)),
            pl.BlockSpec((len(POOL_WINDOWS), POOL_GROUP_DIM, POOL_GROUP_DIM), lambda bi, ti: (0, 0, 0)),
            pl.BlockSpec((1, D_POOL), const2),
        ],
        out_specs=[
            pl.BlockSpec((1, MLA_HEADS, tm // Q_BLOCK, Q_BLOCK, QK_AUG_DIM),
                         lambda bi, ti: (bi, 0, ti, 0, 0)),
            pl.BlockSpec((1, MLA_HEADS, tm, QK_AUG_DIM), lambda bi, ti: (bi, 0, ti, 0)),
            pl.BlockSpec((1, MLA_HEADS, tm, V_AUG_DIM), lambda bi, ti: (bi, 0, ti, 0)),
            pl.BlockSpec((1, tm, D_ATTN), lambda bi, ti: (bi, ti, 0)),
            pl.BlockSpec((1, tm, D_POOL), lambda bi, ti: (bi, ti, 0)),
        ],
        out_shape=[
            jax.ShapeDtypeStruct((b, MLA_HEADS, s // Q_BLOCK, Q_BLOCK, QK_AUG_DIM), BF16),
            jax.ShapeDtypeStruct((b, MLA_HEADS, s, QK_AUG_DIM), BF16),
            jax.ShapeDtypeStruct((b, MLA_HEADS, s, V_AUG_DIM), BF16),
            jax.ShapeDtypeStruct((b, s, D_ATTN), BF16),
            jax.ShapeDtypeStruct((b, s, D_POOL), BF16),
        ],
        scratch_shapes=[
            pltpu.VMEM((tm + 2 * HALO, d), BF16),
        ],
        compiler_params=pltpu.CompilerParams(
            dimension_semantics=("parallel", "parallel"),
            vmem_limit_bytes=VMEM_LIMIT),
        name="proj",
    )(x, x, x, mod3, norm_g, w_in_r, qlg, w_uq_r, kvlg, w_ukv_b, gq, gk, cs, w_pool_b, ps)


def _ctxkv_kernel(x_ref, mod_ref, ng_ref, w_kv_ref, kvlg_ref, w_ukv_ref, gk_ref, cs_ref,
                  k_ref, v_ref):
    d = x_ref.shape[-1]
    shift = mod_ref[0, :, 0:d]
    scale = mod_ref[0, :, d:2 * d]
    a = ng_ref[...] * (1.0 + scale)
    for bi in range(x_ref.shape[0]):
        xv = x_ref[bi]
        ms = jnp.mean(xv * xv, axis=-1, keepdims=True)
        hc = (xv * lax.rsqrt(ms + NORM_EPS) * a + shift).astype(BF16)
        u = jnp.dot(hc, w_kv_ref[...], preferred_element_type=F32)
        _kv_heads(u[:, 0:KV_LORA_RANK], u[:, KV_LORA_RANK:], kvlg_ref[...], w_ukv_ref[...],
                  gk_ref, cs_ref[...], k_ref.at[pl.ds(bi, 1)], v_ref.at[pl.ds(bi, 1)],
                  slice(None))


def _ctxkv(ctx, mod3, norm_g, w_kv, kvlg, w_ukv_b, gk, cs_ctx, *, bb):
    b, lc, d = ctx.shape
    const2 = lambda bi: (0, 0)
    nkv = KV_LORA_RANK + 2 * QK_ROPE_DIM
    return pl.pallas_call(
        _ctxkv_kernel,
        grid=(b // bb,),
        in_specs=[
            pl.BlockSpec((bb, lc, d), lambda bi: (bi, 0, 0)),
            pl.BlockSpec((1, 1, 3 * d), lambda bi: (b, 0, 0)),
            pl.BlockSpec((1, d), const2),
            pl.BlockSpec((d, nkv), const2),
            pl.BlockSpec((1, KV_LORA_RANK), const2),
            pl.BlockSpec((KV_LORA_RANK, MLA_HEADS * HEAD_COLS), const2),
            pl.BlockSpec((1, 3 * LANES), const2),
            pl.BlockSpec((lc, LANES), const2),
        ],
        out_specs=[
            pl.BlockSpec((bb, MLA_HEADS, lc, QK_AUG_DIM), lambda bi: (bi, 0, 0, 0)),
            pl.BlockSpec((bb, MLA_HEADS, lc, V_AUG_DIM), lambda bi: (bi, 0, 0, 0)),
        ],
        out_shape=[
            jax.ShapeDtypeStruct((b, MLA_HEADS, lc, QK_AUG_DIM), BF16),
            jax.ShapeDtypeStruct((b, MLA_HEADS, lc, V_AUG_DIM), BF16),
        ],
        compiler_params=pltpu.CompilerParams(
            dimension_semantics=("parallel",), vmem_limit_bytes=VMEM_LIMIT),
        name="ctxkv",
    )(ctx, mod3, norm_g, w_kv, kvlg, w_ukv_b, gk, cs_ctx)


def _lane_tile(a, n):
    return a if n == 1 else jnp.concatenate([a] * n, axis=1)


def _attn_out_kernel(q_ref, k_ref, v_ref, kc_ref, vc_ref, sga_ref, brp_ref, x_ref, mod_ref,
                     w_ref, o_ref, attn_scr, *stat_scr, tk, running_max):
    nb, ib, dk = q_ref.shape[2:]
    rows = nb * ib
    nk = k_ref.shape[2] // tk
    dv = V_HEAD_DIM
    d = x_ref.shape[-1]

    def keys(h, c):
        return kc_ref[0, h] if c == 0 else k_ref[0, h, (c - 1) * tk:c * tk, :]

    def values(h, c):
        return vc_ref[0, h] if c == 0 else v_ref[0, h, (c - 1) * tk:c * tk, :]

    def scores(h, c):
        q = q_ref[0, h].reshape(rows, dk)
        return lax.dot_general(q, keys(h, c), (((1,), (1,)), ((), ())),
                               preferred_element_type=F32)

    gate = mod_ref[0, :, 2 * d:3 * d]
    o_ref[0] = x_ref[0] + gate * jnp.dot(brp_ref[0], w_ref[D_ATTN:, :],
                                         preferred_element_type=F32)

    if running_max:
        m_scr, acc_scr = stat_scr

        def accumulate(h, s, vb):
            m_prev = m_scr[h]
            m_next = jnp.maximum(m_prev, jnp.max(s, axis=-1, keepdims=True))
            alpha = jnp.exp2(m_prev - m_next)
            p = jnp.exp2(s - _lane_tile(m_next, s.shape[1] // LANES))
            pv = jnp.dot(p.astype(BF16), vb, preferred_element_type=F32)
            acc_scr[h] = _lane_tile(alpha, V_AUG_DIM // LANES) * acc_scr[h] + pv
            m_scr[h] = m_next

        m_scr[...] = jnp.full(m_scr.shape, -jnp.inf, F32)
        acc_scr[...] = jnp.zeros(acc_scr.shape, F32)
        for c in range(nk + 1):
            for h in range(MLA_HEADS):
                accumulate(h, scores(h, c), values(h, c))
        for h in range(MLA_HEADS):
            attn_scr[h] = acc_scr[h, :, 0:dv] / acc_scr[h, :, dv:2 * dv]
    else:
        acc = [None] * MLA_HEADS
        for c in range(nk + 1):
            for h in range(MLA_HEADS):
                p = jnp.exp2(scores(h, c)).astype(BF16)
                pv = jnp.dot(p, values(h, c), preferred_element_type=F32)
                acc[h] = pv if acc[h] is None else acc[h] + pv
        for h in range(MLA_HEADS):
            attn_scr[h] = acc[h][:, 0:dv] / acc[h][:, dv:2 * dv]
    pair = 2 * dv
    for h0 in range(0, MLA_HEADS, 2):
        attn = jnp.concatenate(
            [jnp.concatenate([attn_scr[h, pl.ds(il, nb, stride=ib), :] for il in range(ib)],
                             axis=0) for h in (h0, h0 + 1)], axis=1)
        cols = slice(h0 * dv, h0 * dv + pair)
        br_a = (sga_ref[0, :, cols].astype(F32) * attn).astype(BF16)
        o_ref[0] = o_ref[0] + gate * jnp.dot(br_a, w_ref[cols, :], preferred_element_type=F32)


def _attn_out(q5, k, v, kc, vc, sga, brp, x, mod3, w_out_b, *, ib, tk, running_max):
    b, h, nb, _, dk = q5.shape
    _, s, d = x.shape
    lc = kc.shape[2]
    dva = v.shape[-1]
    rows = nb * ib
    kern = functools.partial(_attn_out_kernel, tk=tk, running_max=running_max)
    stat_scr = [pltpu.VMEM((h, rows, LANES), F32), pltpu.VMEM((h, rows, dva), F32)]
    resident = dict(pipeline_mode=pl.Buffered(1))
    row_tile = lambda bi, ti: (bi, ti, 0)
    return pl.pallas_call(
        kern,
        grid=(b, Q_BLOCK // ib),
        in_specs=[
            pl.BlockSpec((1, h, nb, ib, dk), lambda bi, ti: (bi, 0, 0, ti, 0)),
            pl.BlockSpec((1, h, s, dk), lambda bi, ti: (bi, 0, 0, 0)),
            pl.BlockSpec((1, h, s, dva), lambda bi, ti: (bi, 0, 0, 0)),
            pl.BlockSpec((1, h, lc, dk), lambda bi, ti: (bi, 0, 0, 0), **resident),
            pl.BlockSpec((1, h, lc, dva), lambda bi, ti: (bi, 0, 0, 0), **resident),
            pl.BlockSpec((1, rows, D_ATTN), row_tile),
            pl.BlockSpec((1, rows, D_POOL), row_tile),
            pl.BlockSpec((1, rows, d), row_tile),
            pl.BlockSpec((1, 1, 3 * d), lambda bi, ti: (bi, 0, 0)),
            pl.BlockSpec((D_ATTN + D_POOL, d), lambda bi, ti: (0, 0), **resident),
        ],
        out_specs=pl.BlockSpec((1, rows, d), row_tile),
        out_shape=jax.ShapeDtypeStruct((b, s, d), F32),
        scratch_shapes=[pltpu.VMEM((h, rows, V_HEAD_DIM), F32)] + (stat_scr if running_max else []),
        compiler_params=pltpu.CompilerParams(
            dimension_semantics=("parallel", "arbitrary"),
            vmem_limit_bytes=VMEM_LIMIT),
        name="attn_out_rmax" if running_max else "attn_out",
    )(q5, k, v, kc, vc, sga, brp, x, mod3, w_out_b)


def _rotate_half_cols(w, signed):
    q = QK_ROPE_DIM // 4
    a, b, c, d = (w[..., i * q:(i + 1) * q] for i in range(4))
    if signed:
        return jnp.concatenate([-b, a, -d, c], axis=-1)
    return jnp.concatenate([b, a, d, c], axis=-1)


def _rope_table(seq):
    rows = seq // GRID_W
    row = np.repeat(np.arange(rows, dtype=np.float32), GRID_W)
    col = np.tile(np.arange(GRID_W, dtype=np.float32), rows)
    n_freq = QK_ROPE_DIM // 4
    inv = np.float32(ROPE_BASE) ** (-np.arange(n_freq, dtype=np.float32) / np.float32(n_freq))
    ang_r = row[:, None] * inv
    ang_c = col[:, None] * inv
    ang = np.concatenate([ang_r, ang_r, ang_c, ang_c], axis=-1).astype(np.float64)
    table = np.concatenate([np.cos(ang), np.sin(ang)], axis=-1)
    return jnp.asarray(table, dtype=F32)


def kernel(x, c, ctx, c_ctx, w_mod, b_mod, norm_g, w_in, q_lora_g, w_uq, kv_lora_g, w_ukv,
           q_norm_g, k_norm_g, w_pool, pool_scale, w_out):
    b, s, d = x.shape
    lc = ctx.shape[1]
    depth = w_mod.shape[0]
    assert depth == 1, "single-layer block"
    l = 0

    cc = jnp.concatenate([c, c_ctx[None, :], jnp.zeros((MOD_ROWS - b - 1, d), F32)], axis=0)
    mod = _modulation(cc, w_mod, b_mod[l][None, :], l)
    mod3 = mod[:, None, :]

    wi = w_in.astype(BF16)[l]
    o_cq, o_ckv, o_kr = 0, Q_LORA_RANK, Q_LORA_RANK + KV_LORA_RANK
    o_ga = o_kr + QK_ROPE_DIM
    o_pin = o_ga + D_ATTN
    o_gp = o_pin + D_POOL
    w_kr = wi[:, o_kr:o_ga]
    w_in_r = jnp.concatenate(
        [wi[:, o_cq:o_kr], w_kr, _rotate_half_cols(w_kr, True), wi[:, o_ga:o_pin],
         wi[:, o_pin:o_gp], wi[:, o_gp:]], axis=1)
    wq = w_uq.astype(BF16)[l].reshape(Q_LORA_RANK, MLA_HEADS, QK_HEAD_DIM)
    wq_rope = wq[:, :, QK_NOPE_DIM:]
    w_uq_r = jnp.concatenate([wq, _rotate_half_cols(wq_rope, True)], axis=-1)
    w_uq_r = w_uq_r.reshape(Q_LORA_RANK, MLA_HEADS * HEAD_COLS)
    w_ukv_b = w_ukv.astype(BF16)[l]
    w_pool_b = w_pool.astype(BF16)[l]
    w_out_b = w_out.astype(BF16)[l]

    bound = (Q_SCALE * QK_HEAD_DIM) * jnp.max(jnp.abs(q_norm_g[l])) * jnp.max(jnp.abs(k_norm_g[l]))

    def gain_row(g, const, tail0):
        gains = jnp.concatenate([g, _rotate_half_cols(g[QK_NOPE_DIM:], False)]) * const
        tail = jnp.zeros((LANES,), F32).at[QK_ROPE_DIM].set(tail0)
        return jnp.concatenate([gains, tail])[None, :]

    gq = gain_row(q_norm_g[l], Q_SCALE * math.sqrt(QK_HEAD_DIM), -bound)
    gk = gain_row(k_norm_g[l], math.sqrt(QK_HEAD_DIM), 1.0)
    cs = _rope_table(s)
    cs_ctx = jnp.concatenate([jnp.ones((lc, QK_ROPE_DIM), F32),
                              jnp.zeros((lc, QK_ROPE_DIM), F32)], axis=1)

    ng = norm_g[l][None, :]
    qlg = q_lora_g[l][None, :]
    kvlg = kv_lora_g[l][None, :]
    ps = pool_scale[l][None, :]

    q, k, v, sga, brp = _proj(x, mod3, ng, w_in_r, qlg, w_uq_r, kvlg, w_ukv_b, gq, gk, cs,
                              w_pool_b, ps, tm=1024, nsub=4)
    kc, vc = _ctxkv(ctx, mod3, ng, w_in_r[:, C_CKV:C_GA], kvlg, w_ukv_b, gk, cs_ctx,
                    bb=b // 2 if b % 2 == 0 else b)
    args = (q, k, v, kc, vc, sga, brp, x, mod3, w_out_b)
    attn_out = functools.partial(_attn_out, ib=16, tk=512)
    return lax.cond(2.0 * bound < MAX_SHIFT_RANGE,
                    lambda a: attn_out(*a, running_max=False),
                    lambda a: attn_out(*a, running_max=True), args)
```

```python
import functools
import math

import jax
import jax.numpy as jnp
import numpy as np
from jax import lax
from jax.experimental import pallas as pl
from jax.experimental.pallas import tpu as pltpu

F32 = jnp.float32
BF16 = jnp.bfloat16

GRID_W = 64
MLA_HEADS = 4
QK_NOPE_DIM = 128
QK_ROPE_DIM = 64
QK_HEAD_DIM = QK_NOPE_DIM + QK_ROPE_DIM
V_HEAD_DIM = 128
V_AUG_DIM = 2 * V_HEAD_DIM
QK_AUG_DIM = 256
MAX_SHIFT_RANGE = 110.0
Q_LORA_RANK = 256
KV_LORA_RANK = 128
ROPE_BASE = 10000.0
POOL_WINDOWS = (2, 4, 8, 16)
POOL_GROUP_DIM = 128
NORM_EPS = 1e-6
Q_BLOCK = 128
D_ATTN = MLA_HEADS * V_HEAD_DIM
D_POOL = len(POOL_WINDOWS) * POOL_GROUP_DIM

LANES = 128
HEAD_COLS = 2 * LANES
HALO = 16
EDGE = 8
assert max(POOL_WINDOWS) // 2 <= EDGE <= HALO
MOD_ROWS = 16

C_CQ = 0
C_CKV = C_CQ + Q_LORA_RANK
C_KR = C_CKV + KV_LORA_RANK
C_GA = C_KR + 2 * QK_ROPE_DIM
C_PIN = C_GA + D_ATTN
C_GP = C_PIN + D_POOL
C_END = C_GP + D_POOL

VMEM_LIMIT = 56 * 1024 * 1024
MOD_TN = 768
PROJ_TM = 1024
PROJ_NSUB = 4
ATTN_IB = 16
ATTN_TK = 512

Q_SCALE = QK_HEAD_DIM ** -0.5 * math.log2(math.e)


def _silu(x):
    hx = 0.5 * x
    return hx + hx * jnp.tanh(hx)


def _window_sum(p, w):
    n = p.shape[0]
    up = lambda a, k: pltpu.roll(a, n - k, axis=0)
    down = lambda a, k: pltpu.roll(a, k, axis=0)
    if w == 2:
        return p + down(p, 1)
    acc = p + up(p, 1)
    span = 2
    while span < w // 2:
        acc = acc + up(acc, span)
        span *= 2
    return acc + down(acc, w // 2)


def _rms_rows(x, g):
    ms = jnp.mean(x * x, axis=-1, keepdims=True)
    return x * lax.rsqrt(ms + NORM_EPS) * g


def _mod_kernel(cc_ref, w_ref, b_ref, o_ref):
    a = _silu(cc_ref[...]).astype(BF16)
    o_ref[...] = jnp.dot(a, w_ref[0].astype(BF16), preferred_element_type=F32) + b_ref[...]


def _modulation(cc, w_mod, b_mod, layer):
    _, d, n = w_mod.shape
    tn = MOD_TN
    return pl.pallas_call(
        _mod_kernel,
        grid=(n // tn,),
        in_specs=[
            pl.BlockSpec((MOD_ROWS, d), lambda j: (0, 0)),
            pl.BlockSpec((1, d, tn), lambda j: (layer, 0, j)),
            pl.BlockSpec((1, tn), lambda j: (0, j)),
        ],
        out_specs=pl.BlockSpec((MOD_ROWS, tn), lambda j: (0, j)),
        out_shape=jax.ShapeDtypeStruct((MOD_ROWS, n), F32),
        name="mod",
    )(cc, w_mod, b_mod)


HEAD_SS_EPS = QK_HEAD_DIM * NORM_EPS


def _rope_pair(rr, csg):
    pr = rr * csg
    return pr + pltpu.roll(pr, QK_ROPE_DIM, axis=1)


def _half_rope_sq(rr):
    return 0.5 * (rr * rr)


def _rope_lanes():
    return lax.broadcasted_iota(jnp.int32, (1, LANES), 1) < QK_ROPE_DIM


def _kv_heads(ckv, kr2, kvlg, w_ukv, gk_ref, cs, k_ref, v_ref, rows):
    ckv_n = _rms_rows(ckv, kvlg).astype(BF16)
    kv = jnp.dot(ckv_n, w_ukv, preferred_element_type=F32)
    g_nope = gk_ref[:, 0:LANES]
    tail = gk_ref[:, 2 * LANES:3 * LANES]
    rope_lanes = _rope_lanes()
    k_rope = _rope_pair(kr2, cs * gk_ref[:, LANES:2 * LANES])
    sq_rope = _half_rope_sq(kr2)
    for h in range(MLA_HEADS):
        kn = kv[:, h * HEAD_COLS:h * HEAD_COLS + LANES]
        vv = kv[:, h * HEAD_COLS + LANES:(h + 1) * HEAD_COLS]
        ss = jnp.sum(kn * kn + sq_rope, axis=-1, keepdims=True)
        r = lax.rsqrt(ss + HEAD_SS_EPS)
        k_ref[0, h, rows, 0:LANES] = (kn * r * g_nope).astype(BF16)
        k_ref[0, h, rows, LANES:] = jnp.where(rope_lanes, k_rope * r, tail).astype(BF16)
        v_ref[0, h, rows, 0:V_HEAD_DIM] = vv.astype(BF16)
        v_ref[0, h, rows, V_HEAD_DIM:] = jnp.ones((vv.shape[0], V_AUG_DIM - V_HEAD_DIM), BF16)


def _proj_kernel(x_ref, xp_ref, xn_ref, mod_ref, ng_ref, w_in_ref, qlg_ref, w_uq_ref,
                 kvlg_ref, w_ukv_ref, gq_ref, gk_ref, cs_ref, w_pool_ref, ps_ref,
                 q_ref, k_ref, v_ref, sga_ref, brp_ref,
                 h_scr, *, tm, seq, nsub):
    t = pl.program_id(1)
    nt = pl.num_programs(1)
    d = x_ref.shape[-1]
    shift = mod_ref[0, :, 0:d]
    scale = mod_ref[0, :, d:2 * d]
    a = ng_ref[...] * (1.0 + scale)

    def hnorm(xv):
        ms = jnp.mean(xv * xv, axis=-1, keepdims=True)
        return xv * lax.rsqrt(ms + NORM_EPS) * a + shift

    nh = tm // nsub
    h_scr[0:HALO, :] = jnp.where(t > 0, hnorm(xp_ref[0]), 0.0).astype(BF16)
    for i in range(nsub):
        h_scr[HALO + i * nh:HALO + (i + 1) * nh, :] = hnorm(x_ref[0, i * nh:(i + 1) * nh, :]).astype(BF16)
    h_scr[HALO + tm:, :] = jnp.where(t < nt - 1, hnorm(xn_ref[0]), 0.0).astype(BF16)

    cs_all = cs_ref[...]
    gq_nope = gq_ref[:, 0:LANES]
    gq_rr = gq_ref[:, LANES:2 * LANES]
    gq_tail = gq_ref[:, 2 * LANES:3 * LANES]
    rope_lanes = _rope_lanes()
    assert (tm // nsub) % Q_BLOCK == 0

    def h_rows(i):
        return h_scr[HALO + i * nh:HALO + (i + 1) * nh, :]

    for i in range(nsub):
        rows = slice(i * nh, (i + 1) * nh)
        cs = cs_all[rows]
        u_a = jnp.dot(h_rows(i), w_in_ref[:, C_CQ:C_GA], preferred_element_type=F32)
        _kv_heads(u_a[:, C_CKV:C_KR], u_a[:, C_KR:C_GA], kvlg_ref[...], w_ukv_ref[...],
                  gk_ref, cs, k_ref, v_ref, rows)
        cq_n = _rms_rows(u_a[:, C_CQ:C_CKV], qlg_ref[...]).astype(BF16)
        qa = jnp.dot(cq_n, w_uq_ref[...], preferred_element_type=F32)
        csq = cs * gq_rr
        for h in range(MLA_HEADS):
            qn = qa[:, h * HEAD_COLS:h * HEAD_COLS + LANES]
            rr = qa[:, h * HEAD_COLS + LANES:(h + 1) * HEAD_COLS]
            ss = jnp.sum(qn * qn + _half_rope_sq(rr), axis=-1, keepdims=True)
            r = lax.rsqrt(ss + HEAD_SS_EPS)
            q_lo = (qn * r * gq_nope).astype(BF16)
            q_hi = jnp.where(rope_lanes, _rope_pair(rr, csq) * r, gq_tail).astype(BF16)
            for j in range(nh // Q_BLOCK):
                blk = i * (nh // Q_BLOCK) + j
                q_ref[0, h, blk, :, 0:LANES] = q_lo[j * Q_BLOCK:(j + 1) * Q_BLOCK]
                q_ref[0, h, blk, :, LANES:] = q_hi[j * Q_BLOCK:(j + 1) * Q_BLOCK]

    for i in range(nsub):
        u_ga = jnp.dot(h_rows(i), w_in_ref[:, C_GA:C_PIN], preferred_element_type=F32)
        sga_ref[0, i * nh:(i + 1) * nh, :] = _silu(u_ga).astype(BF16)

    for i in range(nsub):
        rows = slice(i * nh, (i + 1) * nh)
        pin = jnp.dot(h_scr[i * nh:(i + 1) * nh + 2 * HALO, :], w_in_ref[:, C_PIN:C_GP],
                      preferred_element_type=F32)
        u_gp = jnp.dot(h_rows(i), w_in_ref[:, C_GP:C_END], preferred_element_type=F32)
        tok_head = t * tm + i * nh + lax.broadcasted_iota(jnp.int32, (EDGE, LANES), 0)
        tok_tail = tok_head + (nh - EDGE)
        for g, w in enumerate(POOL_WINDOWS):
            c0 = g * POOL_GROUP_DIM
            c1 = c0 + POOL_GROUP_DIM
            pg = pin[:, c0:c1]
            ws = _window_sum(pg, w)[HALO:HALO + nh]

            def inv_count(tok):
                lo = jnp.maximum(tok - w // 2, 0)
                hi = jnp.minimum(tok - w // 2 + w, seq)
                return 1.0 / (hi - lo).astype(F32)

            inv_cnt = jnp.concatenate(
                [inv_count(tok_head), jnp.full((nh - 2 * EDGE, LANES), 1.0 / w, F32),
                 inv_count(tok_tail)], axis=0)
            pooled = ws * inv_cnt - pg[HALO:HALO + nh]
            y = jnp.dot(pooled.astype(BF16), w_pool_ref[g], preferred_element_type=F32)
            brp_ref[0, rows, c0:c1] = (_silu(u_gp[:, c0:c1]) * (y * ps_ref[:, c0:c1])).astype(BF16)


def _proj(x, mod3, norm_g, w_in_r, qlg, w_uq_r, kvlg, w_ukv_b, gq, gk, cs, w_pool_b, ps, *,
          tm, nsub):
    b, s, d = x.shape
    nt = s // tm
    hb = tm // HALO
    nhb = s // HALO
    const2 = lambda bi, ti: (0, 0)
    kern = functools.partial(_proj_kernel, tm=tm, seq=s, nsub=nsub)
    return pl.pallas_call(
        kern,
        grid=(b, nt),
        in_specs=[
            pl.BlockSpec((1, tm, d), lambda bi, ti: (bi, ti, 0)),
            pl.BlockSpec((1, HALO, d), lambda bi, ti: (bi, jnp.maximum(ti * hb - 1, 0), 0)),
            pl.BlockSpec((1, HALO, d), lambda bi, ti: (bi, jnp.minimum((ti + 1) * hb, nhb - 1), 0)),
            pl.BlockSpec((1, 1, 3 * d), lambda bi, ti: (bi, 0, 0)),
            pl.BlockSpec((1, d), const2),
            pl.BlockSpec((d, C_END), const2),
            pl.BlockSpec((1, Q_LORA_RANK), const2),
            pl.BlockSpec((Q_LORA_RANK, MLA_HEADS * HEAD_COLS), const2),
            pl.BlockSpec((1, KV_LORA_RANK), const2),
            pl.BlockSpec((KV_LORA_RANK, MLA_HEADS * HEAD_COLS), const2),
            pl.BlockSpec((1, 3 * LANES), const2),
            pl.BlockSpec((1, 3 * LANES), const2),
            pl.BlockSpec((tm, LANES), lambda bi, ti: (ti, 0)),
            pl.BlockSpec((len(POOL_WINDOWS), POOL_GROUP_DIM, POOL_GROUP_DIM), lambda bi, ti: (0, 0, 0)),
            pl.BlockSpec((1, D_POOL), const2),
        ],
        out_specs=[
            pl.BlockSpec((1, MLA_HEADS, tm // Q_BLOCK, Q_BLOCK, QK_AUG_DIM),
                         lambda bi, ti: (bi, 0, ti, 0, 0)),
            pl.BlockSpec((1, MLA_HEADS, tm, QK_AUG_DIM), lambda bi, ti: (bi, 0, ti, 0)),
            pl.BlockSpec((1, MLA_HEADS, tm, V_AUG_DIM), lambda bi, ti: (bi, 0, ti, 0)),
            pl.BlockSpec((1, tm, D_ATTN), lambda bi, ti: (bi, ti, 0)),
            pl.BlockSpec((1, tm, D_POOL), lambda bi, ti: (bi, ti, 0)),
        ],
        out_shape=[
            jax.ShapeDtypeStruct((b, MLA_HEADS, s // Q_BLOCK, Q_BLOCK, QK_AUG_DIM), BF16),
            jax.ShapeDtypeStruct((b, MLA_HEADS, s, QK_AUG_DIM), BF16),
            jax.ShapeDtypeStruct((b, MLA_HEADS, s, V_AUG_DIM), BF16),
            jax.ShapeDtypeStruct((b, s, D_ATTN), BF16),
            jax.ShapeDtypeStruct((b, s, D_POOL), BF16),
        ],
        scratch_shapes=[
            pltpu.VMEM((tm + 2 * HALO, d), BF16),
        ],
        compiler_params=pltpu.CompilerParams(
            dimension_semantics=("parallel", "parallel"),
            vmem_limit_bytes=VMEM_LIMIT),
        name="proj",
    )(x, x, x, mod3, norm_g, w_in_r, qlg, w_uq_r, kvlg, w_ukv_b, gq, gk, cs, w_pool_b, ps)


def _ctxkv_kernel(x_ref, mod_ref, ng_ref, w_kv_ref, kvlg_ref, w_ukv_ref, gk_ref, cs_ref,
                  k_ref, v_ref):
    d = x_ref.shape[-1]
    shift = mod_ref[0, :, 0:d]
    scale = mod_ref[0, :, d:2 * d]
    a = ng_ref[...] * (1.0 + scale)
    for bi in range(x_ref.shape[0]):
        xv = x_ref[bi]
        ms = jnp.mean(xv * xv, axis=-1, keepdims=True)
        hc = (xv * lax.rsqrt(ms + NORM_EPS) * a + shift).astype(BF16)
        u = jnp.dot(hc, w_kv_ref[...], preferred_element_type=F32)
        _kv_heads(u[:, 0:KV_LORA_RANK], u[:, KV_LORA_RANK:], kvlg_ref[...], w_ukv_ref[...],
                  gk_ref, cs_ref[...], k_ref.at[pl.ds(bi, 1)], v_ref.at[pl.ds(bi, 1)],
                  slice(None))


def _ctxkv(ctx, mod3, norm_g, w_kv, kvlg, w_ukv_b, gk, cs_ctx, *, bb):
    b, lc, d = ctx.shape
    const2 = lambda bi: (0, 0)
    nkv = KV_LORA_RANK + 2 * QK_ROPE_DIM
    return pl.pallas_call(
        _ctxkv_kernel,
        grid=(b // bb,),
        in_specs=[
            pl.BlockSpec((bb, lc, d), lambda bi: (bi, 0, 0)),
            pl.BlockSpec((1, 1, 3 * d), lambda bi: (b, 0, 0)),
            pl.BlockSpec((1, d), const2),
            pl.BlockSpec((d, nkv), const2),
            pl.BlockSpec((1, KV_LORA_RANK), const2),
            pl.BlockSpec((KV_LORA_RANK, MLA_HEADS * HEAD_COLS), const2),
            pl.BlockSpec((1, 3 * LANES), const2),
            pl.BlockSpec((lc, LANES), const2),
        ],
        out_specs=[
            pl.BlockSpec((bb, MLA_HEADS, lc, QK_AUG_DIM), lambda bi: (bi, 0, 0, 0)),
            pl.BlockSpec((bb, MLA_HEADS, lc, V_AUG_DIM), lambda bi: (bi, 0, 0, 0)),
        ],
        out_shape=[
            jax.ShapeDtypeStruct((b, MLA_HEADS, lc, QK_AUG_DIM), BF16),
            jax.ShapeDtypeStruct((b, MLA_HEADS, lc, V_AUG_DIM), BF16),
        ],
        compiler_params=pltpu.CompilerParams(
            dimension_semantics=("parallel",), vmem_limit_bytes=VMEM_LIMIT),
        name="ctxkv",
    )(ctx, mod3, norm_g, w_kv, kvlg, w_ukv_b, gk, cs_ctx)


def _lane_tile(a, n):
    return a if n == 1 else jnp.concatenate([a] * n, axis=1)


def _attn_out_kernel(q_ref, k_ref, v_ref, kc_ref, vc_ref, sga_ref, brp_ref, x_ref, mod_ref,
                     w_ref, o_ref, attn_scr, *stat_scr, tk, running_max):
    nb, ib, dk = q_ref.shape[2:]
    rows = nb * ib
    nk = k_ref.shape[2] // tk
    dv = V_HEAD_DIM
    d = x_ref.shape[-1]

    def keys(h, c):
        return kc_ref[0, h] if c == 0 else k_ref[0, h, (c - 1) * tk:c * tk, :]

    def values(h, c):
        return vc_ref[0, h] if c == 0 else v_ref[0, h, (c - 1) * tk:c * tk, :]

    def scores(h, c):
        q = q_ref[0, h].reshape(rows, dk)
        return lax.dot_general(q, keys(h, c), (((1,), (1,)), ((), ())),
                               preferred_element_type=F32)

    gate = mod_ref[0, :, 2 * d:3 * d]
    o_ref[0] = x_ref[0] + gate * jnp.dot(brp_ref[0], w_ref[D_ATTN:, :],
                                         preferred_element_type=F32)

    if running_max:
        m_scr, acc_scr = stat_scr

        def accumulate(h, s, vb):
            m_prev = m_scr[h]
            m_next = jnp.maximum(m_prev, jnp.max(s, axis=-1, keepdims=True))
            alpha = jnp.exp2(m_prev - m_next)
            p = jnp.exp2(s - _lane_tile(m_next, s.shape[1] // LANES))
            pv = jnp.dot(p.astype(BF16), vb, preferred_element_type=F32)
            acc_scr[h] = _lane_tile(alpha, V_AUG_DIM // LANES) * acc_scr[h] + pv
            m_scr[h] = m_next

        m_scr[...] = jnp.full(m_scr.shape, -jnp.inf, F32)
        acc_scr[...] = jnp.zeros(acc_scr.shape, F32)
        for c in range(nk + 1):
            for h in range(MLA_HEADS):
                accumulate(h, scores(h, c), values(h, c))
        for h in range(MLA_HEADS):
            attn_scr[h] = acc_scr[h, :, 0:dv] / acc_scr[h, :, dv:2 * dv]
    else:
        acc = [None] * MLA_HEADS
        for c in range(nk + 1):
            for h in range(MLA_HEADS):
                p = jnp.exp2(scores(h, c)).astype(BF16)
                pv = jnp.dot(p, values(h, c), preferred_element_type=F32)
                acc[h] = pv if acc[h] is None else acc[h] + pv
        for h in range(MLA_HEADS):
            attn_scr[h] = acc[h][:, 0:dv] / acc[h][:, dv:2 * dv]
    pair = 2 * dv
    for h0 in range(0, MLA_HEADS, 2):
        attn = jnp.concatenate(
            [jnp.concatenate([attn_scr[h, pl.ds(il, nb, stride=ib), :] for il in range(ib)],
                             axis=0) for h in (h0, h0 + 1)], axis=1)
        cols = slice(h0 * dv, h0 * dv + pair)
        br_a = (sga_ref[0, :, cols].astype(F32) * attn).astype(BF16)
        o_ref[0] = o_ref[0] + gate * jnp.dot(br_a, w_ref[cols, :], preferred_element_type=F32)


def _attn_out(q5, k, v, kc, vc, sga, brp, x, mod3, w_out_b, *, ib, tk, running_max):
    b, h, nb, _, dk = q5.shape
    _, s, d = x.shape
    lc = kc.shape[2]
    dva = v.shape[-1]
    rows = nb * ib
    kern = functools.partial(_attn_out_kernel, tk=tk, running_max=running_max)
    stat_scr = [pltpu.VMEM((h, rows, LANES), F32), pltpu.VMEM((h, rows, dva), F32)]
    resident = dict(pipeline_mode=pl.Buffered(1))
    row_tile = lambda bi, ti: (bi, ti, 0)
    return pl.pallas_call(
        kern,
        grid=(b, Q_BLOCK // ib),
        in_specs=[
            pl.BlockSpec((1, h, nb, ib, dk), lambda bi, ti: (bi, 0, 0, ti, 0)),
            pl.BlockSpec((1, h, s, dk), lambda bi, ti: (bi, 0, 0, 0)),
            pl.BlockSpec((1, h, s, dva), lambda bi, ti: (bi, 0, 0, 0)),
            pl.BlockSpec((1, h, lc, dk), lambda bi, ti: (bi, 0, 0, 0), **resident),
            pl.BlockSpec((1, h, lc, dva), lambda bi, ti: (bi, 0, 0, 0), **resident),
            pl.BlockSpec((1, rows, D_ATTN), row_tile),
            pl.BlockSpec((1, rows, D_POOL), row_tile),
            pl.BlockSpec((1, rows, d), row_tile),
            pl.BlockSpec((1, 1, 3 * d), lambda bi, ti: (bi, 0, 0)),
            pl.BlockSpec((D_ATTN + D_POOL, d), lambda bi, ti: (0, 0), **resident),
        ],
        out_specs=pl.BlockSpec((1, rows, d), row_tile),
        out_shape=jax.ShapeDtypeStruct((b, s, d), F32),
        scratch_shapes=[pltpu.VMEM((h, rows, V_HEAD_DIM), F32)] + (stat_scr if running_max else []),
        compiler_params=pltpu.CompilerParams(
            dimension_semantics=("parallel", "arbitrary"),
            vmem_limit_bytes=VMEM_LIMIT),
        name="attn_out_rmax" if running_max else "attn_out",
    )(q5, k, v, kc, vc, sga, brp, x, mod3, w_out_b)


def _rotate_half_cols(w, signed):
    q = QK_ROPE_DIM // 4
    a, b, c, d = (w[..., i * q:(i + 1) * q] for i in range(4))
    if signed:
        return jnp.concatenate([-b, a, -d, c], axis=-1)
    return jnp.concatenate([b, a, d, c], axis=-1)


def _rope_table(seq):
    rows = seq // GRID_W
    row = np.repeat(np.arange(rows, dtype=np.float32), GRID_W)
    col = np.tile(np.arange(GRID_W, dtype=np.float32), rows)
    n_freq = QK_ROPE_DIM // 4
    inv = np.float32(ROPE_BASE) ** (-np.arange(n_freq, dtype=np.float32) / np.float32(n_freq))
    ang_r = row[:, None] * inv
    ang_c = col[:, None] * inv
    ang = np.concatenate([ang_r, ang_r, ang_c, ang_c], axis=-1).astype(np.float64)
    table = np.concatenate([np.cos(ang), np.sin(ang)], axis=-1)
    return jnp.asarray(table, dtype=F32)


def kernel(x, c, ctx, c_ctx, w_mod, b_mod, norm_g, w_in, q_lora_g, w_uq, kv_lora_g, w_ukv,
           q_norm_g, k_norm_g, w_pool, pool_scale, w_out):
    b, s, d = x.shape
    lc = ctx.shape[1]
    depth = w_mod.shape[0]
    assert depth == 1, "single-layer block"
    assert b + 1 <= MOD_ROWS and s % PROJ_TM == 0 and s % ATTN_TK == 0
    assert s % Q_BLOCK == 0 and Q_BLOCK % ATTN_IB == 0 and s % GRID_W == 0
    l = 0

    cc = jnp.concatenate([c, c_ctx[None, :], jnp.zeros((MOD_ROWS - b - 1, d), F32)], axis=0)
    mod = _modulation(cc, w_mod, b_mod[l][None, :], l)
    mod3 = mod[:, None, :]

    wi = w_in.astype(BF16)[l]
    o_cq, o_ckv, o_kr = 0, Q_LORA_RANK, Q_LORA_RANK + KV_LORA_RANK
    o_ga = o_kr + QK_ROPE_DIM
    o_pin = o_ga + D_ATTN
    o_gp = o_pin + D_POOL
    w_kr = wi[:, o_kr:o_ga]
    w_in_r = jnp.concatenate(
        [wi[:, o_cq:o_kr], w_kr, _rotate_half_cols(w_kr, True), wi[:, o_ga:o_pin],
         wi[:, o_pin:o_gp], wi[:, o_gp:]], axis=1)
    wq = w_uq.astype(BF16)[l].reshape(Q_LORA_RANK, MLA_HEADS, QK_HEAD_DIM)
    wq_rope = wq[:, :, QK_NOPE_DIM:]
    w_uq_r = jnp.concatenate([wq, _rotate_half_cols(wq_rope, True)], axis=-1)
    w_uq_r = w_uq_r.reshape(Q_LORA_RANK, MLA_HEADS * HEAD_COLS)
    w_ukv_b = w_ukv.astype(BF16)[l]
    w_pool_b = w_pool.astype(BF16)[l]
    w_out_b = w_out.astype(BF16)[l]

    bound = (Q_SCALE * QK_HEAD_DIM) * jnp.max(jnp.abs(q_norm_g[l])) * jnp.max(jnp.abs(k_norm_g[l]))

    def gain_row(g, const, tail0):
        gains = jnp.concatenate([g, _rotate_half_cols(g[QK_NOPE_DIM:], False)]) * const
        tail = jnp.zeros((LANES,), F32).at[QK_ROPE_DIM].set(tail0)
        return jnp.concatenate([gains, tail])[None, :]

    gq = gain_row(q_norm_g[l], Q_SCALE * math.sqrt(QK_HEAD_DIM), -bound)
    gk = gain_row(k_norm_g[l], math.sqrt(QK_HEAD_DIM), 1.0)
    cs = _rope_table(s)
    cs_ctx = jnp.concatenate([jnp.ones((lc, QK_ROPE_DIM), F32),
                              jnp.zeros((lc, QK_ROPE_DIM), F32)], axis=1)

    ng = norm_g[l][None, :]
    qlg = q_lora_g[l][None, :]
    kvlg = kv_lora_g[l][None, :]
    ps = pool_scale[l][None, :]

    q, k, v, sga, brp = _proj(x, mod3, ng, w_in_r, qlg, w_uq_r, kvlg, w_ukv_b, gq, gk, cs,
                              w_pool_b, ps, tm=PROJ_TM, nsub=PROJ_NSUB)
    kc, vc = _ctxkv(ctx, mod3, ng, w_in_r[:, C_CKV:C_GA], kvlg, w_ukv_b, gk, cs_ctx,
                    bb=b // 2 if b % 2 == 0 else b)
    args = (q, k, v, kc, vc, sga, brp, x, mod3, w_out_b)
    attn_out = functools.partial(_attn_out, ib=ATTN_IB, tk=ATTN_TK)
    return lax.cond(2.0 * bound < MAX_SHIFT_RANGE,
                    lambda a: attn_out(*a, running_max=False),
                    lambda a: attn_out(*a, running_max=True), args)
```

```python
import functools
import math

import jax
import jax.numpy as jnp
import numpy as np
from jax import lax
from jax.experimental import pallas as pl
from jax.experimental.pallas import tpu as pltpu

F32 = jnp.float32
BF16 = jnp.bfloat16

GRID_W = 64
MLA_HEADS = 4
QK_NOPE_DIM = 128
QK_ROPE_DIM = 64
QK_HEAD_DIM = QK_NOPE_DIM + QK_ROPE_DIM
V_HEAD_DIM = 128
V_AUG_DIM = 2 * V_HEAD_DIM
QK_AUG_DIM = 256
MAX_SHIFT_RANGE = 110.0
Q_LORA_RANK = 256
KV_LORA_RANK = 128
ROPE_BASE = 10000.0
POOL_WINDOWS = (2, 4, 8, 16)
POOL_GROUP_DIM = 128
NORM_EPS = 1e-6
Q_BLOCK = 128
D_ATTN = MLA_HEADS * V_HEAD_DIM
D_POOL = len(POOL_WINDOWS) * POOL_GROUP_DIM

LANES = 128
HEAD_COLS = 2 * LANES
HALO = 16
EDGE = 8
assert max(POOL_WINDOWS) // 2 <= EDGE <= HALO
MOD_ROWS = 16

C_CQ = 0
C_CKV = C_CQ + Q_LORA_RANK
C_KR = C_CKV + KV_LORA_RANK
C_GA = C_KR + 2 * QK_ROPE_DIM
C_GP = C_GA + D_ATTN
C_PIN = C_GP + D_POOL
C_END = C_PIN + D_POOL

VMEM_LIMIT = 56 * 1024 * 1024
MOD_TN = 768
PROJ_TM = 1024
PROJ_NSUB = 4
ATTN_IB = 16
ATTN_TK = 512

Q_SCALE = QK_HEAD_DIM ** -0.5 * math.log2(math.e)


def _silu(x):
    hx = 0.5 * x
    return hx + hx * jnp.tanh(hx)


def _window_sum(p, w):
    n = p.shape[0]
    up = lambda a, k: pltpu.roll(a, n - k, axis=0)
    down = lambda a, k: pltpu.roll(a, k, axis=0)
    if w == 2:
        return p + down(p, 1)
    acc = p + up(p, 1)
    span = 2
    while span < w // 2:
        acc = acc + up(acc, span)
        span *= 2
    return acc + down(acc, w // 2)


def _rms_rows(x, g):
    ms = jnp.mean(x * x, axis=-1, keepdims=True)
    return x * lax.rsqrt(ms + NORM_EPS) * g


def _mod_kernel(cc_ref, w_ref, b_ref, o_ref):
    a = _silu(cc_ref[...]).astype(BF16)
    o_ref[...] = jnp.dot(a, w_ref[0].astype(BF16), preferred_element_type=F32) + b_ref[...]


def _modulation(cc, w_mod, b_mod, layer):
    _, d, n = w_mod.shape
    tn = MOD_TN
    return pl.pallas_call(
        _mod_kernel,
        grid=(n // tn,),
        in_specs=[
            pl.BlockSpec((MOD_ROWS, d), lambda j: (0, 0)),
            pl.BlockSpec((1, d, tn), lambda j: (layer, 0, j)),
            pl.BlockSpec((1, tn), lambda j: (0, j)),
        ],
        out_specs=pl.BlockSpec((MOD_ROWS, tn), lambda j: (0, j)),
        out_shape=jax.ShapeDtypeStruct((MOD_ROWS, n), F32),
        name="mod",
    )(cc, w_mod, b_mod)


HEAD_SS_EPS = QK_HEAD_DIM * NORM_EPS


def _rope_pair(rr, csg):
    pr = rr * csg
    return pr + pltpu.roll(pr, QK_ROPE_DIM, axis=1)


def _half_rope_sq(rr):
    return 0.5 * (rr * rr)


def _rope_lanes():
    return lax.broadcasted_iota(jnp.int32, (1, LANES), 1) < QK_ROPE_DIM


def _kv_heads(ckv, kr2, kvlg, w_ukv, gk_ref, cs, k_ref, v_ref, rows):
    ckv_n = _rms_rows(ckv, kvlg).astype(BF16)
    kv = jnp.dot(ckv_n, w_ukv, preferred_element_type=F32)
    g_nope = gk_ref[:, 0:LANES]
    tail = gk_ref[:, 2 * LANES:3 * LANES]
    rope_lanes = _rope_lanes()
    k_rope = _rope_pair(kr2, cs * gk_ref[:, LANES:2 * LANES])
    sq_rope = _half_rope_sq(kr2)
    for h in range(MLA_HEADS):
        kn = kv[:, h * HEAD_COLS:h * HEAD_COLS + LANES]
        vv = kv[:, h * HEAD_COLS + LANES:(h + 1) * HEAD_COLS]
        ss = jnp.sum(kn * kn + sq_rope, axis=-1, keepdims=True)
        r = lax.rsqrt(ss + HEAD_SS_EPS)
        k_ref[0, h, rows, 0:LANES] = (kn * r * g_nope).astype(BF16)
        k_ref[0, h, rows, LANES:] = jnp.where(rope_lanes, k_rope * r, tail).astype(BF16)
        v_ref[0, h, rows, 0:V_HEAD_DIM] = vv.astype(BF16)
        v_ref[0, h, rows, V_HEAD_DIM:] = jnp.ones((vv.shape[0], V_AUG_DIM - V_HEAD_DIM), BF16)


def _proj_kernel(x_ref, xp_ref, xn_ref, mod_ref, ng_ref, w_in_ref, qlg_ref, w_uq_ref,
                 kvlg_ref, w_ukv_ref, gq_ref, gk_ref, cs_ref, w_pool_ref, ps_ref,
                 q_ref, k_ref, v_ref, sga_ref, brp_ref,
                 h_scr, *, tm, seq, nsub):
    t = pl.program_id(1)
    nt = pl.num_programs(1)
    d = x_ref.shape[-1]
    shift = mod_ref[0, :, 0:d]
    scale = mod_ref[0, :, d:2 * d]
    a = ng_ref[...] * (1.0 + scale)

    def hnorm(xv):
        ms = jnp.mean(xv * xv, axis=-1, keepdims=True)
        return xv * lax.rsqrt(ms + NORM_EPS) * a + shift

    nh = tm // nsub
    h_scr[0:HALO, :] = jnp.where(t > 0, hnorm(xp_ref[0]), 0.0).astype(BF16)
    for i in range(nsub):
        h_scr[HALO + i * nh:HALO + (i + 1) * nh, :] = hnorm(x_ref[0, i * nh:(i + 1) * nh, :]).astype(BF16)
    h_scr[HALO + tm:, :] = jnp.where(t < nt - 1, hnorm(xn_ref[0]), 0.0).astype(BF16)

    cs_all = cs_ref[...]
    gq_nope = gq_ref[:, 0:LANES]
    gq_rr = gq_ref[:, LANES:2 * LANES]
    gq_tail = gq_ref[:, 2 * LANES:3 * LANES]
    rope_lanes = _rope_lanes()
    assert (tm // nsub) % Q_BLOCK == 0

    def h_rows(i):
        return h_scr[HALO + i * nh:HALO + (i + 1) * nh, :]

    u_gate = []
    for i in range(nsub):
        rows = slice(i * nh, (i + 1) * nh)
        cs = cs_all[rows]
        u = jnp.dot(h_rows(i), w_in_ref[:, C_CQ:C_GP], preferred_element_type=F32)
        u_a = u[:, C_CQ:C_GA]
        u_gate.append(u[:, C_GA:C_GP])
        _kv_heads(u_a[:, C_CKV:C_KR], u_a[:, C_KR:C_GA], kvlg_ref[...], w_ukv_ref[...],
                  gk_ref, cs, k_ref, v_ref, rows)
        cq_n = _rms_rows(u_a[:, C_CQ:C_CKV], qlg_ref[...]).astype(BF16)
        qa = jnp.dot(cq_n, w_uq_ref[...], preferred_element_type=F32)
        csq = cs * gq_rr
        for h in range(MLA_HEADS):
            qn = qa[:, h * HEAD_COLS:h * HEAD_COLS + LANES]
            rr = qa[:, h * HEAD_COLS + LANES:(h + 1) * HEAD_COLS]
            ss = jnp.sum(qn * qn + _half_rope_sq(rr), axis=-1, keepdims=True)
            r = lax.rsqrt(ss + HEAD_SS_EPS)
            q_lo = (qn * r * gq_nope).astype(BF16)
            q_hi = jnp.where(rope_lanes, _rope_pair(rr, csq) * r, gq_tail).astype(BF16)
            for j in range(nh // Q_BLOCK):
                blk = i * (nh // Q_BLOCK) + j
                q_ref[0, h, blk, :, 0:LANES] = q_lo[j * Q_BLOCK:(j + 1) * Q_BLOCK]
                q_ref[0, h, blk, :, LANES:] = q_hi[j * Q_BLOCK:(j + 1) * Q_BLOCK]

    for i in range(nsub):
        sga_ref[0, i * nh:(i + 1) * nh, :] = _silu(u_gate[i]).astype(BF16)

    for i in range(nsub):
        rows = slice(i * nh, (i + 1) * nh)
        pin = jnp.dot(h_scr[i * nh:(i + 1) * nh + 2 * HALO, :], w_in_ref[:, C_PIN:C_END],
                      preferred_element_type=F32)
        u_gp = jnp.dot(h_rows(i), w_in_ref[:, C_GP:C_PIN], preferred_element_type=F32)
        tok_head = t * tm + i * nh + lax.broadcasted_iota(jnp.int32, (EDGE, LANES), 0)
        tok_tail = tok_head + (nh - EDGE)
        for g, w in enumerate(POOL_WINDOWS):
            c0 = g * POOL_GROUP_DIM
            c1 = c0 + POOL_GROUP_DIM
            pg = pin[:, c0:c1]
            ws = _window_sum(pg, w)[HALO:HALO + nh]

            def inv_count(tok):
                lo = jnp.maximum(tok - w // 2, 0)
                hi = jnp.minimum(tok - w // 2 + w, seq)
                return 1.0 / (hi - lo).astype(F32)

            inv_cnt = jnp.concatenate(
                [inv_count(tok_head), jnp.full((nh - 2 * EDGE, LANES), 1.0 / w, F32),
                 inv_count(tok_tail)], axis=0)
            pooled = ws * inv_cnt - pg[HALO:HALO + nh]
            y = jnp.dot(pooled.astype(BF16), w_pool_ref[g], preferred_element_type=F32)
            brp_ref[0, rows, c0:c1] = (_silu(u_gp[:, c0:c1]) * (y * ps_ref[:, c0:c1])).astype(BF16)


def _proj(x, mod3, norm_g, w_in_r, qlg, w_uq_r, kvlg, w_ukv_b, gq, gk, cs, w_pool_b, ps, *,
          tm, nsub):
    b, s, d = x.shape
    nt = s // tm
    hb = tm // HALO
    nhb = s // HALO
    const2 = lambda bi, ti: (0, 0)
    kern = functools.partial(_proj_kernel, tm=tm, seq=s, nsub=nsub)
    return pl.pallas_call(
        kern,
        grid=(b, nt),
        in_specs=[
            pl.BlockSpec((1, tm, d), lambda bi, ti: (bi, ti, 0)),
            pl.BlockSpec((1, HALO, d), lambda bi, ti: (bi, jnp.maximum(ti * hb - 1, 0), 0)),
            pl.BlockSpec((1, HALO, d), lambda bi, ti: (bi, jnp.minimum((ti + 1) * hb, nhb - 1), 0)),
            pl.BlockSpec((1, 1, 3 * d), lambda bi, ti: (bi, 0, 0)),
            pl.BlockSpec((1, d), const2),
            pl.BlockSpec((d, C_END), const2),
            pl.BlockSpec((1, Q_LORA_RANK), const2),
            pl.BlockSpec((Q_LORA_RANK, MLA_HEADS * HEAD_COLS), const2),
            pl.BlockSpec((1, KV_LORA_RANK), const2),
            pl.BlockSpec((KV_LORA_RANK, MLA_HEADS * HEAD_COLS), const2),
            pl.BlockSpec((1, 3 * LANES), const2),
            pl.BlockSpec((1, 3 * LANES), const2),
            pl.BlockSpec((tm, LANES), lambda bi, ti: (ti, 0)),
            pl.BlockSpec((len(POOL_WINDOWS), POOL_GROUP_DIM, POOL_GROUP_DIM), lambda bi, ti: (0, 0, 0)),
            pl.BlockSpec((1, D_POOL), const2),
        ],
        out_specs=[
            pl.BlockSpec((1, MLA_HEADS, tm // Q_BLOCK, Q_BLOCK, QK_AUG_DIM),
                         lambda bi, ti: (bi, 0, ti, 0, 0)),
            pl.BlockSpec((1, MLA_HEADS, tm, QK_AUG_DIM), lambda bi, ti: (bi, 0, ti, 0)),
            pl.BlockSpec((1, MLA_HEADS, tm, V_AUG_DIM), lambda bi, ti: (bi, 0, ti, 0)),
            pl.BlockSpec((1, tm, D_ATTN), lambda bi, ti: (bi, ti, 0)),
            pl.BlockSpec((1, tm, D_POOL), lambda bi, ti: (bi, ti, 0)),
        ],
        out_shape=[
            jax.ShapeDtypeStruct((b, MLA_HEADS, s // Q_BLOCK, Q_BLOCK, QK_AUG_DIM), BF16),
            jax.ShapeDtypeStruct((b, MLA_HEADS, s, QK_AUG_DIM), BF16),
            jax.ShapeDtypeStruct((b, MLA_HEADS, s, V_AUG_DIM), BF16),
            jax.ShapeDtypeStruct((b, s, D_ATTN), BF16),
            jax.ShapeDtypeStruct((b, s, D_POOL), BF16),
        ],
        scratch_shapes=[
            pltpu.VMEM((tm + 2 * HALO, d), BF16),
        ],
        compiler_params=pltpu.CompilerParams(
            dimension_semantics=("parallel", "parallel"),
            vmem_limit_bytes=VMEM_LIMIT),
        name="proj",
    )(x, x, x, mod3, norm_g, w_in_r, qlg, w_uq_r, kvlg, w_ukv_b, gq, gk, cs, w_pool_b, ps)


def _ctxkv_kernel(x_ref, mod_ref, ng_ref, w_kv_ref, kvlg_ref, w_ukv_ref, gk_ref, cs_ref,
                  k_ref, v_ref):
    d = x_ref.shape[-1]
    shift = mod_ref[0, :, 0:d]
    scale = mod_ref[0, :, d:2 * d]
    a = ng_ref[...] * (1.0 + scale)
    for bi in range(x_ref.shape[0]):
        xv = x_ref[bi]
        ms = jnp.mean(xv * xv, axis=-1, keepdims=True)
        hc = (xv * lax.rsqrt(ms + NORM_EPS) * a + shift).astype(BF16)
        u = jnp.dot(hc, w_kv_ref[...], preferred_element_type=F32)
        _kv_heads(u[:, 0:KV_LORA_RANK], u[:, KV_LORA_RANK:], kvlg_ref[...], w_ukv_ref[...],
                  gk_ref, cs_ref[...], k_ref.at[pl.ds(bi, 1)], v_ref.at[pl.ds(bi, 1)],
                  slice(None))


def _ctxkv(ctx, mod3, norm_g, w_kv, kvlg, w_ukv_b, gk, cs_ctx, *, bb):
    b, lc, d = ctx.shape
    const2 = lambda bi: (0, 0)
    nkv = KV_LORA_RANK + 2 * QK_ROPE_DIM
    return pl.pallas_call(
        _ctxkv_kernel,
        grid=(b // bb,),
        in_specs=[
            pl.BlockSpec((bb, lc, d), lambda bi: (bi, 0, 0)),
            pl.BlockSpec((1, 1, 3 * d), lambda bi: (b, 0, 0)),
            pl.BlockSpec((1, d), const2),
            pl.BlockSpec((d, nkv), const2),
            pl.BlockSpec((1, KV_LORA_RANK), const2),
            pl.BlockSpec((KV_LORA_RANK, MLA_HEADS * HEAD_COLS), const2),
            pl.BlockSpec((1, 3 * LANES), const2),
            pl.BlockSpec((lc, LANES), const2),
        ],
        out_specs=[
            pl.BlockSpec((bb, MLA_HEADS, lc, QK_AUG_DIM), lambda bi: (bi, 0, 0, 0)),
            pl.BlockSpec((bb, MLA_HEADS, lc, V_AUG_DIM), lambda bi: (bi, 0, 0, 0)),
        ],
        out_shape=[
            jax.ShapeDtypeStruct((b, MLA_HEADS, lc, QK_AUG_DIM), BF16),
            jax.ShapeDtypeStruct((b, MLA_HEADS, lc, V_AUG_DIM), BF16),
        ],
        compiler_params=pltpu.CompilerParams(
            dimension_semantics=("parallel",), vmem_limit_bytes=VMEM_LIMIT),
        name="ctxkv",
    )(ctx, mod3, norm_g, w_kv, kvlg, w_ukv_b, gk, cs_ctx)


def _lane_tile(a, n):
    return a if n == 1 else jnp.concatenate([a] * n, axis=1)


def _attn_out_kernel(q_ref, k_ref, v_ref, kc_ref, vc_ref, sga_ref, brp_ref, x_ref, mod_ref,
                     w_ref, o_ref, attn_scr, *stat_scr, tk, running_max):
    nb, ib, dk = q_ref.shape[2:]
    rows = nb * ib
    nk = k_ref.shape[2] // tk
    dv = V_HEAD_DIM
    d = x_ref.shape[-1]

    def keys(h, c):
        return kc_ref[0, h] if c == 0 else k_ref[0, h, (c - 1) * tk:c * tk, :]

    def values(h, c):
        return vc_ref[0, h] if c == 0 else v_ref[0, h, (c - 1) * tk:c * tk, :]

    def scores(h, c):
        q = q_ref[0, h].reshape(rows, dk)
        return lax.dot_general(q, keys(h, c), (((1,), (1,)), ((), ())),
                               preferred_element_type=F32)

    gate = mod_ref[0, :, 2 * d:3 * d]
    o_ref[0] = x_ref[0] + gate * jnp.dot(brp_ref[0], w_ref[D_ATTN:, :],
                                         preferred_element_type=F32)

    if running_max:
        m_scr, acc_scr = stat_scr

        def accumulate(h, s, vb):
            m_prev = m_scr[h]
            m_next = jnp.maximum(m_prev, jnp.max(s, axis=-1, keepdims=True))
            alpha = jnp.exp2(m_prev - m_next)
            p = jnp.exp2(s - _lane_tile(m_next, s.shape[1] // LANES))
            pv = jnp.dot(p.astype(BF16), vb, preferred_element_type=F32)
            acc_scr[h] = _lane_tile(alpha, V_AUG_DIM // LANES) * acc_scr[h] + pv
            m_scr[h] = m_next

        m_scr[...] = jnp.full(m_scr.shape, -jnp.inf, F32)
        acc_scr[...] = jnp.zeros(acc_scr.shape, F32)
        for c in range(nk + 1):
            for h in range(MLA_HEADS):
                accumulate(h, scores(h, c), values(h, c))
        for h in range(MLA_HEADS):
            attn_scr[h] = acc_scr[h, :, 0:dv] / acc_scr[h, :, dv:2 * dv]
    else:
        acc = [None] * MLA_HEADS
        for c in range(nk + 1):
            for h in range(MLA_HEADS):
                p = jnp.exp2(scores(h, c)).astype(BF16)
                pv = jnp.dot(p, values(h, c), preferred_element_type=F32)
                acc[h] = pv if acc[h] is None else acc[h] + pv
        for h in range(MLA_HEADS):
            attn_scr[h] = acc[h][:, 0:dv] / acc[h][:, dv:2 * dv]
    pair = 2 * dv
    for h0 in range(0, MLA_HEADS, 2):
        attn = jnp.concatenate(
            [jnp.concatenate([attn_scr[h, pl.ds(il, nb, stride=ib), :] for il in range(ib)],
                             axis=0) for h in (h0, h0 + 1)], axis=1)
        cols = slice(h0 * dv, h0 * dv + pair)
        br_a = (sga_ref[0, :, cols].astype(F32) * attn).astype(BF16)
        o_ref[0] = o_ref[0] + gate * jnp.dot(br_a, w_ref[cols, :], preferred_element_type=F32)


def _attn_out(q5, k, v, kc, vc, sga, brp, x, mod3, w_out_b, *, ib, tk, running_max):
    b, h, nb, _, dk = q5.shape
    _, s, d = x.shape
    lc = kc.shape[2]
    dva = v.shape[-1]
    rows = nb * ib
    kern = functools.partial(_attn_out_kernel, tk=tk, running_max=running_max)
    stat_scr = [pltpu.VMEM((h, rows, LANES), F32), pltpu.VMEM((h, rows, dva), F32)]
    resident = dict(pipeline_mode=pl.Buffered(1))
    row_tile = lambda bi, ti: (bi, ti, 0)
    return pl.pallas_call(
        kern,
        grid=(b, Q_BLOCK // ib),
        in_specs=[
            pl.BlockSpec((1, h, nb, ib, dk), lambda bi, ti: (bi, 0, 0, ti, 0)),
            pl.BlockSpec((1, h, s, dk), lambda bi, ti: (bi, 0, 0, 0)),
            pl.BlockSpec((1, h, s, dva), lambda bi, ti: (bi, 0, 0, 0)),
            pl.BlockSpec((1, h, lc, dk), lambda bi, ti: (bi, 0, 0, 0), **resident),
            pl.BlockSpec((1, h, lc, dva), lambda bi, ti: (bi, 0, 0, 0), **resident),
            pl.BlockSpec((1, rows, D_ATTN), row_tile),
            pl.BlockSpec((1, rows, D_POOL), row_tile),
            pl.BlockSpec((1, rows, d), row_tile),
            pl.BlockSpec((1, 1, 3 * d), lambda bi, ti: (bi, 0, 0)),
            pl.BlockSpec((D_ATTN + D_POOL, d), lambda bi, ti: (0, 0), **resident),
        ],
        out_specs=pl.BlockSpec((1, rows, d), row_tile),
        out_shape=jax.ShapeDtypeStruct((b, s, d), F32),
        scratch_shapes=[pltpu.VMEM((h, rows, V_HEAD_DIM), F32)] + (stat_scr if running_max else []),
        compiler_params=pltpu.CompilerParams(
            dimension_semantics=("parallel", "arbitrary"),
            vmem_limit_bytes=VMEM_LIMIT),
        name="attn_out_rmax" if running_max else "attn_out",
    )(q5, k, v, kc, vc, sga, brp, x, mod3, w_out_b)


def _rotate_half_cols(w, signed):
    q = QK_ROPE_DIM // 4
    a, b, c, d = (w[..., i * q:(i + 1) * q] for i in range(4))
    if signed:
        return jnp.concatenate([-b, a, -d, c], axis=-1)
    return jnp.concatenate([b, a, d, c], axis=-1)


def _rope_table(seq):
    rows = seq // GRID_W
    row = np.repeat(np.arange(rows, dtype=np.float32), GRID_W)
    col = np.tile(np.arange(GRID_W, dtype=np.float32), rows)
    n_freq = QK_ROPE_DIM // 4
    inv = np.float32(ROPE_BASE) ** (-np.arange(n_freq, dtype=np.float32) / np.float32(n_freq))
    ang_r = row[:, None] * inv
    ang_c = col[:, None] * inv
    ang = np.concatenate([ang_r, ang_r, ang_c, ang_c], axis=-1).astype(np.float64)
    table = np.concatenate([np.cos(ang), np.sin(ang)], axis=-1)
    return jnp.asarray(table, dtype=F32)


def kernel(x, c, ctx, c_ctx, w_mod, b_mod, norm_g, w_in, q_lora_g, w_uq, kv_lora_g, w_ukv,
           q_norm_g, k_norm_g, w_pool, pool_scale, w_out):
    b, s, d = x.shape
    lc = ctx.shape[1]
    depth = w_mod.shape[0]
    assert depth == 1, "single-layer block"
    assert b + 1 <= MOD_ROWS and s % PROJ_TM == 0 and s % ATTN_TK == 0
    assert s % Q_BLOCK == 0 and Q_BLOCK % ATTN_IB == 0 and s % GRID_W == 0
    l = 0

    cc = jnp.concatenate([c, c_ctx[None, :], jnp.zeros((MOD_ROWS - b - 1, d), F32)], axis=0)
    mod = _modulation(cc, w_mod, b_mod[l][None, :], l)
    mod3 = mod[:, None, :]

    wi = w_in.astype(BF16)[l]
    o_cq, o_ckv, o_kr = 0, Q_LORA_RANK, Q_LORA_RANK + KV_LORA_RANK
    o_ga = o_kr + QK_ROPE_DIM
    o_pin = o_ga + D_ATTN
    o_gp = o_pin + D_POOL
    w_kr = wi[:, o_kr:o_ga]
    w_in_r = jnp.concatenate(
        [wi[:, o_cq:o_kr], w_kr, _rotate_half_cols(w_kr, True), wi[:, o_ga:o_pin],
         wi[:, o_gp:], wi[:, o_pin:o_gp]], axis=1)
    wq = w_uq.astype(BF16)[l].reshape(Q_LORA_RANK, MLA_HEADS, QK_HEAD_DIM)
    wq_rope = wq[:, :, QK_NOPE_DIM:]
    w_uq_r = jnp.concatenate([wq, _rotate_half_cols(wq_rope, True)], axis=-1)
    w_uq_r = w_uq_r.reshape(Q_LORA_RANK, MLA_HEADS * HEAD_COLS)
    w_ukv_b = w_ukv.astype(BF16)[l]
    w_pool_b = w_pool.astype(BF16)[l]
    w_out_b = w_out.astype(BF16)[l]

    bound = (Q_SCALE * QK_HEAD_DIM) * jnp.max(jnp.abs(q_norm_g[l])) * jnp.max(jnp.abs(k_norm_g[l]))

    def gain_row(g, const, tail0):
        gains = jnp.concatenate([g, _rotate_half_cols(g[QK_NOPE_DIM:], False)]) * const
        tail = jnp.zeros((LANES,), F32).at[QK_ROPE_DIM].set(tail0)
        return jnp.concatenate([gains, tail])[None, :]

    gq = gain_row(q_norm_g[l], Q_SCALE * math.sqrt(QK_HEAD_DIM), -bound)
    gk = gain_row(k_norm_g[l], math.sqrt(QK_HEAD_DIM), 1.0)
    cs = _rope_table(s)
    cs_ctx = jnp.concatenate([jnp.ones((lc, QK_ROPE_DIM), F32),
                              jnp.zeros((lc, QK_ROPE_DIM), F32)], axis=1)

    ng = norm_g[l][None, :]
    qlg = q_lora_g[l][None, :]
    kvlg = kv_lora_g[l][None, :]
    ps = pool_scale[l][None, :]

    q, k, v, sga, brp = _proj(x, mod3, ng, w_in_r, qlg, w_uq_r, kvlg, w_ukv_b, gq, gk, cs,
                              w_pool_b, ps, tm=PROJ_TM, nsub=PROJ_NSUB)
    kc, vc = _ctxkv(ctx, mod3, ng, w_in_r[:, C_CKV:C_GA], kvlg, w_ukv_b, gk, cs_ctx,
                    bb=b // 2 if b % 2 == 0 else b)
    args = (q, k, v, kc, vc, sga, brp, x, mod3, w_out_b)
    attn_out = functools.partial(_attn_out, ib=ATTN_IB, tk=ATTN_TK)
    return lax.cond(2.0 * bound < MAX_SHIFT_RANGE,
                    lambda a: attn_out(*a, running_max=False),
                    lambda a: attn_out(*a, running_max=True), args)
```

```python
import functools
import math

import jax
import jax.numpy as jnp
import numpy as np
from jax import lax
from jax.experimental import pallas as pl
from jax.experimental.pallas import tpu as pltpu

F32 = jnp.float32
BF16 = jnp.bfloat16

GRID_W = 64
MLA_HEADS = 4
QK_NOPE_DIM = 128
QK_ROPE_DIM = 64
QK_HEAD_DIM = QK_NOPE_DIM + QK_ROPE_DIM
V_HEAD_DIM = 128
V_AUG_DIM = 2 * V_HEAD_DIM
MAX_SHIFT_RANGE = 110.0
Q_LORA_RANK = 256
KV_LORA_RANK = 128
ROPE_BASE = 10000.0
POOL_WINDOWS = (2, 4, 8, 16)
POOL_GROUP_DIM = 128
NORM_EPS = 1e-6
Q_BLOCK = 128
D_ATTN = MLA_HEADS * V_HEAD_DIM
D_POOL = len(POOL_WINDOWS) * POOL_GROUP_DIM

LANES = 128
HEAD_COLS = 2 * LANES
HALO = 16
EDGE = 8
assert max(POOL_WINDOWS) // 2 <= EDGE <= HALO
MOD_ROWS = 16

C_CQ = 0
C_CKV = C_CQ + Q_LORA_RANK
C_KR = C_CKV + KV_LORA_RANK
C_GA = C_KR + 2 * QK_ROPE_DIM
C_GP = C_GA + D_ATTN
C_PIN = C_GP + D_POOL
C_END = C_PIN + D_POOL

VMEM_LIMIT = 56 * 1024 * 1024
MOD_TN = 768
PROJ_TM = 1024
PROJ_NSUB = 4
ATTN_IB = 16
ATTN_TK = 512

Q_SCALE = QK_HEAD_DIM ** -0.5 * math.log2(math.e)


def _silu(x):
    hx = 0.5 * x
    return hx + hx * jnp.tanh(hx)


def _window_sum(p, w):
    n = p.shape[0]
    up = lambda a, k: pltpu.roll(a, n - k, axis=0)
    down = lambda a, k: pltpu.roll(a, k, axis=0)
    if w == 2:
        return p + down(p, 1)
    acc = p + up(p, 1)
    span = 2
    while span < w // 2:
        acc = acc + up(acc, span)
        span *= 2
    return acc + down(acc, w // 2)


def _rms_rows(x, g):
    ms = jnp.mean(x * x, axis=-1, keepdims=True)
    return x * lax.rsqrt(ms + NORM_EPS) * g


def _mod_kernel(cc_ref, w_ref, b_ref, o_ref):
    a = _silu(cc_ref[...]).astype(BF16)
    o_ref[...] = jnp.dot(a, w_ref[0].astype(BF16), preferred_element_type=F32) + b_ref[...]


def _modulation(cc, w_mod, b_mod, layer):
    _, d, n = w_mod.shape
    tn = MOD_TN
    return pl.pallas_call(
        _mod_kernel,
        grid=(n // tn,),
        in_specs=[
            pl.BlockSpec((MOD_ROWS, d), lambda j: (0, 0)),
            pl.BlockSpec((1, d, tn), lambda j: (layer, 0, j)),
            pl.BlockSpec((1, tn), lambda j: (0, j)),
        ],
        out_specs=pl.BlockSpec((MOD_ROWS, tn), lambda j: (0, j)),
        out_shape=jax.ShapeDtypeStruct((MOD_ROWS, n), F32),
        name="mod",
    )(cc, w_mod, b_mod)


HEAD_SS_EPS = QK_HEAD_DIM * NORM_EPS


def _rope_pair(rr, csg):
    pr = rr * csg
    return pr + pltpu.roll(pr, QK_ROPE_DIM, axis=1)


def _half_rope_sq(rr):
    return 0.5 * (rr * rr)


def _kv_heads(ckv, kr2, kvlg, w_ukv, gk_ref, cs, k_ref, v_ref, rows):
    ckv_n = _rms_rows(ckv, kvlg).astype(BF16)
    kv = jnp.dot(ckv_n, w_ukv, preferred_element_type=F32)
    g_nope = gk_ref[:, 0:LANES]
    k_rope = _rope_pair(kr2, cs * gk_ref[:, LANES:2 * LANES])
    sq_rope = _half_rope_sq(kr2)
    for h in range(MLA_HEADS):
        kn = kv[:, h * HEAD_COLS:h * HEAD_COLS + LANES]
        vv = kv[:, h * HEAD_COLS + LANES:(h + 1) * HEAD_COLS]
        ss = jnp.sum(kn * kn + sq_rope, axis=-1, keepdims=True)
        r = lax.rsqrt(ss + HEAD_SS_EPS)
        k_ref[0, h, rows, 0:LANES] = (kn * r * g_nope).astype(BF16)
        k_ref[0, h, rows, LANES:] = (k_rope * r)[:, 0:QK_ROPE_DIM].astype(BF16)
        v_ref[0, h, rows, 0:V_HEAD_DIM] = vv.astype(BF16)
        v_ref[0, h, rows, V_HEAD_DIM:] = jnp.ones((vv.shape[0], V_AUG_DIM - V_HEAD_DIM), BF16)


def _proj_kernel(x_ref, xp_ref, xn_ref, mod_ref, ng_ref, w_in_ref, qlg_ref, w_uq_ref,
                 kvlg_ref, w_ukv_ref, gq_ref, gk_ref, cs_ref, w_pool_ref, ps_ref,
                 q_ref, k_ref, v_ref, sga_ref, brp_ref,
                 h_scr, *, tm, seq, nsub):
    t = pl.program_id(1)
    nt = pl.num_programs(1)
    d = x_ref.shape[-1]
    shift = mod_ref[0, :, 0:d]
    scale = mod_ref[0, :, d:2 * d]
    a = ng_ref[...] * (1.0 + scale)

    def hnorm(xv):
        ms = jnp.mean(xv * xv, axis=-1, keepdims=True)
        return xv * lax.rsqrt(ms + NORM_EPS) * a + shift

    nh = tm // nsub
    h_scr[0:HALO, :] = jnp.where(t > 0, hnorm(xp_ref[0]), 0.0).astype(BF16)
    for i in range(nsub):
        h_scr[HALO + i * nh:HALO + (i + 1) * nh, :] = hnorm(x_ref[0, i * nh:(i + 1) * nh, :]).astype(BF16)
    h_scr[HALO + tm:, :] = jnp.where(t < nt - 1, hnorm(xn_ref[0]), 0.0).astype(BF16)

    cs_all = cs_ref[...]
    gq_nope = gq_ref[:, 0:LANES]
    gq_rr = gq_ref[:, LANES:2 * LANES]
    assert (tm // nsub) % Q_BLOCK == 0

    def h_rows(i):
        return h_scr[HALO + i * nh:HALO + (i + 1) * nh, :]

    u_gate = []
    for i in range(nsub):
        rows = slice(i * nh, (i + 1) * nh)
        cs = cs_all[rows]
        u = jnp.dot(h_rows(i), w_in_ref[:, C_CQ:C_GP], preferred_element_type=F32)
        u_a = u[:, C_CQ:C_GA]
        u_gate.append(u[:, C_GA:C_GP])
        _kv_heads(u_a[:, C_CKV:C_KR], u_a[:, C_KR:C_GA], kvlg_ref[...], w_ukv_ref[...],
                  gk_ref, cs, k_ref, v_ref, rows)
        cq_n = _rms_rows(u_a[:, C_CQ:C_CKV], qlg_ref[...]).astype(BF16)
        qa = jnp.dot(cq_n, w_uq_ref[...], preferred_element_type=F32)
        csq = cs * gq_rr
        for h in range(MLA_HEADS):
            qn = qa[:, h * HEAD_COLS:h * HEAD_COLS + LANES]
            rr = qa[:, h * HEAD_COLS + LANES:(h + 1) * HEAD_COLS]
            ss = jnp.sum(qn * qn + _half_rope_sq(rr), axis=-1, keepdims=True)
            r = lax.rsqrt(ss + HEAD_SS_EPS)
            q_lo = (qn * r * gq_nope).astype(BF16)
            q_hi = (_rope_pair(rr, csq) * r)[:, 0:QK_ROPE_DIM].astype(BF16)
            for j in range(nh // Q_BLOCK):
                blk = i * (nh // Q_BLOCK) + j
                q_ref[0, h, blk, :, 0:LANES] = q_lo[j * Q_BLOCK:(j + 1) * Q_BLOCK]
                q_ref[0, h, blk, :, LANES:] = q_hi[j * Q_BLOCK:(j + 1) * Q_BLOCK]

    for i in range(nsub):
        sga_ref[0, i * nh:(i + 1) * nh, :] = _silu(u_gate[i]).astype(BF16)

    for i in range(nsub):
        rows = slice(i * nh, (i + 1) * nh)
        pin = jnp.dot(h_scr[i * nh:(i + 1) * nh + 2 * HALO, :], w_in_ref[:, C_PIN:C_END],
                      preferred_element_type=F32)
        u_gp = jnp.dot(h_rows(i), w_in_ref[:, C_GP:C_PIN], preferred_element_type=F32)
        tok_head = t * tm + i * nh + lax.broadcasted_iota(jnp.int32, (EDGE, LANES), 0)
        tok_tail = tok_head + (nh - EDGE)
        for g, w in enumerate(POOL_WINDOWS):
            c0 = g * POOL_GROUP_DIM
            c1 = c0 + POOL_GROUP_DIM
            pg = pin[:, c0:c1]
            ws = _window_sum(pg, w)[HALO:HALO + nh]

            def inv_count(tok):
                lo = jnp.maximum(tok - w // 2, 0)
                hi = jnp.minimum(tok - w // 2 + w, seq)
                return 1.0 / (hi - lo).astype(F32)

            inv_cnt = jnp.concatenate(
                [inv_count(tok_head), jnp.full((nh - 2 * EDGE, LANES), 1.0 / w, F32),
                 inv_count(tok_tail)], axis=0)
            pooled = ws * inv_cnt - pg[HALO:HALO + nh]
            y = jnp.dot(pooled.astype(BF16), w_pool_ref[g], preferred_element_type=F32)
            brp_ref[0, rows, c0:c1] = (_silu(u_gp[:, c0:c1]) * (y * ps_ref[:, c0:c1])).astype(BF16)


def _proj(x, mod3, norm_g, w_in_r, qlg, w_uq_r, kvlg, w_ukv_b, gq, gk, cs, w_pool_b, ps, *,
          tm, nsub):
    b, s, d = x.shape
    nt = s // tm
    hb = tm // HALO
    nhb = s // HALO
    const2 = lambda bi, ti: (0, 0)
    kern = functools.partial(_proj_kernel, tm=tm, seq=s, nsub=nsub)
    return pl.pallas_call(
        kern,
        grid=(b, nt),
        in_specs=[
            pl.BlockSpec((1, tm, d), lambda bi, ti: (bi, ti, 0)),
            pl.BlockSpec((1, HALO, d), lambda bi, ti: (bi, jnp.maximum(ti * hb - 1, 0), 0)),
            pl.BlockSpec((1, HALO, d), lambda bi, ti: (bi, jnp.minimum((ti + 1) * hb, nhb - 1), 0)),
            pl.BlockSpec((1, 1, 3 * d), lambda bi, ti: (bi, 0, 0)),
            pl.BlockSpec((1, d), const2),
            pl.BlockSpec((d, C_END), const2),
            pl.BlockSpec((1, Q_LORA_RANK), const2),
            pl.BlockSpec((Q_LORA_RANK, MLA_HEADS * HEAD_COLS), const2),
            pl.BlockSpec((1, KV_LORA_RANK), const2),
            pl.BlockSpec((KV_LORA_RANK, MLA_HEADS * HEAD_COLS), const2),
            pl.BlockSpec((1, 2 * LANES), const2),
            pl.BlockSpec((1, 2 * LANES), const2),
            pl.BlockSpec((tm, LANES), lambda bi, ti: (ti, 0)),
            pl.BlockSpec((len(POOL_WINDOWS), POOL_GROUP_DIM, POOL_GROUP_DIM), lambda bi, ti: (0, 0, 0)),
            pl.BlockSpec((1, D_POOL), const2),
        ],
        out_specs=[
            pl.BlockSpec((1, MLA_HEADS, tm // Q_BLOCK, Q_BLOCK, QK_HEAD_DIM),
                         lambda bi, ti: (bi, 0, ti, 0, 0)),
            pl.BlockSpec((1, MLA_HEADS, tm, QK_HEAD_DIM), lambda bi, ti: (bi, 0, ti, 0)),
            pl.BlockSpec((1, MLA_HEADS, tm, V_AUG_DIM), lambda bi, ti: (bi, 0, ti, 0)),
            pl.BlockSpec((1, tm, D_ATTN), lambda bi, ti: (bi, ti, 0)),
            pl.BlockSpec((1, tm, D_POOL), lambda bi, ti: (bi, ti, 0)),
        ],
        out_shape=[
            jax.ShapeDtypeStruct((b, MLA_HEADS, s // Q_BLOCK, Q_BLOCK, QK_HEAD_DIM), BF16),
            jax.ShapeDtypeStruct((b, MLA_HEADS, s, QK_HEAD_DIM), BF16),
            jax.ShapeDtypeStruct((b, MLA_HEADS, s, V_AUG_DIM), BF16),
            jax.ShapeDtypeStruct((b, s, D_ATTN), BF16),
            jax.ShapeDtypeStruct((b, s, D_POOL), BF16),
        ],
        scratch_shapes=[
            pltpu.VMEM((tm + 2 * HALO, d), BF16),
        ],
        compiler_params=pltpu.CompilerParams(
            dimension_semantics=("parallel", "parallel"),
            vmem_limit_bytes=VMEM_LIMIT),
        name="proj",
    )(x, x, x, mod3, norm_g, w_in_r, qlg, w_uq_r, kvlg, w_ukv_b, gq, gk, cs, w_pool_b, ps)


def _ctxkv_kernel(x_ref, mod_ref, ng_ref, w_kv_ref, kvlg_ref, w_ukv_ref, gk_ref, cs_ref,
                  k_ref, v_ref):
    d = x_ref.shape[-1]
    shift = mod_ref[0, :, 0:d]
    scale = mod_ref[0, :, d:2 * d]
    a = ng_ref[...] * (1.0 + scale)
    for bi in range(x_ref.shape[0]):
        xv = x_ref[bi]
        ms = jnp.mean(xv * xv, axis=-1, keepdims=True)
        hc = (xv * lax.rsqrt(ms + NORM_EPS) * a + shift).astype(BF16)
        u = jnp.dot(hc, w_kv_ref[...], preferred_element_type=F32)
        _kv_heads(u[:, 0:KV_LORA_RANK], u[:, KV_LORA_RANK:], kvlg_ref[...], w_ukv_ref[...],
                  gk_ref, cs_ref[...], k_ref.at[pl.ds(bi, 1)], v_ref.at[pl.ds(bi, 1)],
                  slice(None))


def _ctxkv(ctx, mod3, norm_g, w_in_r, kvlg, w_ukv_b, gk, cs_ctx, *, bb):
    b, lc, d = ctx.shape
    const2 = lambda bi: (0, 0)
    nkv = C_GA - C_CKV
    assert C_CKV % nkv == 0
    return pl.pallas_call(
        _ctxkv_kernel,
        grid=(b // bb,),
        in_specs=[
            pl.BlockSpec((bb, lc, d), lambda bi: (bi, 0, 0)),
            pl.BlockSpec((1, 1, 3 * d), lambda bi: (b, 0, 0)),
            pl.BlockSpec((1, d), const2),
            pl.BlockSpec((d, nkv), lambda bi: (0, C_CKV // nkv)),
            pl.BlockSpec((1, KV_LORA_RANK), const2),
            pl.BlockSpec((KV_LORA_RANK, MLA_HEADS * HEAD_COLS), const2),
            pl.BlockSpec((1, 2 * LANES), const2),
            pl.BlockSpec((lc, LANES), const2),
        ],
        out_specs=[
            pl.BlockSpec((bb, MLA_HEADS, lc, QK_HEAD_DIM), lambda bi: (bi, 0, 0, 0)),
            pl.BlockSpec((bb, MLA_HEADS, lc, V_AUG_DIM), lambda bi: (bi, 0, 0, 0)),
        ],
        out_shape=[
            jax.ShapeDtypeStruct((b, MLA_HEADS, lc, QK_HEAD_DIM), BF16),
            jax.ShapeDtypeStruct((b, MLA_HEADS, lc, V_AUG_DIM), BF16),
        ],
        compiler_params=pltpu.CompilerParams(
            dimension_semantics=("parallel",), vmem_limit_bytes=VMEM_LIMIT),
        name="ctxkv",
    )(ctx, mod3, norm_g, w_in_r, kvlg, w_ukv_b, gk, cs_ctx)


def _lane_tile(a, n):
    return a if n == 1 else jnp.concatenate([a] * n, axis=1)


def _attn_out_kernel(q_ref, k_ref, v_ref, kc_ref, vc_ref, sga_ref, brp_ref, x_ref, mod_ref,
                     w_ref, shift_ref, o_ref, attn_scr, *stat_scr, tk, running_max):
    nb, ib, dk = q_ref.shape[2:]
    rows = nb * ib
    nk = k_ref.shape[2] // tk
    dv = V_HEAD_DIM
    d = x_ref.shape[-1]

    def keys(h, c):
        return kc_ref[0, h] if c == 0 else k_ref[0, h, (c - 1) * tk:c * tk, :]

    def values(h, c):
        return vc_ref[0, h] if c == 0 else v_ref[0, h, (c - 1) * tk:c * tk, :]

    def scores(h, c):
        q = q_ref[0, h].reshape(rows, dk)
        return lax.dot_general(q, keys(h, c), (((1,), (1,)), ((), ())),
                               preferred_element_type=F32)

    gate = mod_ref[0, :, 2 * d:3 * d]
    o_ref[0] = x_ref[0] + gate * jnp.dot(brp_ref[0], w_ref[D_ATTN:, :],
                                         preferred_element_type=F32)

    if running_max:
        m_scr, acc_scr = stat_scr

        def accumulate(h, s, vb):
            m_prev = m_scr[h]
            m_next = jnp.maximum(m_prev, jnp.max(s, axis=-1, keepdims=True))
            alpha = jnp.exp2(m_prev - m_next)
            p = jnp.exp2(s - _lane_tile(m_next, s.shape[1] // LANES))
            pv = jnp.dot(p.astype(BF16), vb, preferred_element_type=F32)
            acc_scr[h] = _lane_tile(alpha, V_AUG_DIM // LANES) * acc_scr[h] + pv
            m_scr[h] = m_next

        m_scr[...] = jnp.full(m_scr.shape, -jnp.inf, F32)
        acc_scr[...] = jnp.zeros(acc_scr.shape, F32)
        for c in range(nk + 1):
            for h in range(MLA_HEADS):
                accumulate(h, scores(h, c), values(h, c))
        for h in range(MLA_HEADS):
            attn_scr[h] = acc_scr[h, :, 0:dv] / acc_scr[h, :, dv:2 * dv]
    else:
        acc = [None] * MLA_HEADS
        for c in range(nk + 1):
            for h in range(MLA_HEADS):
                s = scores(h, c)
                p = jnp.exp2(s - shift_ref[:, 0:s.shape[1]]).astype(BF16)
                pv = jnp.dot(p, values(h, c), preferred_element_type=F32)
                acc[h] = pv if acc[h] is None else acc[h] + pv
        for h in range(MLA_HEADS):
            attn_scr[h] = acc[h][:, 0:dv] / acc[h][:, dv:2 * dv]
    pair = 2 * dv
    for h0 in range(0, MLA_HEADS, 2):
        attn = jnp.concatenate(
            [jnp.concatenate([attn_scr[h, pl.ds(il, nb, stride=ib), :] for il in range(ib)],
                             axis=0) for h in (h0, h0 + 1)], axis=1)
        cols = slice(h0 * dv, h0 * dv + pair)
        br_a = (sga_ref[0, :, cols].astype(F32) * attn).astype(BF16)
        o_ref[0] = o_ref[0] + gate * jnp.dot(br_a, w_ref[cols, :], preferred_element_type=F32)


def _attn_out(q5, k, v, kc, vc, sga, brp, x, mod3, w_out_b, shift, *, ib, tk, running_max):
    b, h, nb, _, dk = q5.shape
    _, s, d = x.shape
    lc = kc.shape[2]
    dva = v.shape[-1]
    rows = nb * ib
    assert lc <= tk and s % tk == 0
    kern = functools.partial(_attn_out_kernel, tk=tk, running_max=running_max)
    stat_scr = [pltpu.VMEM((h, rows, LANES), F32), pltpu.VMEM((h, rows, dva), F32)]
    resident = dict(pipeline_mode=pl.Buffered(1))
    row_tile = lambda bi, ti: (bi, ti, 0)
    return pl.pallas_call(
        kern,
        grid=(b, Q_BLOCK // ib),
        in_specs=[
            pl.BlockSpec((1, h, nb, ib, dk), lambda bi, ti: (bi, 0, 0, ti, 0)),
            pl.BlockSpec((1, h, s, dk), lambda bi, ti: (bi, 0, 0, 0)),
            pl.BlockSpec((1, h, s, dva), lambda bi, ti: (bi, 0, 0, 0)),
            pl.BlockSpec((1, h, lc, dk), lambda bi, ti: (bi, 0, 0, 0), **resident),
            pl.BlockSpec((1, h, lc, dva), lambda bi, ti: (bi, 0, 0, 0), **resident),
            pl.BlockSpec((1, rows, D_ATTN), row_tile),
            pl.BlockSpec((1, rows, D_POOL), row_tile),
            pl.BlockSpec((1, rows, d), row_tile),
            pl.BlockSpec((1, 1, 3 * d), lambda bi, ti: (bi, 0, 0)),
            pl.BlockSpec((D_ATTN + D_POOL, d), lambda bi, ti: (0, 0), **resident),
            pl.BlockSpec((1, tk), lambda bi, ti: (0, 0)),
        ],
        out_specs=pl.BlockSpec((1, rows, d), row_tile),
        out_shape=jax.ShapeDtypeStruct((b, s, d), F32),
        scratch_shapes=[pltpu.VMEM((h, rows, V_HEAD_DIM), F32)] + (stat_scr if running_max else []),
        compiler_params=pltpu.CompilerParams(
            dimension_semantics=("parallel", "arbitrary"),
            vmem_limit_bytes=VMEM_LIMIT),
        name="attn_out_rmax" if running_max else "attn_out",
    )(q5, k, v, kc, vc, sga, brp, x, mod3, w_out_b, shift)


def _rotate_half_cols(w, signed):
    q = QK_ROPE_DIM // 4
    a, b, c, d = (w[..., i * q:(i + 1) * q] for i in range(4))
    if signed:
        return jnp.concatenate([-b, a, -d, c], axis=-1)
    return jnp.concatenate([b, a, d, c], axis=-1)


def _rope_table(seq):
    rows = seq // GRID_W
    row = np.repeat(np.arange(rows, dtype=np.float32), GRID_W)
    col = np.tile(np.arange(GRID_W, dtype=np.float32), rows)
    n_freq = QK_ROPE_DIM // 4
    inv = np.float32(ROPE_BASE) ** (-np.arange(n_freq, dtype=np.float32) / np.float32(n_freq))
    ang_r = row[:, None] * inv
    ang_c = col[:, None] * inv
    ang = np.concatenate([ang_r, ang_r, ang_c, ang_c], axis=-1).astype(np.float64)
    table = np.concatenate([np.cos(ang), np.sin(ang)], axis=-1)
    return jnp.asarray(table, dtype=F32)


def kernel(x, c, ctx, c_ctx, w_mod, b_mod, norm_g, w_in, q_lora_g, w_uq, kv_lora_g, w_ukv,
           q_norm_g, k_norm_g, w_pool, pool_scale, w_out):
    b, s, d = x.shape
    lc = ctx.shape[1]
    depth = w_mod.shape[0]
    assert depth == 1, "single-layer block"
    assert b + 1 <= MOD_ROWS and s % PROJ_TM == 0 and s % ATTN_TK == 0
    assert s % Q_BLOCK == 0 and Q_BLOCK % ATTN_IB == 0 and s % GRID_W == 0
    l = 0

    cc = jnp.concatenate([c, c_ctx[None, :], jnp.zeros((MOD_ROWS - b - 1, d), F32)], axis=0)
    mod = _modulation(cc, w_mod, b_mod[l][None, :], l)
    mod3 = mod[:, None, :]

    wi = w_in.astype(BF16)[l]
    o_cq, o_ckv, o_kr = 0, Q_LORA_RANK, Q_LORA_RANK + KV_LORA_RANK
    o_ga = o_kr + QK_ROPE_DIM
    o_pin = o_ga + D_ATTN
    o_gp = o_pin + D_POOL
    w_kr = wi[:, o_kr:o_ga]
    w_in_r = jnp.concatenate(
        [wi[:, o_cq:o_kr], w_kr, _rotate_half_cols(w_kr, True), wi[:, o_ga:o_pin],
         wi[:, o_gp:], wi[:, o_pin:o_gp]], axis=1)
    wq = w_uq.astype(BF16)[l].reshape(Q_LORA_RANK, MLA_HEADS, QK_HEAD_DIM)
    wq_rope = wq[:, :, QK_NOPE_DIM:]
    w_uq_r = jnp.concatenate([wq, _rotate_half_cols(wq_rope, True)], axis=-1)
    w_uq_r = w_uq_r.reshape(Q_LORA_RANK, MLA_HEADS * HEAD_COLS)
    w_ukv_b = w_ukv.astype(BF16)[l]
    w_pool_b = w_pool.astype(BF16)[l]
    w_out_b = w_out.astype(BF16)[l]

    bound = (Q_SCALE * QK_HEAD_DIM) * jnp.max(jnp.abs(q_norm_g[l])) * jnp.max(jnp.abs(k_norm_g[l]))

    def gain_row(g, const):
        return (jnp.concatenate([g, _rotate_half_cols(g[QK_NOPE_DIM:], False)]) * const)[None, :]

    gq = gain_row(q_norm_g[l], Q_SCALE * math.sqrt(QK_HEAD_DIM))
    gk = gain_row(k_norm_g[l], math.sqrt(QK_HEAD_DIM))
    cs = _rope_table(s)
    cs_ctx = jnp.concatenate([jnp.ones((lc, QK_ROPE_DIM), F32),
                              jnp.zeros((lc, QK_ROPE_DIM), F32)], axis=1)

    ng = norm_g[l][None, :]
    qlg = q_lora_g[l][None, :]
    kvlg = kv_lora_g[l][None, :]
    ps = pool_scale[l][None, :]

    q, k, v, sga, brp = _proj(x, mod3, ng, w_in_r, qlg, w_uq_r, kvlg, w_ukv_b, gq, gk, cs,
                              w_pool_b, ps, tm=PROJ_TM, nsub=PROJ_NSUB)
    kc, vc = _ctxkv(ctx, mod3, ng, w_in_r, kvlg, w_ukv_b, gk, cs_ctx,
                    bb=b // 2 if b % 2 == 0 else b)
    shift = jnp.full((1, ATTN_TK), bound, F32)
    args = (q, k, v, kc, vc, sga, brp, x, mod3, w_out_b, shift)
    attn_out = functools.partial(_attn_out, ib=ATTN_IB, tk=ATTN_TK)
    return lax.cond(2.0 * bound < MAX_SHIFT_RANGE,
                    lambda a: attn_out(*a, running_max=False),
                    lambda a: attn_out(*a, running_max=True), args)
```

```python
import functools
import math

import jax
import jax.numpy as jnp
import numpy as np
from jax import lax
from jax.experimental import pallas as pl
from jax.experimental.pallas import tpu as pltpu

F32 = jnp.float32
BF16 = jnp.bfloat16

GRID_W = 64
MLA_HEADS = 4
QK_NOPE_DIM = 128
QK_ROPE_DIM = 64
QK_HEAD_DIM = QK_NOPE_DIM + QK_ROPE_DIM
V_HEAD_DIM = 128
V_AUG_DIM = 2 * V_HEAD_DIM
MAX_SHIFT_RANGE = 110.0
Q_LORA_RANK = 256
KV_LORA_RANK = 128
ROPE_BASE = 10000.0
POOL_WINDOWS = (2, 4, 8, 16)
POOL_GROUP_DIM = 128
NORM_EPS = 1e-6
Q_BLOCK = 128
D_ATTN = MLA_HEADS * V_HEAD_DIM
D_POOL = len(POOL_WINDOWS) * POOL_GROUP_DIM

LANES = 128
HEAD_COLS = 2 * LANES
HALO = 16
EDGE = 8
assert max(POOL_WINDOWS) // 2 <= EDGE <= HALO
MOD_ROWS = 16

C_CQ = 0
C_CKV = C_CQ + Q_LORA_RANK
C_KR = C_CKV + KV_LORA_RANK
C_GA = C_KR + 2 * QK_ROPE_DIM
C_GP = C_GA + D_ATTN
C_PIN = C_GP + D_POOL
C_END = C_PIN + D_POOL

VMEM_LIMIT = 56 * 1024 * 1024
MOD_TN = 768
PROJ_TM = 1024
PROJ_NSUB = 2
ATTN_IB = 16
ATTN_TK = 1024

Q_SCALE = QK_HEAD_DIM ** -0.5 * math.log2(math.e)


def _silu(x):
    hx = 0.5 * x
    return hx + hx * jnp.tanh(hx)


def _window_sum(p, w):
    n = p.shape[0]
    up = lambda a, k: pltpu.roll(a, n - k, axis=0)
    down = lambda a, k: pltpu.roll(a, k, axis=0)
    if w == 2:
        return p + down(p, 1)
    acc = p + up(p, 1)
    span = 2
    while span < w // 2:
        acc = acc + up(acc, span)
        span *= 2
    return acc + down(acc, w // 2)


def _rms_rows(x, g):
    ms = jnp.mean(x * x, axis=-1, keepdims=True)
    return x * lax.rsqrt(ms + NORM_EPS) * g


def _mod_kernel(cc_ref, w_ref, b_ref, o_ref):
    a = _silu(cc_ref[...]).astype(BF16)
    o_ref[...] = jnp.dot(a, w_ref[0].astype(BF16), preferred_element_type=F32) + b_ref[...]


def _modulation(cc, w_mod, b_mod, layer):
    _, d, n = w_mod.shape
    tn = MOD_TN
    return pl.pallas_call(
        _mod_kernel,
        grid=(n // tn,),
        in_specs=[
            pl.BlockSpec((MOD_ROWS, d), lambda j: (0, 0)),
            pl.BlockSpec((1, d, tn), lambda j: (layer, 0, j)),
            pl.BlockSpec((1, tn), lambda j: (0, j)),
        ],
        out_specs=pl.BlockSpec((MOD_ROWS, tn), lambda j: (0, j)),
        out_shape=jax.ShapeDtypeStruct((MOD_ROWS, n), F32),
        name="mod",
    )(cc, w_mod, b_mod)


HEAD_SS_EPS = QK_HEAD_DIM * NORM_EPS


def _rope_pair(rr, csg):
    pr = rr * csg
    return pr + pltpu.roll(pr, QK_ROPE_DIM, axis=1)


def _half_rope_sq(rr):
    return 0.5 * (rr * rr)


def _kv_heads(ckv, kr2, kvlg, w_ukv, gk_ref, cs, k_ref, v_ref, rows):
    ckv_n = _rms_rows(ckv, kvlg).astype(BF16)
    kv = jnp.dot(ckv_n, w_ukv, preferred_element_type=F32)
    g_nope = gk_ref[:, 0:LANES]
    k_rope = _rope_pair(kr2, cs * gk_ref[:, LANES:2 * LANES])
    sq_rope = _half_rope_sq(kr2)
    for h in range(MLA_HEADS):
        kn = kv[:, h * HEAD_COLS:h * HEAD_COLS + LANES]
        vv = kv[:, h * HEAD_COLS + LANES:(h + 1) * HEAD_COLS]
        ss = jnp.sum(kn * kn + sq_rope, axis=-1, keepdims=True)
        r = lax.rsqrt(ss + HEAD_SS_EPS)
        k_ref[0, h, rows, 0:LANES] = (kn * r * g_nope).astype(BF16)
        k_ref[0, h, rows, LANES:] = (k_rope * r)[:, 0:QK_ROPE_DIM].astype(BF16)
        v_ref[0, h, rows, 0:V_HEAD_DIM] = vv.astype(BF16)
        v_ref[0, h, rows, V_HEAD_DIM:] = jnp.ones((vv.shape[0], V_AUG_DIM - V_HEAD_DIM), BF16)


def _proj_kernel(x_ref, xp_ref, xn_ref, mod_ref, ng_ref, w_in_ref, qlg_ref, w_uq_ref,
                 kvlg_ref, w_ukv_ref, gq_ref, gk_ref, cs_ref, w_pool_ref, ps_ref,
                 q_ref, k_ref, v_ref, sga_ref, brp_ref,
                 h_scr, *, tm, seq, nsub):
    t = pl.program_id(1)
    nt = pl.num_programs(1)
    d = x_ref.shape[-1]
    shift = mod_ref[0, :, 0:d]
    scale = mod_ref[0, :, d:2 * d]
    a = ng_ref[...] * (1.0 + scale)

    def hnorm(xv):
        ms = jnp.mean(xv * xv, axis=-1, keepdims=True)
        return xv * lax.rsqrt(ms + NORM_EPS) * a + shift

    nh = tm // nsub
    h_scr[0:HALO, :] = jnp.where(t > 0, hnorm(xp_ref[0]), 0.0).astype(BF16)
    for i in range(nsub):
        h_scr[HALO + i * nh:HALO + (i + 1) * nh, :] = hnorm(x_ref[0, i * nh:(i + 1) * nh, :]).astype(BF16)
    h_scr[HALO + tm:, :] = jnp.where(t < nt - 1, hnorm(xn_ref[0]), 0.0).astype(BF16)

    cs_all = cs_ref[...]
    gq_nope = gq_ref[:, 0:LANES]
    gq_rr = gq_ref[:, LANES:2 * LANES]
    assert (tm // nsub) % Q_BLOCK == 0

    def h_rows(i):
        return h_scr[HALO + i * nh:HALO + (i + 1) * nh, :]

    u_gate = []
    for i in range(nsub):
        rows = slice(i * nh, (i + 1) * nh)
        cs = cs_all[rows]
        u = jnp.dot(h_rows(i), w_in_ref[:, C_CQ:C_GP], preferred_element_type=F32)
        u_a = u[:, C_CQ:C_GA]
        u_gate.append(u[:, C_GA:C_GP])
        _kv_heads(u_a[:, C_CKV:C_KR], u_a[:, C_KR:C_GA], kvlg_ref[...], w_ukv_ref[...],
                  gk_ref, cs, k_ref, v_ref, rows)
        cq_n = _rms_rows(u_a[:, C_CQ:C_CKV], qlg_ref[...]).astype(BF16)
        qa = jnp.dot(cq_n, w_uq_ref[...], preferred_element_type=F32)
        csq = cs * gq_rr
        for h in range(MLA_HEADS):
            qn = qa[:, h * HEAD_COLS:h * HEAD_COLS + LANES]
            rr = qa[:, h * HEAD_COLS + LANES:(h + 1) * HEAD_COLS]
            ss = jnp.sum(qn * qn + _half_rope_sq(rr), axis=-1, keepdims=True)
            r = lax.rsqrt(ss + HEAD_SS_EPS)
            q_lo = (qn * r * gq_nope).astype(BF16)
            q_hi = (_rope_pair(rr, csq) * r)[:, 0:QK_ROPE_DIM].astype(BF16)
            for j in range(nh // Q_BLOCK):
                blk = i * (nh // Q_BLOCK) + j
                q_ref[0, h, blk, :, 0:LANES] = q_lo[j * Q_BLOCK:(j + 1) * Q_BLOCK]
                q_ref[0, h, blk, :, LANES:] = q_hi[j * Q_BLOCK:(j + 1) * Q_BLOCK]

    for i in range(nsub):
        sga_ref[0, i * nh:(i + 1) * nh, :] = _silu(u_gate[i]).astype(BF16)

    for i in range(nsub):
        rows = slice(i * nh, (i + 1) * nh)
        pin = jnp.dot(h_scr[i * nh:(i + 1) * nh + 2 * HALO, :], w_in_ref[:, C_PIN:C_END],
                      preferred_element_type=F32)
        u_gp = jnp.dot(h_rows(i), w_in_ref[:, C_GP:C_PIN], preferred_element_type=F32)
        tok_head = t * tm + i * nh + lax.broadcasted_iota(jnp.int32, (EDGE, LANES), 0)
        tok_tail = tok_head + (nh - EDGE)
        for g, w in enumerate(POOL_WINDOWS):
            c0 = g * POOL_GROUP_DIM
            c1 = c0 + POOL_GROUP_DIM
            pg = pin[:, c0:c1]
            ws = _window_sum(pg, w)[HALO:HALO + nh]

            def inv_count(tok):
                lo = jnp.maximum(tok - w // 2, 0)
                hi = jnp.minimum(tok - w // 2 + w, seq)
                return 1.0 / (hi - lo).astype(F32)

            inv_cnt = jnp.concatenate(
                [inv_count(tok_head), jnp.full((nh - 2 * EDGE, LANES), 1.0 / w, F32),
                 inv_count(tok_tail)], axis=0)
            pooled = ws * inv_cnt - pg[HALO:HALO + nh]
            y = jnp.dot(pooled.astype(BF16), w_pool_ref[g], preferred_element_type=F32)
            brp_ref[0, rows, c0:c1] = (_silu(u_gp[:, c0:c1]) * (y * ps_ref[:, c0:c1])).astype(BF16)


def _proj(x, mod3, norm_g, w_in_r, qlg, w_uq_r, kvlg, w_ukv_b, gq, gk, cs, w_pool_b, ps, *,
          tm, nsub):
    b, s, d = x.shape
    nt = s // tm
    hb = tm // HALO
    nhb = s // HALO
    const2 = lambda bi, ti: (0, 0)
    kern = functools.partial(_proj_kernel, tm=tm, seq=s, nsub=nsub)
    return pl.pallas_call(
        kern,
        grid=(b, nt),
        in_specs=[
            pl.BlockSpec((1, tm, d), lambda bi, ti: (bi, ti, 0)),
            pl.BlockSpec((1, HALO, d), lambda bi, ti: (bi, jnp.maximum(ti * hb - 1, 0), 0)),
            pl.BlockSpec((1, HALO, d), lambda bi, ti: (bi, jnp.minimum((ti + 1) * hb, nhb - 1), 0)),
            pl.BlockSpec((1, 1, 3 * d), lambda bi, ti: (bi, 0, 0)),
            pl.BlockSpec((1, d), const2),
            pl.BlockSpec((d, C_END), const2),
            pl.BlockSpec((1, Q_LORA_RANK), const2),
            pl.BlockSpec((Q_LORA_RANK, MLA_HEADS * HEAD_COLS), const2),
            pl.BlockSpec((1, KV_LORA_RANK), const2),
            pl.BlockSpec((KV_LORA_RANK, MLA_HEADS * HEAD_COLS), const2),
            pl.BlockSpec((1, 2 * LANES), const2),
            pl.BlockSpec((1, 2 * LANES), const2),
            pl.BlockSpec((tm, LANES), lambda bi, ti: (ti, 0)),
            pl.BlockSpec((len(POOL_WINDOWS), POOL_GROUP_DIM, POOL_GROUP_DIM), lambda bi, ti: (0, 0, 0)),
            pl.BlockSpec((1, D_POOL), const2),
        ],
        out_specs=[
            pl.BlockSpec((1, MLA_HEADS, tm // Q_BLOCK, Q_BLOCK, QK_HEAD_DIM),
                         lambda bi, ti: (bi, 0, ti, 0, 0)),
            pl.BlockSpec((1, MLA_HEADS, tm, QK_HEAD_DIM), lambda bi, ti: (bi, 0, ti, 0)),
            pl.BlockSpec((1, MLA_HEADS, tm, V_AUG_DIM), lambda bi, ti: (bi, 0, ti, 0)),
            pl.BlockSpec((1, tm, D_ATTN), lambda bi, ti: (bi, ti, 0)),
            pl.BlockSpec((1, tm, D_POOL), lambda bi, ti: (bi, ti, 0)),
        ],
        out_shape=[
            jax.ShapeDtypeStruct((b, MLA_HEADS, s // Q_BLOCK, Q_BLOCK, QK_HEAD_DIM), BF16),
            jax.ShapeDtypeStruct((b, MLA_HEADS, s, QK_HEAD_DIM), BF16),
            jax.ShapeDtypeStruct((b, MLA_HEADS, s, V_AUG_DIM), BF16),
            jax.ShapeDtypeStruct((b, s, D_ATTN), BF16),
            jax.ShapeDtypeStruct((b, s, D_POOL), BF16),
        ],
        scratch_shapes=[
            pltpu.VMEM((tm + 2 * HALO, d), BF16),
        ],
        compiler_params=pltpu.CompilerParams(
            dimension_semantics=("parallel", "parallel"),
            vmem_limit_bytes=VMEM_LIMIT),
        name="proj",
    )(x, x, x, mod3, norm_g, w_in_r, qlg, w_uq_r, kvlg, w_ukv_b, gq, gk, cs, w_pool_b, ps)


def _ctxkv_kernel(x_ref, mod_ref, ng_ref, w_kv_ref, kvlg_ref, w_ukv_ref, gk_ref, cs_ref,
                  k_ref, v_ref):
    d = x_ref.shape[-1]
    shift = mod_ref[0, :, 0:d]
    scale = mod_ref[0, :, d:2 * d]
    a = ng_ref[...] * (1.0 + scale)
    for bi in range(x_ref.shape[0]):
        xv = x_ref[bi]
        ms = jnp.mean(xv * xv, axis=-1, keepdims=True)
        hc = (xv * lax.rsqrt(ms + NORM_EPS) * a + shift).astype(BF16)
        u = jnp.dot(hc, w_kv_ref[...], preferred_element_type=F32)
        _kv_heads(u[:, 0:KV_LORA_RANK], u[:, KV_LORA_RANK:], kvlg_ref[...], w_ukv_ref[...],
                  gk_ref, cs_ref[...], k_ref.at[pl.ds(bi, 1)], v_ref.at[pl.ds(bi, 1)],
                  slice(None))


def _ctxkv(ctx, mod3, norm_g, w_in_r, kvlg, w_ukv_b, gk, cs_ctx, *, bb):
    b, lc, d = ctx.shape
    const2 = lambda bi: (0, 0)
    nkv = C_GA - C_CKV
    assert C_CKV % nkv == 0
    return pl.pallas_call(
        _ctxkv_kernel,
        grid=(b // bb,),
        in_specs=[
            pl.BlockSpec((bb, lc, d), lambda bi: (bi, 0, 0)),
            pl.BlockSpec((1, 1, 3 * d), lambda bi: (b, 0, 0)),
            pl.BlockSpec((1, d), const2),
            pl.BlockSpec((d, nkv), lambda bi: (0, C_CKV // nkv)),
            pl.BlockSpec((1, KV_LORA_RANK), const2),
            pl.BlockSpec((KV_LORA_RANK, MLA_HEADS * HEAD_COLS), const2),
            pl.BlockSpec((1, 2 * LANES), const2),
            pl.BlockSpec((lc, LANES), const2),
        ],
        out_specs=[
            pl.BlockSpec((bb, MLA_HEADS, lc, QK_HEAD_DIM), lambda bi: (bi, 0, 0, 0)),
            pl.BlockSpec((bb, MLA_HEADS, lc, V_AUG_DIM), lambda bi: (bi, 0, 0, 0)),
        ],
        out_shape=[
            jax.ShapeDtypeStruct((b, MLA_HEADS, lc, QK_HEAD_DIM), BF16),
            jax.ShapeDtypeStruct((b, MLA_HEADS, lc, V_AUG_DIM), BF16),
        ],
        compiler_params=pltpu.CompilerParams(
            dimension_semantics=("parallel",), vmem_limit_bytes=VMEM_LIMIT),
        name="ctxkv",
    )(ctx, mod3, norm_g, w_in_r, kvlg, w_ukv_b, gk, cs_ctx)


def _lane_tile(a, n):
    return a if n == 1 else jnp.concatenate([a] * n, axis=1)


def _attn_out_kernel(q_ref, k_ref, v_ref, kc_ref, vc_ref, sga_ref, brp_ref, x_ref, mod_ref,
                     w_ref, shift_ref, o_ref, attn_scr, *stat_scr, tk, running_max):
    nb, ib, dk = q_ref.shape[2:]
    rows = nb * ib
    nk = k_ref.shape[2] // tk
    dv = V_HEAD_DIM
    d = x_ref.shape[-1]

    def keys(h, c):
        return kc_ref[0, h] if c == 0 else k_ref[0, h, (c - 1) * tk:c * tk, :]

    def values(h, c):
        return vc_ref[0, h] if c == 0 else v_ref[0, h, (c - 1) * tk:c * tk, :]

    def scores(h, c):
        q = q_ref[0, h].reshape(rows, dk)
        return lax.dot_general(q, keys(h, c), (((1,), (1,)), ((), ())),
                               preferred_element_type=F32)

    gate = mod_ref[0, :, 2 * d:3 * d]
    o_ref[0] = x_ref[0] + gate * jnp.dot(brp_ref[0], w_ref[D_ATTN:, :],
                                         preferred_element_type=F32)

    if running_max:
        m_scr, acc_scr = stat_scr

        def accumulate(h, s, vb):
            m_prev = m_scr[h]
            m_next = jnp.maximum(m_prev, jnp.max(s, axis=-1, keepdims=True))
            alpha = jnp.exp2(m_prev - m_next)
            p = jnp.exp2(s - _lane_tile(m_next, s.shape[1] // LANES))
            pv = jnp.dot(p.astype(BF16), vb, preferred_element_type=F32)
            acc_scr[h] = _lane_tile(alpha, V_AUG_DIM // LANES) * acc_scr[h] + pv
            m_scr[h] = m_next

        m_scr[...] = jnp.full(m_scr.shape, -jnp.inf, F32)
        acc_scr[...] = jnp.zeros(acc_scr.shape, F32)
        for c in range(nk + 1):
            for h in range(MLA_HEADS):
                accumulate(h, scores(h, c), values(h, c))
        for h in range(MLA_HEADS):
            attn_scr[h] = acc_scr[h, :, 0:dv] / acc_scr[h, :, dv:2 * dv]
    else:
        acc = [None] * MLA_HEADS
        for c in range(nk + 1):
            for h in range(MLA_HEADS):
                s = scores(h, c)
                p = jnp.exp2(s - shift_ref[:, 0:s.shape[1]]).astype(BF16)
                pv = jnp.dot(p, values(h, c), preferred_element_type=F32)
                acc[h] = pv if acc[h] is None else acc[h] + pv
        for h in range(MLA_HEADS):
            attn_scr[h] = acc[h][:, 0:dv] / acc[h][:, dv:2 * dv]
    pair = 2 * dv
    for h0 in range(0, MLA_HEADS, 2):
        attn = jnp.concatenate(
            [jnp.concatenate([attn_scr[h, pl.ds(il, nb, stride=ib), :] for il in range(ib)],
                             axis=0) for h in (h0, h0 + 1)], axis=1)
        cols = slice(h0 * dv, h0 * dv + pair)
        br_a = (sga_ref[0, :, cols].astype(F32) * attn).astype(BF16)
        o_ref[0] = o_ref[0] + gate * jnp.dot(br_a, w_ref[cols, :], preferred_element_type=F32)


def _attn_out(q5, k, v, kc, vc, sga, brp, x, mod3, w_out_b, shift, *, ib, tk, running_max):
    b, h, nb, _, dk = q5.shape
    _, s, d = x.shape
    lc = kc.shape[2]
    dva = v.shape[-1]
    rows = nb * ib
    assert lc <= tk and s % tk == 0
    kern = functools.partial(_attn_out_kernel, tk=tk, running_max=running_max)
    stat_scr = [pltpu.VMEM((h, rows, LANES), F32), pltpu.VMEM((h, rows, dva), F32)]
    resident = dict(pipeline_mode=pl.Buffered(1))
    row_tile = lambda bi, ti: (bi, ti, 0)
    return pl.pallas_call(
        kern,
        grid=(b, Q_BLOCK // ib),
        in_specs=[
            pl.BlockSpec((1, h, nb, ib, dk), lambda bi, ti: (bi, 0, 0, ti, 0)),
            pl.BlockSpec((1, h, s, dk), lambda bi, ti: (bi, 0, 0, 0)),
            pl.BlockSpec((1, h, s, dva), lambda bi, ti: (bi, 0, 0, 0)),
            pl.BlockSpec((1, h, lc, dk), lambda bi, ti: (bi, 0, 0, 0), **resident),
            pl.BlockSpec((1, h, lc, dva), lambda bi, ti: (bi, 0, 0, 0), **resident),
            pl.BlockSpec((1, rows, D_ATTN), row_tile),
            pl.BlockSpec((1, rows, D_POOL), row_tile),
            pl.BlockSpec((1, rows, d), row_tile),
            pl.BlockSpec((1, 1, 3 * d), lambda bi, ti: (bi, 0, 0)),
            pl.BlockSpec((D_ATTN + D_POOL, d), lambda bi, ti: (0, 0), **resident),
            pl.BlockSpec((1, tk), lambda bi, ti: (0, 0)),
        ],
        out_specs=pl.BlockSpec((1, rows, d), row_tile),
        out_shape=jax.ShapeDtypeStruct((b, s, d), F32),
        scratch_shapes=[pltpu.VMEM((h, rows, V_HEAD_DIM), F32)] + (stat_scr if running_max else []),
        compiler_params=pltpu.CompilerParams(
            dimension_semantics=("parallel", "arbitrary"),
            vmem_limit_bytes=VMEM_LIMIT),
        name="attn_out_rmax" if running_max else "attn_out",
    )(q5, k, v, kc, vc, sga, brp, x, mod3, w_out_b, shift)


def _rotate_half_cols(w, signed):
    q = QK_ROPE_DIM // 4
    a, b, c, d = (w[..., i * q:(i + 1) * q] for i in range(4))
    if signed:
        return jnp.concatenate([-b, a, -d, c], axis=-1)
    return jnp.concatenate([b, a, d, c], axis=-1)


def _rope_table(seq):
    rows = seq // GRID_W
    row = np.repeat(np.arange(rows, dtype=np.float32), GRID_W)
    col = np.tile(np.arange(GRID_W, dtype=np.float32), rows)
    n_freq = QK_ROPE_DIM // 4
    inv = np.float32(ROPE_BASE) ** (-np.arange(n_freq, dtype=np.float32) / np.float32(n_freq))
    ang_r = row[:, None] * inv
    ang_c = col[:, None] * inv
    ang = np.concatenate([ang_r, ang_r, ang_c, ang_c], axis=-1).astype(np.float64)
    table = np.concatenate([np.cos(ang), np.sin(ang)], axis=-1)
    return jnp.asarray(table, dtype=F32)


def kernel(x, c, ctx, c_ctx, w_mod, b_mod, norm_g, w_in, q_lora_g, w_uq, kv_lora_g, w_ukv,
           q_norm_g, k_norm_g, w_pool, pool_scale, w_out):
    b, s, d = x.shape
    lc = ctx.shape[1]
    depth = w_mod.shape[0]
    assert depth == 1, "single-layer block"
    assert b + 1 <= MOD_ROWS and s % PROJ_TM == 0 and s % ATTN_TK == 0
    assert s % Q_BLOCK == 0 and Q_BLOCK % ATTN_IB == 0 and s % GRID_W == 0
    l = 0

    cc = jnp.concatenate([c, c_ctx[None, :], jnp.zeros((MOD_ROWS - b - 1, d), F32)], axis=0)
    mod = _modulation(cc, w_mod, b_mod[l][None, :], l)
    mod3 = mod[:, None, :]

    wi = w_in.astype(BF16)[l]
    o_cq, o_ckv, o_kr = 0, Q_LORA_RANK, Q_LORA_RANK + KV_LORA_RANK
    o_ga = o_kr + QK_ROPE_DIM
    o_pin = o_ga + D_ATTN
    o_gp = o_pin + D_POOL
    w_kr = wi[:, o_kr:o_ga]
    w_in_r = jnp.concatenate(
        [wi[:, o_cq:o_kr], w_kr, _rotate_half_cols(w_kr, True), wi[:, o_ga:o_pin],
         wi[:, o_gp:], wi[:, o_pin:o_gp]], axis=1)
    wq = w_uq.astype(BF16)[l].reshape(Q_LORA_RANK, MLA_HEADS, QK_HEAD_DIM)
    wq_rope = wq[:, :, QK_NOPE_DIM:]
    w_uq_r = jnp.concatenate([wq, _rotate_half_cols(wq_rope, True)], axis=-1)
    w_uq_r = w_uq_r.reshape(Q_LORA_RANK, MLA_HEADS * HEAD_COLS)
    w_ukv_b = w_ukv.astype(BF16)[l]
    w_pool_b = w_pool.astype(BF16)[l]
    w_out_b = w_out.astype(BF16)[l]

    bound = (Q_SCALE * QK_HEAD_DIM) * jnp.max(jnp.abs(q_norm_g[l])) * jnp.max(jnp.abs(k_norm_g[l]))

    def gain_row(g, const):
        return (jnp.concatenate([g, _rotate_half_cols(g[QK_NOPE_DIM:], False)]) * const)[None, :]

    gq = gain_row(q_norm_g[l], Q_SCALE * math.sqrt(QK_HEAD_DIM))
    gk = gain_row(k_norm_g[l], math.sqrt(QK_HEAD_DIM))
    cs = _rope_table(s)
    cs_ctx = jnp.concatenate([jnp.ones((lc, QK_ROPE_DIM), F32),
                              jnp.zeros((lc, QK_ROPE_DIM), F32)], axis=1)

    ng = norm_g[l][None, :]
    qlg = q_lora_g[l][None, :]
    kvlg = kv_lora_g[l][None, :]
    ps = pool_scale[l][None, :]

    q, k, v, sga, brp = _proj(x, mod3, ng, w_in_r, qlg, w_uq_r, kvlg, w_ukv_b, gq, gk, cs,
                              w_pool_b, ps, tm=PROJ_TM, nsub=PROJ_NSUB)
    kc, vc = _ctxkv(ctx, mod3, ng, w_in_r, kvlg, w_ukv_b, gk, cs_ctx,
                    bb=b // 2 if b % 2 == 0 else b)
    shift = jnp.full((1, ATTN_TK), bound, F32)
    args = (q, k, v, kc, vc, sga, brp, x, mod3, w_out_b, shift)
    attn_out = functools.partial(_attn_out, ib=ATTN_IB, tk=ATTN_TK)
    return lax.cond(2.0 * bound < MAX_SHIFT_RANGE,
                    lambda a: attn_out(*a, running_max=False),
                    lambda a: attn_out(*a, running_max=True), args)
```

```python
import functools
import math

import jax
import jax.numpy as jnp
import numpy as np
from jax import lax
from jax.experimental import pallas as pl
from jax.experimental.pallas import tpu as pltpu

F32 = jnp.float32
BF16 = jnp.bfloat16

GRID_W = 64
MLA_HEADS = 4
QK_NOPE_DIM = 128
QK_ROPE_DIM = 64
QK_HEAD_DIM = QK_NOPE_DIM + QK_ROPE_DIM
V_HEAD_DIM = 128
V_AUG_DIM = 2 * V_HEAD_DIM
MAX_SHIFT_RANGE = 110.0
Q_LORA_RANK = 256
KV_LORA_RANK = 128
ROPE_BASE = 10000.0
POOL_WINDOWS = (2, 4, 8, 16)
POOL_GROUP_DIM = 128
NORM_EPS = 1e-6
Q_BLOCK = 128
D_ATTN = MLA_HEADS * V_HEAD_DIM
D_POOL = len(POOL_WINDOWS) * POOL_GROUP_DIM

LANES = 128
HEAD_COLS = 2 * LANES
HALO = 16
EDGE = 8
assert max(POOL_WINDOWS) // 2 <= EDGE <= HALO
MOD_ROWS = 16

C_CQ = 0
C_CKV = C_CQ + Q_LORA_RANK
C_KR = C_CKV + KV_LORA_RANK
C_GA = C_KR + 2 * QK_ROPE_DIM
C_GP = C_GA + D_ATTN
C_PIN = C_GP + D_POOL
C_END = C_PIN + D_POOL

VMEM_LIMIT = 56 * 1024 * 1024
MOD_TN = 384
CTX_BB = 2
PROJ_TM = 1024
PROJ_NSUB = 2
ATTN_IB = 16
ATTN_TK = 512

Q_SCALE = QK_HEAD_DIM ** -0.5 * math.log2(math.e)


def _silu(x):
    hx = 0.5 * x
    return hx + hx * jnp.tanh(hx)


def _window_sum(p, w):
    n = p.shape[0]
    up = lambda a, k: pltpu.roll(a, n - k, axis=0)
    down = lambda a, k: pltpu.roll(a, k, axis=0)
    if w == 2:
        return p + down(p, 1)
    acc = p + up(p, 1)
    span = 2
    while span < w // 2:
        acc = acc + up(acc, span)
        span *= 2
    return acc + down(acc, w // 2)


def _rms_rows(x, g):
    ms = jnp.mean(x * x, axis=-1, keepdims=True)
    return x * lax.rsqrt(ms + NORM_EPS) * g


def _mod_kernel(cc_ref, w_ref, b_ref, o_ref):
    a = _silu(cc_ref[...]).astype(BF16)
    o_ref[...] = jnp.dot(a, w_ref[0].astype(BF16), preferred_element_type=F32) + b_ref[...]


def _modulation(cc, w_mod, b_mod, layer):
    _, d, n = w_mod.shape
    tn = MOD_TN
    return pl.pallas_call(
        _mod_kernel,
        grid=(n // tn,),
        in_specs=[
            pl.BlockSpec((MOD_ROWS, d), lambda j: (0, 0)),
            pl.BlockSpec((1, d, tn), lambda j: (layer, 0, j)),
            pl.BlockSpec((1, tn), lambda j: (0, j)),
        ],
        out_specs=pl.BlockSpec((MOD_ROWS, tn), lambda j: (0, j)),
        out_shape=jax.ShapeDtypeStruct((MOD_ROWS, n), F32),
        name="mod",
    )(cc, w_mod, b_mod)


HEAD_SS_EPS = QK_HEAD_DIM * NORM_EPS


def _rope_pair(rr, csg):
    pr = rr * csg
    return pr + pltpu.roll(pr, QK_ROPE_DIM, axis=1)


def _half_rope_sq(rr):
    return 0.5 * (rr * rr)


def _kv_heads(ckv, kr2, kvlg, w_ukv, gk_ref, cs, k_ref, v_ref, rows):
    ckv_n = _rms_rows(ckv, kvlg).astype(BF16)
    kv = jnp.dot(ckv_n, w_ukv, preferred_element_type=F32)
    g_nope = gk_ref[:, 0:LANES]
    k_rope = _rope_pair(kr2, cs * gk_ref[:, LANES:2 * LANES])
    sq_rope = _half_rope_sq(kr2)
    for h in range(MLA_HEADS):
        kn = kv[:, h * HEAD_COLS:h * HEAD_COLS + LANES]
        vv = kv[:, h * HEAD_COLS + LANES:(h + 1) * HEAD_COLS]
        ss = jnp.sum(kn * kn + sq_rope, axis=-1, keepdims=True)
        r = lax.rsqrt(ss + HEAD_SS_EPS)
        k_ref[0, h, rows, 0:LANES] = (kn * r * g_nope).astype(BF16)
        k_ref[0, h, rows, LANES:] = (k_rope * r)[:, 0:QK_ROPE_DIM].astype(BF16)
        v_ref[0, h, rows, 0:V_HEAD_DIM] = vv.astype(BF16)
        v_ref[0, h, rows, V_HEAD_DIM:] = jnp.ones((vv.shape[0], V_AUG_DIM - V_HEAD_DIM), BF16)


def _proj_kernel(x_ref, xp_ref, xn_ref, mod_ref, ng_ref, w_in_ref, qlg_ref, w_uq_ref,
                 kvlg_ref, w_ukv_ref, gq_ref, gk_ref, cs_ref, w_pool_ref, ps_ref,
                 q_ref, k_ref, v_ref, sga_ref, brp_ref,
                 h_scr, *, tm, seq, nsub):
    t = pl.program_id(1)
    nt = pl.num_programs(1)
    d = x_ref.shape[-1]
    shift = mod_ref[0, :, 0:d]
    scale = mod_ref[0, :, d:2 * d]
    a = ng_ref[...] * (1.0 + scale)

    def hnorm(xv):
        ms = jnp.mean(xv * xv, axis=-1, keepdims=True)
        return xv * lax.rsqrt(ms + NORM_EPS) * a + shift

    nh = tm // nsub
    h_scr[0:HALO, :] = jnp.where(t > 0, hnorm(xp_ref[0]), 0.0).astype(BF16)
    for i in range(nsub):
        h_scr[HALO + i * nh:HALO + (i + 1) * nh, :] = hnorm(x_ref[0, i * nh:(i + 1) * nh, :]).astype(BF16)
    h_scr[HALO + tm:, :] = jnp.where(t < nt - 1, hnorm(xn_ref[0]), 0.0).astype(BF16)

    cs_all = cs_ref[...]
    gq_nope = gq_ref[:, 0:LANES]
    gq_rr = gq_ref[:, LANES:2 * LANES]
    assert (tm // nsub) % Q_BLOCK == 0

    def h_rows(i):
        return h_scr[HALO + i * nh:HALO + (i + 1) * nh, :]

    u_gate = []
    for i in range(nsub):
        rows = slice(i * nh, (i + 1) * nh)
        cs = cs_all[rows]
        u = jnp.dot(h_rows(i), w_in_ref[:, C_CQ:C_GP], preferred_element_type=F32)
        u_a = u[:, C_CQ:C_GA]
        u_gate.append(u[:, C_GA:C_GP])
        _kv_heads(u_a[:, C_CKV:C_KR], u_a[:, C_KR:C_GA], kvlg_ref[...], w_ukv_ref[...],
                  gk_ref, cs, k_ref, v_ref, rows)
        cq_n = _rms_rows(u_a[:, C_CQ:C_CKV], qlg_ref[...]).astype(BF16)
        qa = jnp.dot(cq_n, w_uq_ref[...], preferred_element_type=F32)
        csq = cs * gq_rr
        for h in range(MLA_HEADS):
            qn = qa[:, h * HEAD_COLS:h * HEAD_COLS + LANES]
            rr = qa[:, h * HEAD_COLS + LANES:(h + 1) * HEAD_COLS]
            ss = jnp.sum(qn * qn + _half_rope_sq(rr), axis=-1, keepdims=True)
            r = lax.rsqrt(ss + HEAD_SS_EPS)
            q_lo = (qn * r * gq_nope).astype(BF16)
            q_hi = (_rope_pair(rr, csq) * r)[:, 0:QK_ROPE_DIM].astype(BF16)
            for j in range(nh // Q_BLOCK):
                blk = i * (nh // Q_BLOCK) + j
                q_ref[0, h, blk, :, 0:LANES] = q_lo[j * Q_BLOCK:(j + 1) * Q_BLOCK]
                q_ref[0, h, blk, :, LANES:] = q_hi[j * Q_BLOCK:(j + 1) * Q_BLOCK]

    for i in range(nsub):
        sga_ref[0, i * nh:(i + 1) * nh, :] = _silu(u_gate[i]).astype(BF16)

    for i in range(nsub):
        rows = slice(i * nh, (i + 1) * nh)
        pin = jnp.dot(h_scr[i * nh:(i + 1) * nh + 2 * HALO, :], w_in_ref[:, C_PIN:C_END],
                      preferred_element_type=F32)
        u_gp = jnp.dot(h_rows(i), w_in_ref[:, C_GP:C_PIN], preferred_element_type=F32)
        tok_head = t * tm + i * nh + lax.broadcasted_iota(jnp.int32, (EDGE, LANES), 0)
        tok_tail = tok_head + (nh - EDGE)
        for g, w in enumerate(POOL_WINDOWS):
            c0 = g * POOL_GROUP_DIM
            c1 = c0 + POOL_GROUP_DIM
            pg = pin[:, c0:c1]
            ws = _window_sum(pg, w)[HALO:HALO + nh]

            def inv_count(tok):
                lo = jnp.maximum(tok - w // 2, 0)
                hi = jnp.minimum(tok - w // 2 + w, seq)
                return 1.0 / (hi - lo).astype(F32)

            inv_cnt = jnp.concatenate(
                [inv_count(tok_head), jnp.full((nh - 2 * EDGE, LANES), 1.0 / w, F32),
                 inv_count(tok_tail)], axis=0)
            pooled = ws * inv_cnt - pg[HALO:HALO + nh]
            y = jnp.dot(pooled.astype(BF16), w_pool_ref[g], preferred_element_type=F32)
            brp_ref[0, rows, c0:c1] = (_silu(u_gp[:, c0:c1]) * (y * ps_ref[:, c0:c1])).astype(BF16)


def _proj(x, mod3, norm_g, w_in_r, qlg, w_uq_r, kvlg, w_ukv_b, gq, gk, cs, w_pool_b, ps, *,
          tm, nsub):
    b, s, d = x.shape
    nt = s // tm
    hb = tm // HALO
    nhb = s // HALO
    const2 = lambda bi, ti: (0, 0)
    kern = functools.partial(_proj_kernel, tm=tm, seq=s, nsub=nsub)
    return pl.pallas_call(
        kern,
        grid=(b, nt),
        in_specs=[
            pl.BlockSpec((1, tm, d), lambda bi, ti: (bi, ti, 0)),
            pl.BlockSpec((1, HALO, d), lambda bi, ti: (bi, jnp.maximum(ti * hb - 1, 0), 0)),
            pl.BlockSpec((1, HALO, d), lambda bi, ti: (bi, jnp.minimum((ti + 1) * hb, nhb - 1), 0)),
            pl.BlockSpec((1, 1, 3 * d), lambda bi, ti: (bi, 0, 0)),
            pl.BlockSpec((1, d), const2),
            pl.BlockSpec((d, C_END), const2),
            pl.BlockSpec((1, Q_LORA_RANK), const2),
            pl.BlockSpec((Q_LORA_RANK, MLA_HEADS * HEAD_COLS), const2),
            pl.BlockSpec((1, KV_LORA_RANK), const2),
            pl.BlockSpec((KV_LORA_RANK, MLA_HEADS * HEAD_COLS), const2),
            pl.BlockSpec((1, 2 * LANES), const2),
            pl.BlockSpec((1, 2 * LANES), const2),
            pl.BlockSpec((tm, LANES), lambda bi, ti: (ti, 0)),
            pl.BlockSpec((len(POOL_WINDOWS), POOL_GROUP_DIM, POOL_GROUP_DIM), lambda bi, ti: (0, 0, 0)),
            pl.BlockSpec((1, D_POOL), const2),
        ],
        out_specs=[
            pl.BlockSpec((1, MLA_HEADS, tm // Q_BLOCK, Q_BLOCK, QK_HEAD_DIM),
                         lambda bi, ti: (bi, 0, ti, 0, 0)),
            pl.BlockSpec((1, MLA_HEADS, tm, QK_HEAD_DIM), lambda bi, ti: (bi, 0, ti, 0)),
            pl.BlockSpec((1, MLA_HEADS, tm, V_AUG_DIM), lambda bi, ti: (bi, 0, ti, 0)),
            pl.BlockSpec((1, tm, D_ATTN), lambda bi, ti: (bi, ti, 0)),
            pl.BlockSpec((1, tm, D_POOL), lambda bi, ti: (bi, ti, 0)),
        ],
        out_shape=[
            jax.ShapeDtypeStruct((b, MLA_HEADS, s // Q_BLOCK, Q_BLOCK, QK_HEAD_DIM), BF16),
            jax.ShapeDtypeStruct((b, MLA_HEADS, s, QK_HEAD_DIM), BF16),
            jax.ShapeDtypeStruct((b, MLA_HEADS, s, V_AUG_DIM), BF16),
            jax.ShapeDtypeStruct((b, s, D_ATTN), BF16),
            jax.ShapeDtypeStruct((b, s, D_POOL), BF16),
        ],
        scratch_shapes=[
            pltpu.VMEM((tm + 2 * HALO, d), BF16),
        ],
        compiler_params=pltpu.CompilerParams(
            dimension_semantics=("parallel", "parallel"),
            vmem_limit_bytes=VMEM_LIMIT),
        name="proj",
    )(x, x, x, mod3, norm_g, w_in_r, qlg, w_uq_r, kvlg, w_ukv_b, gq, gk, cs, w_pool_b, ps)


def _ctxkv_kernel(x_ref, mod_ref, ng_ref, w_kv_ref, kvlg_ref, w_ukv_ref, gk_ref, cs_ref,
                  k_ref, v_ref):
    d = x_ref.shape[-1]
    shift = mod_ref[0, :, 0:d]
    scale = mod_ref[0, :, d:2 * d]
    a = ng_ref[...] * (1.0 + scale)
    for bi in range(x_ref.shape[0]):
        xv = x_ref[bi]
        ms = jnp.mean(xv * xv, axis=-1, keepdims=True)
        hc = (xv * lax.rsqrt(ms + NORM_EPS) * a + shift).astype(BF16)
        u = jnp.dot(hc, w_kv_ref[...], preferred_element_type=F32)
        _kv_heads(u[:, 0:KV_LORA_RANK], u[:, KV_LORA_RANK:], kvlg_ref[...], w_ukv_ref[...],
                  gk_ref, cs_ref[...], k_ref.at[pl.ds(bi, 1)], v_ref.at[pl.ds(bi, 1)],
                  slice(None))


def _ctxkv(ctx, mod3, norm_g, w_in_r, kvlg, w_ukv_b, gk, cs_ctx, *, bb):
    b, lc, d = ctx.shape
    const2 = lambda bi: (0, 0)
    nkv = C_GA - C_CKV
    assert C_CKV % nkv == 0
    return pl.pallas_call(
        _ctxkv_kernel,
        grid=(b // bb,),
        in_specs=[
            pl.BlockSpec((bb, lc, d), lambda bi: (bi, 0, 0)),
            pl.BlockSpec((1, 1, 3 * d), lambda bi: (b, 0, 0)),
            pl.BlockSpec((1, d), const2),
            pl.BlockSpec((d, nkv), lambda bi: (0, C_CKV // nkv)),
            pl.BlockSpec((1, KV_LORA_RANK), const2),
            pl.BlockSpec((KV_LORA_RANK, MLA_HEADS * HEAD_COLS), const2),
            pl.BlockSpec((1, 2 * LANES), const2),
            pl.BlockSpec((lc, LANES), const2),
        ],
        out_specs=[
            pl.BlockSpec((bb, MLA_HEADS, lc, QK_HEAD_DIM), lambda bi: (bi, 0, 0, 0)),
            pl.BlockSpec((bb, MLA_HEADS, lc, V_AUG_DIM), lambda bi: (bi, 0, 0, 0)),
        ],
        out_shape=[
            jax.ShapeDtypeStruct((b, MLA_HEADS, lc, QK_HEAD_DIM), BF16),
            jax.ShapeDtypeStruct((b, MLA_HEADS, lc, V_AUG_DIM), BF16),
        ],
        compiler_params=pltpu.CompilerParams(
            dimension_semantics=("parallel",), vmem_limit_bytes=VMEM_LIMIT),
        name="ctxkv",
    )(ctx, mod3, norm_g, w_in_r, kvlg, w_ukv_b, gk, cs_ctx)


def _lane_tile(a, n):
    return a if n == 1 else jnp.concatenate([a] * n, axis=1)


def _attn_out_kernel(q_ref, k_ref, v_ref, kc_ref, vc_ref, sga_ref, brp_ref, x_ref, mod_ref,
                     w_ref, shift_ref, o_ref, attn_scr, *stat_scr, tk, running_max):
    nb, ib, dk = q_ref.shape[2:]
    rows = nb * ib
    nk = k_ref.shape[2] // tk
    dv = V_HEAD_DIM
    d = x_ref.shape[-1]

    def keys(h, c):
        return kc_ref[0, h] if c == 0 else k_ref[0, h, (c - 1) * tk:c * tk, :]

    def values(h, c):
        return vc_ref[0, h] if c == 0 else v_ref[0, h, (c - 1) * tk:c * tk, :]

    def scores(h, c):
        q = q_ref[0, h].reshape(rows, dk)
        return lax.dot_general(q, keys(h, c), (((1,), (1,)), ((), ())),
                               preferred_element_type=F32)

    gate = mod_ref[0, :, 2 * d:3 * d]
    o_ref[0] = x_ref[0] + gate * jnp.dot(brp_ref[0], w_ref[D_ATTN:, :],
                                         preferred_element_type=F32)

    if running_max:
        m_scr, acc_scr = stat_scr

        def accumulate(h, s, vb):
            m_prev = m_scr[h]
            m_next = jnp.maximum(m_prev, jnp.max(s, axis=-1, keepdims=True))
            alpha = jnp.exp2(m_prev - m_next)
            p = jnp.exp2(s - _lane_tile(m_next, s.shape[1] // LANES))
            pv = jnp.dot(p.astype(BF16), vb, preferred_element_type=F32)
            acc_scr[h] = _lane_tile(alpha, V_AUG_DIM // LANES) * acc_scr[h] + pv
            m_scr[h] = m_next

        m_scr[...] = jnp.full(m_scr.shape, -jnp.inf, F32)
        acc_scr[...] = jnp.zeros(acc_scr.shape, F32)
        for c in range(nk + 1):
            for h in range(MLA_HEADS):
                accumulate(h, scores(h, c), values(h, c))
        for h in range(MLA_HEADS):
            attn_scr[h] = acc_scr[h, :, 0:dv] / acc_scr[h, :, dv:2 * dv]
    else:
        acc = [None] * MLA_HEADS
        for c in range(nk + 1):
            for h in range(MLA_HEADS):
                s = scores(h, c)
                p = jnp.exp2(s - shift_ref[:, 0:s.shape[1]]).astype(BF16)
                pv = jnp.dot(p, values(h, c), preferred_element_type=F32)
                acc[h] = pv if acc[h] is None else acc[h] + pv
        for h in range(MLA_HEADS):
            attn_scr[h] = acc[h][:, 0:dv] / acc[h][:, dv:2 * dv]
    pair = 2 * dv
    for h0 in range(0, MLA_HEADS, 2):
        attn = jnp.concatenate(
            [jnp.concatenate([attn_scr[h, pl.ds(il, nb, stride=ib), :] for il in range(ib)],
                             axis=0) for h in (h0, h0 + 1)], axis=1)
        cols = slice(h0 * dv, h0 * dv + pair)
        br_a = (sga_ref[0, :, cols].astype(F32) * attn).astype(BF16)
        o_ref[0] = o_ref[0] + gate * jnp.dot(br_a, w_ref[cols, :], preferred_element_type=F32)


def _attn_out(q5, k, v, kc, vc, sga, brp, x, mod3, w_out_b, shift, *, ib, tk, running_max):
    b, h, nb, _, dk = q5.shape
    _, s, d = x.shape
    lc = kc.shape[2]
    dva = v.shape[-1]
    rows = nb * ib
    assert lc <= tk and s % tk == 0
    kern = functools.partial(_attn_out_kernel, tk=tk, running_max=running_max)
    stat_scr = [pltpu.VMEM((h, rows, LANES), F32), pltpu.VMEM((h, rows, dva), F32)]
    resident = dict(pipeline_mode=pl.Buffered(1))
    row_tile = lambda bi, ti: (bi, ti, 0)
    return pl.pallas_call(
        kern,
        grid=(b, Q_BLOCK // ib),
        in_specs=[
            pl.BlockSpec((1, h, nb, ib, dk), lambda bi, ti: (bi, 0, 0, ti, 0)),
            pl.BlockSpec((1, h, s, dk), lambda bi, ti: (bi, 0, 0, 0)),
            pl.BlockSpec((1, h, s, dva), lambda bi, ti: (bi, 0, 0, 0)),
            pl.BlockSpec((1, h, lc, dk), lambda bi, ti: (bi, 0, 0, 0), **resident),
            pl.BlockSpec((1, h, lc, dva), lambda bi, ti: (bi, 0, 0, 0), **resident),
            pl.BlockSpec((1, rows, D_ATTN), row_tile),
            pl.BlockSpec((1, rows, D_POOL), row_tile),
            pl.BlockSpec((1, rows, d), row_tile),
            pl.BlockSpec((1, 1, 3 * d), lambda bi, ti: (bi, 0, 0)),
            pl.BlockSpec((D_ATTN + D_POOL, d), lambda bi, ti: (0, 0), **resident),
            pl.BlockSpec((1, tk), lambda bi, ti: (0, 0)),
        ],
        out_specs=pl.BlockSpec((1, rows, d), row_tile),
        out_shape=jax.ShapeDtypeStruct((b, s, d), F32),
        scratch_shapes=[pltpu.VMEM((h, rows, V_HEAD_DIM), F32)] + (stat_scr if running_max else []),
        compiler_params=pltpu.CompilerParams(
            dimension_semantics=("parallel", "arbitrary"),
            vmem_limit_bytes=VMEM_LIMIT),
        name="attn_out_rmax" if running_max else "attn_out",
    )(q5, k, v, kc, vc, sga, brp, x, mod3, w_out_b, shift)


def _rotate_half_cols(w, signed):
    q = QK_ROPE_DIM // 4
    a, b, c, d = (w[..., i * q:(i + 1) * q] for i in range(4))
    if signed:
        return jnp.concatenate([-b, a, -d, c], axis=-1)
    return jnp.concatenate([b, a, d, c], axis=-1)


def _rope_table(seq):
    rows = seq // GRID_W
    row = np.repeat(np.arange(rows, dtype=np.float32), GRID_W)
    col = np.tile(np.arange(GRID_W, dtype=np.float32), rows)
    n_freq = QK_ROPE_DIM // 4
    inv = np.float32(ROPE_BASE) ** (-np.arange(n_freq, dtype=np.float32) / np.float32(n_freq))
    ang_r = row[:, None] * inv
    ang_c = col[:, None] * inv
    ang = np.concatenate([ang_r, ang_r, ang_c, ang_c], axis=-1).astype(np.float64)
    table = np.concatenate([np.cos(ang), np.sin(ang)], axis=-1)
    return jnp.asarray(table, dtype=F32)


def kernel(x, c, ctx, c_ctx, w_mod, b_mod, norm_g, w_in, q_lora_g, w_uq, kv_lora_g, w_ukv,
           q_norm_g, k_norm_g, w_pool, pool_scale, w_out):
    b, s, d = x.shape
    lc = ctx.shape[1]
    depth = w_mod.shape[0]
    assert depth == 1, "single-layer block"
    assert b + 1 <= MOD_ROWS and s % PROJ_TM == 0 and s % ATTN_TK == 0
    assert s % Q_BLOCK == 0 and Q_BLOCK % ATTN_IB == 0 and s % GRID_W == 0
    l = 0

    cc = jnp.concatenate([c, c_ctx[None, :], jnp.zeros((MOD_ROWS - b - 1, d), F32)], axis=0)
    mod = _modulation(cc, w_mod, b_mod[l][None, :], l)
    mod3 = mod[:, None, :]

    wi = w_in.astype(BF16)[l]
    o_cq, o_ckv, o_kr = 0, Q_LORA_RANK, Q_LORA_RANK + KV_LORA_RANK
    o_ga = o_kr + QK_ROPE_DIM
    o_pin = o_ga + D_ATTN
    o_gp = o_pin + D_POOL
    w_kr = wi[:, o_kr:o_ga]
    w_in_r = jnp.concatenate(
        [wi[:, o_cq:o_kr], w_kr, _rotate_half_cols(w_kr, True), wi[:, o_ga:o_pin],
         wi[:, o_gp:], wi[:, o_pin:o_gp]], axis=1)
    wq = w_uq.astype(BF16)[l].reshape(Q_LORA_RANK, MLA_HEADS, QK_HEAD_DIM)
    wq_rope = wq[:, :, QK_NOPE_DIM:]
    w_uq_r = jnp.concatenate([wq, _rotate_half_cols(wq_rope, True)], axis=-1)
    w_uq_r = w_uq_r.reshape(Q_LORA_RANK, MLA_HEADS * HEAD_COLS)
    w_ukv_b = w_ukv.astype(BF16)[l]
    w_pool_b = w_pool.astype(BF16)[l]
    w_out_b = w_out.astype(BF16)[l]

    bound = (Q_SCALE * QK_HEAD_DIM) * jnp.max(jnp.abs(q_norm_g[l])) * jnp.max(jnp.abs(k_norm_g[l]))

    def gain_row(g, const):
        return (jnp.concatenate([g, _rotate_half_cols(g[QK_NOPE_DIM:], False)]) * const)[None, :]

    gq = gain_row(q_norm_g[l], Q_SCALE * math.sqrt(QK_HEAD_DIM))
    gk = gain_row(k_norm_g[l], math.sqrt(QK_HEAD_DIM))
    cs = _rope_table(s)
    cs_ctx = jnp.concatenate([jnp.ones((lc, QK_ROPE_DIM), F32),
                              jnp.zeros((lc, QK_ROPE_DIM), F32)], axis=1)

    ng = norm_g[l][None, :]
    qlg = q_lora_g[l][None, :]
    kvlg = kv_lora_g[l][None, :]
    ps = pool_scale[l][None, :]

    q, k, v, sga, brp = _proj(x, mod3, ng, w_in_r, qlg, w_uq_r, kvlg, w_ukv_b, gq, gk, cs,
                              w_pool_b, ps, tm=PROJ_TM, nsub=PROJ_NSUB)
    kc, vc = _ctxkv(ctx, mod3, ng, w_in_r, kvlg, w_ukv_b, gk, cs_ctx,
                    bb=CTX_BB if b % CTX_BB == 0 else 1)
    shift = jnp.full((1, ATTN_TK), bound, F32)
    args = (q, k, v, kc, vc, sga, brp, x, mod3, w_out_b, shift)
    attn_out = functools.partial(_attn_out, ib=ATTN_IB, tk=ATTN_TK)
    return lax.cond(2.0 * bound < MAX_SHIFT_RANGE,
                    lambda a: attn_out(*a, running_max=False),
                    lambda a: attn_out(*a, running_max=True), args)
```

```python
import functools
import math

import jax
import jax.numpy as jnp
import numpy as np
from jax import lax
from jax.experimental import pallas as pl
from jax.experimental.pallas import tpu as pltpu

F32 = jnp.float32
BF16 = jnp.bfloat16

GRID_W = 64
MLA_HEADS = 4
QK_NOPE_DIM = 128
QK_ROPE_DIM = 64
QK_HEAD_DIM = QK_NOPE_DIM + QK_ROPE_DIM
V_HEAD_DIM = 128
V_AUG_DIM = 2 * V_HEAD_DIM
MAX_SHIFT_RANGE = 110.0
Q_LORA_RANK = 256
KV_LORA_RANK = 128
ROPE_BASE = 10000.0
POOL_WINDOWS = (2, 4, 8, 16)
POOL_GROUP_DIM = 128
NORM_EPS = 1e-6
Q_BLOCK = 128
D_ATTN = MLA_HEADS * V_HEAD_DIM
D_POOL = len(POOL_WINDOWS) * POOL_GROUP_DIM

LANES = 128
HEAD_COLS = 2 * LANES
HALO = 16
EDGE = 8
assert max(POOL_WINDOWS) // 2 <= EDGE <= HALO
MOD_ROWS = 16

C_CQ = 0
C_CKV = C_CQ + Q_LORA_RANK
C_KR = C_CKV + KV_LORA_RANK
C_GA = C_KR + 2 * QK_ROPE_DIM
C_GP = C_GA + D_ATTN
C_PIN = C_GP + D_POOL
C_END = C_PIN + D_POOL

VMEM_LIMIT = 56 * 1024 * 1024
MOD_TN = 1536
CTX_BB = 4
PROJ_TM = 1024
PROJ_NSUB = 2
ATTN_IB = 16
ATTN_TK = 512

Q_SCALE = QK_HEAD_DIM ** -0.5 * math.log2(math.e)


def _silu(x):
    hx = 0.5 * x
    return hx + hx * jnp.tanh(hx)


def _window_sum(p, w):
    n = p.shape[0]
    up = lambda a, k: pltpu.roll(a, n - k, axis=0)
    down = lambda a, k: pltpu.roll(a, k, axis=0)
    if w == 2:
        return p + down(p, 1)
    acc = p + up(p, 1)
    span = 2
    while span < w // 2:
        acc = acc + up(acc, span)
        span *= 2
    return acc + down(acc, w // 2)


def _rms_rows(x, g):
    ms = jnp.mean(x * x, axis=-1, keepdims=True)
    return x * lax.rsqrt(ms + NORM_EPS) * g


def _mod_kernel(cc_ref, w_ref, b_ref, o_ref):
    a = _silu(cc_ref[...]).astype(BF16)
    o_ref[...] = jnp.dot(a, w_ref[0].astype(BF16), preferred_element_type=F32) + b_ref[...]


def _modulation(cc, w_mod, b_mod, layer):
    _, d, n = w_mod.shape
    tn = MOD_TN
    return pl.pallas_call(
        _mod_kernel,
        grid=(n // tn,),
        in_specs=[
            pl.BlockSpec((MOD_ROWS, d), lambda j: (0, 0)),
            pl.BlockSpec((1, d, tn), lambda j: (layer, 0, j)),
            pl.BlockSpec((1, tn), lambda j: (0, j)),
        ],
        out_specs=pl.BlockSpec((MOD_ROWS, tn), lambda j: (0, j)),
        out_shape=jax.ShapeDtypeStruct((MOD_ROWS, n), F32),
        name="mod",
    )(cc, w_mod, b_mod)


HEAD_SS_EPS = QK_HEAD_DIM * NORM_EPS


def _rope_pair(rr, csg):
    pr = rr * csg
    return pr + pltpu.roll(pr, QK_ROPE_DIM, axis=1)


def _half_rope_sq(rr):
    return 0.5 * (rr * rr)


def _kv_heads(ckv, kr2, kvlg, w_ukv, gk_ref, cs, k_ref, v_ref, rows):
    ckv_n = _rms_rows(ckv, kvlg).astype(BF16)
    kv = jnp.dot(ckv_n, w_ukv, preferred_element_type=F32)
    g_nope = gk_ref[:, 0:LANES]
    k_rope = _rope_pair(kr2, cs * gk_ref[:, LANES:2 * LANES])
    sq_rope = _half_rope_sq(kr2)
    for h in range(MLA_HEADS):
        kn = kv[:, h * HEAD_COLS:h * HEAD_COLS + LANES]
        vv = kv[:, h * HEAD_COLS + LANES:(h + 1) * HEAD_COLS]
        ss = jnp.sum(kn * kn + sq_rope, axis=-1, keepdims=True)
        r = lax.rsqrt(ss + HEAD_SS_EPS)
        k_ref[0, h, rows, 0:LANES] = (kn * r * g_nope).astype(BF16)
        k_ref[0, h, rows, LANES:] = (k_rope * r)[:, 0:QK_ROPE_DIM].astype(BF16)
        v_ref[0, h, rows, 0:V_HEAD_DIM] = vv.astype(BF16)
        v_ref[0, h, rows, V_HEAD_DIM:] = jnp.ones((vv.shape[0], V_AUG_DIM - V_HEAD_DIM), BF16)


def _proj_kernel(x_ref, xp_ref, xn_ref, mod_ref, ng_ref, w_in_ref, qlg_ref, w_uq_ref,
                 kvlg_ref, w_ukv_ref, gq_ref, gk_ref, cs_ref, w_pool_ref, ps_ref,
                 q_ref, k_ref, v_ref, sga_ref, brp_ref,
                 h_scr, *, tm, seq, nsub):
    t = pl.program_id(1)
    nt = pl.num_programs(1)
    d = x_ref.shape[-1]
    shift = mod_ref[0, :, 0:d]
    scale = mod_ref[0, :, d:2 * d]
    a = ng_ref[...] * (1.0 + scale)

    def hnorm(xv):
        ms = jnp.mean(xv * xv, axis=-1, keepdims=True)
        return xv * lax.rsqrt(ms + NORM_EPS) * a + shift

    nh = tm // nsub
    h_scr[0:HALO, :] = jnp.where(t > 0, hnorm(xp_ref[0]), 0.0).astype(BF16)
    for i in range(nsub):
        h_scr[HALO + i * nh:HALO + (i + 1) * nh, :] = hnorm(x_ref[0, i * nh:(i + 1) * nh, :]).astype(BF16)
    h_scr[HALO + tm:, :] = jnp.where(t < nt - 1, hnorm(xn_ref[0]), 0.0).astype(BF16)

    cs_all = cs_ref[...]
    gq_nope = gq_ref[:, 0:LANES]
    gq_rr = gq_ref[:, LANES:2 * LANES]
    assert (tm // nsub) % Q_BLOCK == 0

    def h_rows(i):
        return h_scr[HALO + i * nh:HALO + (i + 1) * nh, :]

    u_gate = []
    for i in range(nsub):
        rows = slice(i * nh, (i + 1) * nh)
        cs = cs_all[rows]
        u = jnp.dot(h_rows(i), w_in_ref[:, C_CQ:C_GP], preferred_element_type=F32)
        u_a = u[:, C_CQ:C_GA]
        u_gate.append(u[:, C_GA:C_GP])
        _kv_heads(u_a[:, C_CKV:C_KR], u_a[:, C_KR:C_GA], kvlg_ref[...], w_ukv_ref[...],
                  gk_ref, cs, k_ref, v_ref, rows)
        cq_n = _rms_rows(u_a[:, C_CQ:C_CKV], qlg_ref[...]).astype(BF16)
        qa = jnp.dot(cq_n, w_uq_ref[...], preferred_element_type=F32)
        csq = cs * gq_rr
        for h in range(MLA_HEADS):
            qn = qa[:, h * HEAD_COLS:h * HEAD_COLS + LANES]
            rr = qa[:, h * HEAD_COLS + LANES:(h + 1) * HEAD_COLS]
            ss = jnp.sum(qn * qn + _half_rope_sq(rr), axis=-1, keepdims=True)
            r = lax.rsqrt(ss + HEAD_SS_EPS)
            q_lo = (qn * r * gq_nope).astype(BF16)
            q_hi = (_rope_pair(rr, csq) * r)[:, 0:QK_ROPE_DIM].astype(BF16)
            for j in range(nh // Q_BLOCK):
                blk = i * (nh // Q_BLOCK) + j
                q_ref[0, h, blk, :, 0:LANES] = q_lo[j * Q_BLOCK:(j + 1) * Q_BLOCK]
                q_ref[0, h, blk, :, LANES:] = q_hi[j * Q_BLOCK:(j + 1) * Q_BLOCK]

    for i in range(nsub):
        sga_ref[0, i * nh:(i + 1) * nh, :] = _silu(u_gate[i]).astype(BF16)

    for i in range(nsub):
        rows = slice(i * nh, (i + 1) * nh)
        pin = jnp.dot(h_scr[i * nh:(i + 1) * nh + 2 * HALO, :], w_in_ref[:, C_PIN:C_END],
                      preferred_element_type=F32)
        u_gp = jnp.dot(h_rows(i), w_in_ref[:, C_GP:C_PIN], preferred_element_type=F32)
        tok_head = t * tm + i * nh + lax.broadcasted_iota(jnp.int32, (EDGE, LANES), 0)
        tok_tail = tok_head + (nh - EDGE)
        for g, w in enumerate(POOL_WINDOWS):
            c0 = g * POOL_GROUP_DIM
            c1 = c0 + POOL_GROUP_DIM
            pg = pin[:, c0:c1]
            ws = _window_sum(pg, w)[HALO:HALO + nh]

            def inv_count(tok):
                lo = jnp.maximum(tok - w // 2, 0)
                hi = jnp.minimum(tok - w // 2 + w, seq)
                return 1.0 / (hi - lo).astype(F32)

            inv_cnt = jnp.concatenate(
                [inv_count(tok_head), jnp.full((nh - 2 * EDGE, LANES), 1.0 / w, F32),
                 inv_count(tok_tail)], axis=0)
            pooled = ws * inv_cnt - pg[HALO:HALO + nh]
            y = jnp.dot(pooled.astype(BF16), w_pool_ref[g], preferred_element_type=F32)
            brp_ref[0, rows, c0:c1] = (_silu(u_gp[:, c0:c1]) * (y * ps_ref[:, c0:c1])).astype(BF16)


def _proj(x, mod3, norm_g, w_in_r, qlg, w_uq_r, kvlg, w_ukv_b, gq, gk, cs, w_pool_b, ps, *,
          tm, nsub):
    b, s, d = x.shape
    nt = s // tm
    hb = tm // HALO
    nhb = s // HALO
    const2 = lambda bi, ti: (0, 0)
    kern = functools.partial(_proj_kernel, tm=tm, seq=s, nsub=nsub)
    return pl.pallas_call(
        kern,
        grid=(b, nt),
        in_specs=[
            pl.BlockSpec((1, tm, d), lambda bi, ti: (bi, ti, 0)),
            pl.BlockSpec((1, HALO, d), lambda bi, ti: (bi, jnp.maximum(ti * hb - 1, 0), 0)),
            pl.BlockSpec((1, HALO, d), lambda bi, ti: (bi, jnp.minimum((ti + 1) * hb, nhb - 1), 0)),
            pl.BlockSpec((1, 1, 3 * d), lambda bi, ti: (bi, 0, 0)),
            pl.BlockSpec((1, d), const2),
            pl.BlockSpec((d, C_END), const2),
            pl.BlockSpec((1, Q_LORA_RANK), const2),
            pl.BlockSpec((Q_LORA_RANK, MLA_HEADS * HEAD_COLS), const2),
            pl.BlockSpec((1, KV_LORA_RANK), const2),
            pl.BlockSpec((KV_LORA_RANK, MLA_HEADS * HEAD_COLS), const2),
            pl.BlockSpec((1, 2 * LANES), const2),
            pl.BlockSpec((1, 2 * LANES), const2),
            pl.BlockSpec((tm, LANES), lambda bi, ti: (ti, 0)),
            pl.BlockSpec((len(POOL_WINDOWS), POOL_GROUP_DIM, POOL_GROUP_DIM), lambda bi, ti: (0, 0, 0)),
            pl.BlockSpec((1, D_POOL), const2),
        ],
        out_specs=[
            pl.BlockSpec((1, MLA_HEADS, tm // Q_BLOCK, Q_BLOCK, QK_HEAD_DIM),
                         lambda bi, ti: (bi, 0, ti, 0, 0)),
            pl.BlockSpec((1, MLA_HEADS, tm, QK_HEAD_DIM), lambda bi, ti: (bi, 0, ti, 0)),
            pl.BlockSpec((1, MLA_HEADS, tm, V_AUG_DIM), lambda bi, ti: (bi, 0, ti, 0)),
            pl.BlockSpec((1, tm, D_ATTN), lambda bi, ti: (bi, ti, 0)),
            pl.BlockSpec((1, tm, D_POOL), lambda bi, ti: (bi, ti, 0)),
        ],
        out_shape=[
            jax.ShapeDtypeStruct((b, MLA_HEADS, s // Q_BLOCK, Q_BLOCK, QK_HEAD_DIM), BF16),
            jax.ShapeDtypeStruct((b, MLA_HEADS, s, QK_HEAD_DIM), BF16),
            jax.ShapeDtypeStruct((b, MLA_HEADS, s, V_AUG_DIM), BF16),
            jax.ShapeDtypeStruct((b, s, D_ATTN), BF16),
            jax.ShapeDtypeStruct((b, s, D_POOL), BF16),
        ],
        scratch_shapes=[
            pltpu.VMEM((tm + 2 * HALO, d), BF16),
        ],
        compiler_params=pltpu.CompilerParams(
            dimension_semantics=("parallel", "parallel"),
            vmem_limit_bytes=VMEM_LIMIT),
        name="proj",
    )(x, x, x, mod3, norm_g, w_in_r, qlg, w_uq_r, kvlg, w_ukv_b, gq, gk, cs, w_pool_b, ps)


def _ctxkv_kernel(x_ref, mod_ref, ng_ref, w_kv_ref, kvlg_ref, w_ukv_ref, gk_ref, cs_ref,
                  k_ref, v_ref):
    d = x_ref.shape[-1]
    shift = mod_ref[0, :, 0:d]
    scale = mod_ref[0, :, d:2 * d]
    a = ng_ref[...] * (1.0 + scale)
    for bi in range(x_ref.shape[0]):
        xv = x_ref[bi]
        ms = jnp.mean(xv * xv, axis=-1, keepdims=True)
        hc = (xv * lax.rsqrt(ms + NORM_EPS) * a + shift).astype(BF16)
        u = jnp.dot(hc, w_kv_ref[...], preferred_element_type=F32)
        _kv_heads(u[:, 0:KV_LORA_RANK], u[:, KV_LORA_RANK:], kvlg_ref[...], w_ukv_ref[...],
                  gk_ref, cs_ref[...], k_ref.at[pl.ds(bi, 1)], v_ref.at[pl.ds(bi, 1)],
                  slice(None))


def _ctxkv(ctx, mod3, norm_g, w_in_r, kvlg, w_ukv_b, gk, cs_ctx, *, bb):
    b, lc, d = ctx.shape
    const2 = lambda bi: (0, 0)
    nkv = C_GA - C_CKV
    assert C_CKV % nkv == 0
    return pl.pallas_call(
        _ctxkv_kernel,
        grid=(b // bb,),
        in_specs=[
            pl.BlockSpec((bb, lc, d), lambda bi: (bi, 0, 0)),
            pl.BlockSpec((1, 1, 3 * d), lambda bi: (b, 0, 0)),
            pl.BlockSpec((1, d), const2),
            pl.BlockSpec((d, nkv), lambda bi: (0, C_CKV // nkv)),
            pl.BlockSpec((1, KV_LORA_RANK), const2),
            pl.BlockSpec((KV_LORA_RANK, MLA_HEADS * HEAD_COLS), const2),
            pl.BlockSpec((1, 2 * LANES), const2),
            pl.BlockSpec((lc, LANES), const2),
        ],
        out_specs=[
            pl.BlockSpec((bb, MLA_HEADS, lc, QK_HEAD_DIM), lambda bi: (bi, 0, 0, 0)),
            pl.BlockSpec((bb, MLA_HEADS, lc, V_AUG_DIM), lambda bi: (bi, 0, 0, 0)),
        ],
        out_shape=[
            jax.ShapeDtypeStruct((b, MLA_HEADS, lc, QK_HEAD_DIM), BF16),
            jax.ShapeDtypeStruct((b, MLA_HEADS, lc, V_AUG_DIM), BF16),
        ],
        compiler_params=pltpu.CompilerParams(
            dimension_semantics=("parallel",), vmem_limit_bytes=VMEM_LIMIT),
        name="ctxkv",
    )(ctx, mod3, norm_g, w_in_r, kvlg, w_ukv_b, gk, cs_ctx)


def _lane_tile(a, n):
    return a if n == 1 else jnp.concatenate([a] * n, axis=1)


def _attn_out_kernel(q_ref, k_ref, v_ref, kc_ref, vc_ref, sga_ref, brp_ref, x_ref, mod_ref,
                     w_ref, shift_ref, o_ref, attn_scr, *stat_scr, tk, running_max):
    nb, ib, dk = q_ref.shape[2:]
    rows = nb * ib
    nk = k_ref.shape[2] // tk
    dv = V_HEAD_DIM
    d = x_ref.shape[-1]

    def keys(h, c):
        return kc_ref[0, h] if c == 0 else k_ref[0, h, (c - 1) * tk:c * tk, :]

    def values(h, c):
        return vc_ref[0, h] if c == 0 else v_ref[0, h, (c - 1) * tk:c * tk, :]

    def scores(h, c):
        q = q_ref[0, h].reshape(rows, dk)
        return lax.dot_general(q, keys(h, c), (((1,), (1,)), ((), ())),
                               preferred_element_type=F32)

    gate = mod_ref[0, :, 2 * d:3 * d]
    o_ref[0] = x_ref[0] + gate * jnp.dot(brp_ref[0], w_ref[D_ATTN:, :],
                                         preferred_element_type=F32)

    if running_max:
        m_scr, acc_scr = stat_scr

        def accumulate(h, s, vb):
            m_prev = m_scr[h]
            m_next = jnp.maximum(m_prev, jnp.max(s, axis=-1, keepdims=True))
            alpha = jnp.exp2(m_prev - m_next)
            p = jnp.exp2(s - _lane_tile(m_next, s.shape[1] // LANES))
            pv = jnp.dot(p.astype(BF16), vb, preferred_element_type=F32)
            acc_scr[h] = _lane_tile(alpha, V_AUG_DIM // LANES) * acc_scr[h] + pv
            m_scr[h] = m_next

        m_scr[...] = jnp.full(m_scr.shape, -jnp.inf, F32)
        acc_scr[...] = jnp.zeros(acc_scr.shape, F32)
        for c in range(nk + 1):
            for h in range(MLA_HEADS):
                accumulate(h, scores(h, c), values(h, c))
        for h in range(MLA_HEADS):
            attn_scr[h] = acc_scr[h, :, 0:dv] / acc_scr[h, :, dv:2 * dv]
    else:
        acc = [None] * MLA_HEADS
        for c in range(nk + 1):
            for h in range(MLA_HEADS):
                s = scores(h, c)
                p = jnp.exp2(s - shift_ref[:, 0:s.shape[1]]).astype(BF16)
                pv = jnp.dot(p, values(h, c), preferred_element_type=F32)
                acc[h] = pv if acc[h] is None else acc[h] + pv
        for h in range(MLA_HEADS):
            attn_scr[h] = acc[h][:, 0:dv] / acc[h][:, dv:2 * dv]
    pair = 2 * dv
    for h0 in range(0, MLA_HEADS, 2):
        attn = jnp.concatenate(
            [jnp.concatenate([attn_scr[h, pl.ds(il, nb, stride=ib), :] for il in range(ib)],
                             axis=0) for h in (h0, h0 + 1)], axis=1)
        cols = slice(h0 * dv, h0 * dv + pair)
        br_a = (sga_ref[0, :, cols].astype(F32) * attn).astype(BF16)
        o_ref[0] = o_ref[0] + gate * jnp.dot(br_a, w_ref[cols, :], preferred_element_type=F32)


def _attn_out(q5, k, v, kc, vc, sga, brp, x, mod3, w_out_b, shift, *, ib, tk, running_max):
    b, h, nb, _, dk = q5.shape
    _, s, d = x.shape
    lc = kc.shape[2]
    dva = v.shape[-1]
    rows = nb * ib
    assert lc <= tk and s % tk == 0
    kern = functools.partial(_attn_out_kernel, tk=tk, running_max=running_max)
    stat_scr = [pltpu.VMEM((h, rows, LANES), F32), pltpu.VMEM((h, rows, dva), F32)]
    resident = dict(pipeline_mode=pl.Buffered(1))
    row_tile = lambda bi, ti: (bi, ti, 0)
    return pl.pallas_call(
        kern,
        grid=(b, Q_BLOCK // ib),
        in_specs=[
            pl.BlockSpec((1, h, nb, ib, dk), lambda bi, ti: (bi, 0, 0, ti, 0)),
            pl.BlockSpec((1, h, s, dk), lambda bi, ti: (bi, 0, 0, 0)),
            pl.BlockSpec((1, h, s, dva), lambda bi, ti: (bi, 0, 0, 0)),
            pl.BlockSpec((1, h, lc, dk), lambda bi, ti: (bi, 0, 0, 0), **resident),
            pl.BlockSpec((1, h, lc, dva), lambda bi, ti: (bi, 0, 0, 0), **resident),
            pl.BlockSpec((1, rows, D_ATTN), row_tile),
            pl.BlockSpec((1, rows, D_POOL), row_tile),
            pl.BlockSpec((1, rows, d), row_tile),
            pl.BlockSpec((1, 1, 3 * d), lambda bi, ti: (bi, 0, 0)),
            pl.BlockSpec((D_ATTN + D_POOL, d), lambda bi, ti: (0, 0), **resident),
            pl.BlockSpec((1, tk), lambda bi, ti: (0, 0)),
        ],
        out_specs=pl.BlockSpec((1, rows, d), row_tile),
        out_shape=jax.ShapeDtypeStruct((b, s, d), F32),
        scratch_shapes=[pltpu.VMEM((h, rows, V_HEAD_DIM), F32)] + (stat_scr if running_max else []),
        compiler_params=pltpu.CompilerParams(
            dimension_semantics=("parallel", "arbitrary"),
            vmem_limit_bytes=VMEM_LIMIT),
        name="attn_out_rmax" if running_max else "attn_out",
    )(q5, k, v, kc, vc, sga, brp, x, mod3, w_out_b, shift)


def _rotate_half_cols(w, signed):
    q = QK_ROPE_DIM // 4
    a, b, c, d = (w[..., i * q:(i + 1) * q] for i in range(4))
    if signed:
        return jnp.concatenate([-b, a, -d, c], axis=-1)
    return jnp.concatenate([b, a, d, c], axis=-1)


def _rope_table(seq):
    rows = seq // GRID_W
    row = np.repeat(np.arange(rows, dtype=np.float32), GRID_W)
    col = np.tile(np.arange(GRID_W, dtype=np.float32), rows)
    n_freq = QK_ROPE_DIM // 4
    inv = np.float32(ROPE_BASE) ** (-np.arange(n_freq, dtype=np.float32) / np.float32(n_freq))
    ang_r = row[:, None] * inv
    ang_c = col[:, None] * inv
    ang = np.concatenate([ang_r, ang_r, ang_c, ang_c], axis=-1).astype(np.float64)
    table = np.concatenate([np.cos(ang), np.sin(ang)], axis=-1)
    return jnp.asarray(table, dtype=F32)


def kernel(x, c, ctx, c_ctx, w_mod, b_mod, norm_g, w_in, q_lora_g, w_uq, kv_lora_g, w_ukv,
           q_norm_g, k_norm_g, w_pool, pool_scale, w_out):
    b, s, d = x.shape
    lc = ctx.shape[1]
    depth = w_mod.shape[0]
    assert depth == 1, "single-layer block"
    assert b + 1 <= MOD_ROWS and s % PROJ_TM == 0 and s % ATTN_TK == 0
    assert s % Q_BLOCK == 0 and Q_BLOCK % ATTN_IB == 0 and s % GRID_W == 0
    l = 0

    cc = jnp.concatenate([c, c_ctx[None, :], jnp.zeros((MOD_ROWS - b - 1, d), F32)], axis=0)
    mod = _modulation(cc, w_mod, b_mod[l][None, :], l)
    mod3 = mod[:, None, :]

    wi = w_in.astype(BF16)[l]
    o_cq, o_ckv, o_kr = 0, Q_LORA_RANK, Q_LORA_RANK + KV_LORA_RANK
    o_ga = o_kr + QK_ROPE_DIM
    o_pin = o_ga + D_ATTN
    o_gp = o_pin + D_POOL
    w_kr = wi[:, o_kr:o_ga]
    w_in_r = jnp.concatenate(
        [wi[:, o_cq:o_kr], w_kr, _rotate_half_cols(w_kr, True), wi[:, o_ga:o_pin],
         wi[:, o_gp:], wi[:, o_pin:o_gp]], axis=1)
    wq = w_uq.astype(BF16)[l].reshape(Q_LORA_RANK, MLA_HEADS, QK_HEAD_DIM)
    wq_rope = wq[:, :, QK_NOPE_DIM:]
    w_uq_r = jnp.concatenate([wq, _rotate_half_cols(wq_rope, True)], axis=-1)
    w_uq_r = w_uq_r.reshape(Q_LORA_RANK, MLA_HEADS * HEAD_COLS)
    w_ukv_b = w_ukv.astype(BF16)[l]
    w_pool_b = w_pool.astype(BF16)[l]
    w_out_b = w_out.astype(BF16)[l]

    bound = (Q_SCALE * QK_HEAD_DIM) * jnp.max(jnp.abs(q_norm_g[l])) * jnp.max(jnp.abs(k_norm_g[l]))

    def gain_row(g, const):
        return (jnp.concatenate([g, _rotate_half_cols(g[QK_NOPE_DIM:], False)]) * const)[None, :]

    gq = gain_row(q_norm_g[l], Q_SCALE * math.sqrt(QK_HEAD_DIM))
    gk = gain_row(k_norm_g[l], math.sqrt(QK_HEAD_DIM))
    cs = _rope_table(s)
    cs_ctx = jnp.concatenate([jnp.ones((lc, QK_ROPE_DIM), F32),
                              jnp.zeros((lc, QK_ROPE_DIM), F32)], axis=1)

    ng = norm_g[l][None, :]
    qlg = q_lora_g[l][None, :]
    kvlg = kv_lora_g[l][None, :]
    ps = pool_scale[l][None, :]

    q, k, v, sga, brp = _proj(x, mod3, ng, w_in_r, qlg, w_uq_r, kvlg, w_ukv_b, gq, gk, cs,
                              w_pool_b, ps, tm=PROJ_TM, nsub=PROJ_NSUB)
    kc, vc = _ctxkv(ctx, mod3, ng, w_in_r, kvlg, w_ukv_b, gk, cs_ctx,
                    bb=CTX_BB if b % CTX_BB == 0 else 1)
    shift = jnp.full((1, ATTN_TK), bound, F32)
    args = (q, k, v, kc, vc, sga, brp, x, mod3, w_out_b, shift)
    attn_out = functools.partial(_attn_out, ib=ATTN_IB, tk=ATTN_TK)
    return lax.cond(2.0 * bound < MAX_SHIFT_RANGE,
                    lambda a: attn_out(*a, running_max=False),
                    lambda a: attn_out(*a, running_max=True), args)
```

```python
import functools
import math

import jax
import jax.numpy as jnp
import numpy as np
from jax import lax
from jax.experimental import pallas as pl
from jax.experimental.pallas import tpu as pltpu

F32 = jnp.float32
BF16 = jnp.bfloat16

GRID_W = 64
MLA_HEADS = 4
QK_NOPE_DIM = 128
QK_ROPE_DIM = 64
QK_HEAD_DIM = QK_NOPE_DIM + QK_ROPE_DIM
V_HEAD_DIM = 128
MAX_PLAIN_SCORE = 55.0
Q_LORA_RANK = 256
KV_LORA_RANK = 128
ROPE_BASE = 10000.0
POOL_WINDOWS = (2, 4, 8, 16)
POOL_GROUP_DIM = 128
NORM_EPS = 1e-6
Q_BLOCK = 128
D_ATTN = MLA_HEADS * V_HEAD_DIM
D_POOL = len(POOL_WINDOWS) * POOL_GROUP_DIM

LANES = 128
HEAD_COLS = 2 * LANES
HALO = 16
EDGE = 8
assert max(POOL_WINDOWS) // 2 <= EDGE <= HALO
MOD_ROWS = 16

C_CQ = 0
C_CKV = C_CQ + Q_LORA_RANK
C_KR = C_CKV + KV_LORA_RANK
C_GA = C_KR + 2 * QK_ROPE_DIM
C_GP = C_GA + D_ATTN
C_PIN = C_GP + D_POOL
C_END = C_PIN + D_POOL

VMEM_LIMIT = 56 * 1024 * 1024
MOD_TN = 1536
CTX_BB = 4
PROJ_TM = 1024
PROJ_NSUB = 2
ATTN_IB = 16
ATTN_TK = 512

Q_SCALE = QK_HEAD_DIM ** -0.5 * math.log2(math.e)


def _silu(x):
    hx = 0.5 * x
    return hx + hx * jnp.tanh(hx)


def _window_sum(p, w):
    n = p.shape[0]
    up = lambda a, k: pltpu.roll(a, n - k, axis=0)
    down = lambda a, k: pltpu.roll(a, k, axis=0)
    if w == 2:
        return p + down(p, 1)
    acc = p + up(p, 1)
    span = 2
    while span < w // 2:
        acc = acc + up(acc, span)
        span *= 2
    return acc + down(acc, w // 2)


def _rms_rows(x, g):
    ms = jnp.mean(x * x, axis=-1, keepdims=True)
    return x * lax.rsqrt(ms + NORM_EPS) * g


def _mod_kernel(cc_ref, w_ref, b_ref, o_ref):
    a = _silu(cc_ref[...]).astype(BF16)
    o_ref[...] = jnp.dot(a, w_ref[0].astype(BF16), preferred_element_type=F32) + b_ref[...]


def _modulation(cc, w_mod, b_mod, layer):
    _, d, n = w_mod.shape
    tn = MOD_TN
    return pl.pallas_call(
        _mod_kernel,
        grid=(n // tn,),
        in_specs=[
            pl.BlockSpec((MOD_ROWS, d), lambda j: (0, 0)),
            pl.BlockSpec((1, d, tn), lambda j: (layer, 0, j)),
            pl.BlockSpec((1, tn), lambda j: (0, j)),
        ],
        out_specs=pl.BlockSpec((MOD_ROWS, tn), lambda j: (0, j)),
        out_shape=jax.ShapeDtypeStruct((MOD_ROWS, n), F32),
        name="mod",
    )(cc, w_mod, b_mod)


HEAD_SS_EPS = QK_HEAD_DIM * NORM_EPS


def _rope_pair(rr, csg):
    pr = rr * csg
    return pr + pltpu.roll(pr, QK_ROPE_DIM, axis=1)


def _half_rope_sq(rr):
    return 0.5 * (rr * rr)


def _kv_heads(ckv, kr2, kvlg, w_ukv, gk_ref, cs, k_ref, v_ref, rows):
    ckv_n = _rms_rows(ckv, kvlg).astype(BF16)
    kv = jnp.dot(ckv_n, w_ukv, preferred_element_type=F32)
    g_nope = gk_ref[:, 0:LANES]
    k_rope = _rope_pair(kr2, cs * gk_ref[:, LANES:2 * LANES])
    sq_rope = _half_rope_sq(kr2)
    for h in range(MLA_HEADS):
        kn = kv[:, h * HEAD_COLS:h * HEAD_COLS + LANES]
        vv = kv[:, h * HEAD_COLS + LANES:(h + 1) * HEAD_COLS]
        ss = jnp.sum(kn * kn + sq_rope, axis=-1, keepdims=True)
        r = lax.rsqrt(ss + HEAD_SS_EPS)
        k_ref[0, h, rows, 0:LANES] = (kn * r * g_nope).astype(BF16)
        k_ref[0, h, rows, LANES:] = (k_rope * r)[:, 0:QK_ROPE_DIM].astype(BF16)
        v_ref[0, h, rows, :] = vv.astype(BF16)


def _proj_kernel(x_ref, xp_ref, xn_ref, mod_ref, ng_ref, w_in_ref, qlg_ref, w_uq_ref,
                 kvlg_ref, w_ukv_ref, gq_ref, gk_ref, cs_ref, w_pool_ref, ps_ref,
                 q_ref, k_ref, v_ref, sga_ref, brp_ref,
                 h_scr, *, tm, seq, nsub):
    t = pl.program_id(1)
    nt = pl.num_programs(1)
    d = x_ref.shape[-1]
    shift = mod_ref[0, :, 0:d]
    scale = mod_ref[0, :, d:2 * d]
    a = ng_ref[...] * (1.0 + scale)

    def hnorm(xv):
        ms = jnp.mean(xv * xv, axis=-1, keepdims=True)
        return xv * lax.rsqrt(ms + NORM_EPS) * a + shift

    nh = tm // nsub
    h_scr[0:HALO, :] = jnp.where(t > 0, hnorm(xp_ref[0]), 0.0).astype(BF16)
    for i in range(nsub):
        h_scr[HALO + i * nh:HALO + (i + 1) * nh, :] = hnorm(x_ref[0, i * nh:(i + 1) * nh, :]).astype(BF16)
    h_scr[HALO + tm:, :] = jnp.where(t < nt - 1, hnorm(xn_ref[0]), 0.0).astype(BF16)

    gq_nope = gq_ref[:, 0:LANES]
    gq_rr = gq_ref[:, LANES:2 * LANES]
    assert (tm // nsub) % Q_BLOCK == 0

    def h_rows(i):
        return h_scr[HALO + i * nh:HALO + (i + 1) * nh, :]

    for i in range(nsub):
        rows = slice(i * nh, (i + 1) * nh)
        cs = cs_ref[rows, :]
        u = jnp.dot(h_rows(i), w_in_ref[:, C_CQ:C_GP], preferred_element_type=F32)
        u_a = u[:, C_CQ:C_GA]
        sga_ref[0, rows, :] = _silu(u[:, C_GA:C_GP]).astype(BF16)
        _kv_heads(u_a[:, C_CKV:C_KR], u_a[:, C_KR:C_GA], kvlg_ref[...], w_ukv_ref[...],
                  gk_ref, cs, k_ref, v_ref, rows)
        cq_n = _rms_rows(u_a[:, C_CQ:C_CKV], qlg_ref[...]).astype(BF16)
        qa = jnp.dot(cq_n, w_uq_ref[...], preferred_element_type=F32)
        csq = cs * gq_rr
        for h in range(MLA_HEADS):
            qn = qa[:, h * HEAD_COLS:h * HEAD_COLS + LANES]
            rr = qa[:, h * HEAD_COLS + LANES:(h + 1) * HEAD_COLS]
            ss = jnp.sum(qn * qn + _half_rope_sq(rr), axis=-1, keepdims=True)
            r = lax.rsqrt(ss + HEAD_SS_EPS)
            q_lo = (qn * r * gq_nope).astype(BF16)
            q_hi = (_rope_pair(rr, csq) * r)[:, 0:QK_ROPE_DIM].astype(BF16)
            for j in range(nh // Q_BLOCK):
                blk = i * (nh // Q_BLOCK) + j
                q_ref[0, h, blk, :, 0:LANES] = q_lo[j * Q_BLOCK:(j + 1) * Q_BLOCK]
                q_ref[0, h, blk, :, LANES:] = q_hi[j * Q_BLOCK:(j + 1) * Q_BLOCK]

    for i in range(nsub):
        rows = slice(i * nh, (i + 1) * nh)
        pin = jnp.dot(h_scr[i * nh:(i + 1) * nh + 2 * HALO, :], w_in_ref[:, C_PIN:C_END],
                      preferred_element_type=F32)
        u_gp = jnp.dot(h_rows(i), w_in_ref[:, C_GP:C_PIN], preferred_element_type=F32)
        tok_head = t * tm + i * nh + lax.broadcasted_iota(jnp.int32, (EDGE, LANES), 0)
        tok_tail = tok_head + (nh - EDGE)
        for g, w in enumerate(POOL_WINDOWS):
            c0 = g * POOL_GROUP_DIM
            c1 = c0 + POOL_GROUP_DIM
            pg = pin[:, c0:c1]
            ws = _window_sum(pg, w)[HALO:HALO + nh]

            def inv_count(tok):
                lo = jnp.maximum(tok - w // 2, 0)
                hi = jnp.minimum(tok - w // 2 + w, seq)
                return 1.0 / (hi - lo).astype(F32)

            inv_cnt = jnp.concatenate(
                [inv_count(tok_head), jnp.full((nh - 2 * EDGE, LANES), 1.0 / w, F32),
                 inv_count(tok_tail)], axis=0)
            pooled = ws * inv_cnt - pg[HALO:HALO + nh]
            y = jnp.dot(pooled.astype(BF16), w_pool_ref[g], preferred_element_type=F32)
            brp_ref[0, rows, c0:c1] = (_silu(u_gp[:, c0:c1]) * (y * ps_ref[:, c0:c1])).astype(BF16)


def _proj(x, mod3, norm_g, w_in_r, qlg, w_uq_r, kvlg, w_ukv_b, gq, gk, cs, w_pool_b, ps, *,
          tm, nsub):
    b, s, d = x.shape
    nt = s // tm
    hb = tm // HALO
    nhb = s // HALO
    const2 = lambda bi, ti: (0, 0)
    kern = functools.partial(_proj_kernel, tm=tm, seq=s, nsub=nsub)
    return pl.pallas_call(
        kern,
        grid=(b, nt),
        in_specs=[
            pl.BlockSpec((1, tm, d), lambda bi, ti: (bi, ti, 0)),
            pl.BlockSpec((1, HALO, d), lambda bi, ti: (bi, jnp.maximum(ti * hb - 1, 0), 0)),
            pl.BlockSpec((1, HALO, d), lambda bi, ti: (bi, jnp.minimum((ti + 1) * hb, nhb - 1), 0)),
            pl.BlockSpec((1, 1, 3 * d), lambda bi, ti: (bi, 0, 0)),
            pl.BlockSpec((1, d), const2),
            pl.BlockSpec((d, C_END), const2),
            pl.BlockSpec((1, Q_LORA_RANK), const2),
            pl.BlockSpec((Q_LORA_RANK, MLA_HEADS * HEAD_COLS), const2),
            pl.BlockSpec((1, KV_LORA_RANK), const2),
            pl.BlockSpec((KV_LORA_RANK, MLA_HEADS * HEAD_COLS), const2),
            pl.BlockSpec((1, 2 * LANES), const2),
            pl.BlockSpec((1, 2 * LANES), const2),
            pl.BlockSpec((tm, LANES), lambda bi, ti: (ti, 0)),
            pl.BlockSpec((len(POOL_WINDOWS), POOL_GROUP_DIM, POOL_GROUP_DIM), lambda bi, ti: (0, 0, 0)),
            pl.BlockSpec((1, D_POOL), const2),
        ],
        out_specs=[
            pl.BlockSpec((1, MLA_HEADS, tm // Q_BLOCK, Q_BLOCK, QK_HEAD_DIM),
                         lambda bi, ti: (bi, 0, ti, 0, 0)),
            pl.BlockSpec((1, MLA_HEADS, tm, QK_HEAD_DIM), lambda bi, ti: (bi, 0, ti, 0)),
            pl.BlockSpec((1, MLA_HEADS, tm, V_HEAD_DIM), lambda bi, ti: (bi, 0, ti, 0)),
            pl.BlockSpec((1, tm, D_ATTN), lambda bi, ti: (bi, ti, 0)),
            pl.BlockSpec((1, tm, D_POOL), lambda bi, ti: (bi, ti, 0)),
        ],
        out_shape=[
            jax.ShapeDtypeStruct((b, MLA_HEADS, s // Q_BLOCK, Q_BLOCK, QK_HEAD_DIM), BF16),
            jax.ShapeDtypeStruct((b, MLA_HEADS, s, QK_HEAD_DIM), BF16),
            jax.ShapeDtypeStruct((b, MLA_HEADS, s, V_HEAD_DIM), BF16),
            jax.ShapeDtypeStruct((b, s, D_ATTN), BF16),
            jax.ShapeDtypeStruct((b, s, D_POOL), BF16),
        ],
        scratch_shapes=[
            pltpu.VMEM((tm + 2 * HALO, d), BF16),
        ],
        compiler_params=pltpu.CompilerParams(
            dimension_semantics=("parallel", "parallel"),
            vmem_limit_bytes=VMEM_LIMIT),
        name="proj",
    )(x, x, x, mod3, norm_g, w_in_r, qlg, w_uq_r, kvlg, w_ukv_b, gq, gk, cs, w_pool_b, ps)


def _ctxkv_kernel(x_ref, mod_ref, ng_ref, w_kv_ref, kvlg_ref, w_ukv_ref, gk_ref, cs_ref,
                  k_ref, v_ref):
    d = x_ref.shape[-1]
    shift = mod_ref[0, :, 0:d]
    scale = mod_ref[0, :, d:2 * d]
    a = ng_ref[...] * (1.0 + scale)
    for bi in range(x_ref.shape[0]):
        xv = x_ref[bi]
        ms = jnp.mean(xv * xv, axis=-1, keepdims=True)
        hc = (xv * lax.rsqrt(ms + NORM_EPS) * a + shift).astype(BF16)
        u = jnp.dot(hc, w_kv_ref[...], preferred_element_type=F32)
        _kv_heads(u[:, 0:KV_LORA_RANK], u[:, KV_LORA_RANK:], kvlg_ref[...], w_ukv_ref[...],
                  gk_ref, cs_ref[...], k_ref.at[pl.ds(bi, 1)], v_ref.at[pl.ds(bi, 1)],
                  slice(None))


def _ctxkv(ctx, mod3, norm_g, w_in_r, kvlg, w_ukv_b, gk, cs_ctx, *, bb):
    b, lc, d = ctx.shape
    const2 = lambda bi: (0, 0)
    nkv = C_GA - C_CKV
    assert C_CKV % nkv == 0
    return pl.pallas_call(
        _ctxkv_kernel,
        grid=(b // bb,),
        in_specs=[
            pl.BlockSpec((bb, lc, d), lambda bi: (bi, 0, 0)),
            pl.BlockSpec((1, 1, 3 * d), lambda bi: (b, 0, 0)),
            pl.BlockSpec((1, d), const2),
            pl.BlockSpec((d, nkv), lambda bi: (0, C_CKV // nkv)),
            pl.BlockSpec((1, KV_LORA_RANK), const2),
            pl.BlockSpec((KV_LORA_RANK, MLA_HEADS * HEAD_COLS), const2),
            pl.BlockSpec((1, 2 * LANES), const2),
            pl.BlockSpec((lc, LANES), const2),
        ],
        out_specs=[
            pl.BlockSpec((bb, MLA_HEADS, lc, QK_HEAD_DIM), lambda bi: (bi, 0, 0, 0)),
            pl.BlockSpec((bb, MLA_HEADS, lc, V_HEAD_DIM), lambda bi: (bi, 0, 0, 0)),
        ],
        out_shape=[
            jax.ShapeDtypeStruct((b, MLA_HEADS, lc, QK_HEAD_DIM), BF16),
            jax.ShapeDtypeStruct((b, MLA_HEADS, lc, V_HEAD_DIM), BF16),
        ],
        compiler_params=pltpu.CompilerParams(
            dimension_semantics=("parallel",), vmem_limit_bytes=VMEM_LIMIT),
        name="ctxkv",
    )(ctx, mod3, norm_g, w_in_r, kvlg, w_ukv_b, gk, cs_ctx)


def _lane_tile(a, n):
    return a if n == 1 else jnp.concatenate([a] * n, axis=1)


def _attn_out_kernel(q_ref, k_ref, v_ref, kc_ref, vc_ref, sga_ref, brp_ref, x_ref, mod_ref,
                     w_ref, o_ref, attn_scr, *stat_scr, tk, running_max):
    nb, ib, dk = q_ref.shape[2:]
    rows = nb * ib
    nk = k_ref.shape[2] // tk
    dv = V_HEAD_DIM
    d = x_ref.shape[-1]

    def keys(h, c):
        return kc_ref[0, h] if c == 0 else k_ref[0, h, (c - 1) * tk:c * tk, :]

    def values(h, c):
        return vc_ref[0, h] if c == 0 else v_ref[0, h, (c - 1) * tk:c * tk, :]

    def scores(h, c):
        q = q_ref[0, h].reshape(rows, dk)
        return lax.dot_general(q, keys(h, c), (((1,), (1,)), ((), ())),
                               preferred_element_type=F32)

    gate = mod_ref[0, :, 2 * d:3 * d]
    o_ref[0] = x_ref[0] + gate * jnp.dot(brp_ref[0], w_ref[0, D_ATTN:, :].astype(BF16),
                                         preferred_element_type=F32)

    def lane_tile_sum(p):
        out = p[:, 0:LANES]
        for j in range(1, p.shape[1] // LANES):
            out = out + p[:, j * LANES:(j + 1) * LANES]
        return out

    if running_max:
        m_scr, l_scr, acc_scr = stat_scr

        def accumulate(h, s, vb):
            m_prev = m_scr[h]
            m_next = jnp.maximum(m_prev, jnp.max(s, axis=-1, keepdims=True))
            alpha = jnp.exp2(m_prev - m_next)
            p = jnp.exp2(s - _lane_tile(m_next, s.shape[1] // LANES))
            pv = jnp.dot(p.astype(BF16), vb, preferred_element_type=F32)
            l_scr[h] = alpha * l_scr[h] + lane_tile_sum(p)
            acc_scr[h] = alpha * acc_scr[h] + pv
            m_scr[h] = m_next

        m_scr[...] = jnp.full(m_scr.shape, -jnp.inf, F32)
        l_scr[...] = jnp.zeros(l_scr.shape, F32)
        acc_scr[...] = jnp.zeros(acc_scr.shape, F32)
        for c in range(nk + 1):
            for h in range(MLA_HEADS):
                accumulate(h, scores(h, c), values(h, c))
        for h in range(MLA_HEADS):
            attn_scr[h] = acc_scr[h] / jnp.sum(l_scr[h], axis=-1, keepdims=True)
    else:
        acc = [None] * MLA_HEADS
        lsum = [None] * MLA_HEADS
        for c in range(nk + 1):
            for h in range(MLA_HEADS):
                p = jnp.exp2(scores(h, c))
                pv = jnp.dot(p.astype(BF16), values(h, c), preferred_element_type=F32)
                acc[h] = pv if acc[h] is None else acc[h] + pv
                lsum[h] = lane_tile_sum(p) if lsum[h] is None else lsum[h] + lane_tile_sum(p)
        for h in range(MLA_HEADS):
            attn_scr[h] = acc[h] / jnp.sum(lsum[h], axis=-1, keepdims=True)
    pair = 2 * dv
    for h0 in range(0, MLA_HEADS, 2):
        attn = jnp.concatenate(
            [jnp.concatenate([attn_scr[h, pl.ds(il, nb, stride=ib), :] for il in range(ib)],
                             axis=0) for h in (h0, h0 + 1)], axis=1)
        cols = slice(h0 * dv, h0 * dv + pair)
        br_a = (sga_ref[0, :, cols].astype(F32) * attn).astype(BF16)
        o_ref[0] = o_ref[0] + gate * jnp.dot(br_a, w_ref[0, cols, :].astype(BF16),
                                             preferred_element_type=F32)


def _attn_out(q5, k, v, kc, vc, sga, brp, x, mod3, w_out_b, *, ib, tk, running_max):
    b, h, nb, _, dk = q5.shape
    _, s, d = x.shape
    lc = kc.shape[2]
    dva = v.shape[-1]
    rows = nb * ib
    assert s % tk == 0
    kern = functools.partial(_attn_out_kernel, tk=tk, running_max=running_max)
    stat_scr = [pltpu.VMEM((h, rows, LANES), F32)] * 2 + [pltpu.VMEM((h, rows, dva), F32)]
    resident = dict(pipeline_mode=pl.Buffered(1))
    latent = resident if running_max else {}
    row_tile = lambda bi, ti: (bi, ti, 0)
    return pl.pallas_call(
        kern,
        grid=(b, Q_BLOCK // ib),
        in_specs=[
            pl.BlockSpec((1, h, nb, ib, dk), lambda bi, ti: (bi, 0, 0, ti, 0)),
            pl.BlockSpec((1, h, s, dk), lambda bi, ti: (bi, 0, 0, 0), **latent),
            pl.BlockSpec((1, h, s, dva), lambda bi, ti: (bi, 0, 0, 0), **latent),
            pl.BlockSpec((1, h, lc, dk), lambda bi, ti: (bi, 0, 0, 0), **resident),
            pl.BlockSpec((1, h, lc, dva), lambda bi, ti: (bi, 0, 0, 0), **resident),
            pl.BlockSpec((1, rows, D_ATTN), row_tile),
            pl.BlockSpec((1, rows, D_POOL), row_tile),
            pl.BlockSpec((1, rows, d), row_tile),
            pl.BlockSpec((1, 1, 3 * d), lambda bi, ti: (bi, 0, 0)),
            pl.BlockSpec((1, D_ATTN + D_POOL, d), lambda bi, ti: (0, 0, 0), **resident),
        ],
        out_specs=pl.BlockSpec((1, rows, d), row_tile),
        out_shape=jax.ShapeDtypeStruct((b, s, d), F32),
        scratch_shapes=[pltpu.VMEM((h, rows, V_HEAD_DIM), F32)] + (stat_scr if running_max else []),
        compiler_params=pltpu.CompilerParams(
            dimension_semantics=("parallel", "arbitrary"),
            vmem_limit_bytes=VMEM_LIMIT),
        name="attn_out_rmax" if running_max else "attn_out",
    )(q5, k, v, kc, vc, sga, brp, x, mod3, w_out_b)


def _rotate_half_cols(w, signed):
    q = QK_ROPE_DIM // 4
    a, b, c, d = (w[..., i * q:(i + 1) * q] for i in range(4))
    if signed:
        return jnp.concatenate([-b, a, -d, c], axis=-1)
    return jnp.concatenate([b, a, d, c], axis=-1)


def _rope_table(seq):
    rows = seq // GRID_W
    row = np.repeat(np.arange(rows, dtype=np.float32), GRID_W)
    col = np.tile(np.arange(GRID_W, dtype=np.float32), rows)
    n_freq = QK_ROPE_DIM // 4
    inv = np.float32(ROPE_BASE) ** (-np.arange(n_freq, dtype=np.float32) / np.float32(n_freq))
    ang_r = row[:, None] * inv
    ang_c = col[:, None] * inv
    ang = np.concatenate([ang_r, ang_r, ang_c, ang_c], axis=-1).astype(np.float64)
    table = np.concatenate([np.cos(ang), np.sin(ang)], axis=-1)
    return jnp.asarray(table, dtype=F32)


def kernel(x, c, ctx, c_ctx, w_mod, b_mod, norm_g, w_in, q_lora_g, w_uq, kv_lora_g, w_ukv,
           q_norm_g, k_norm_g, w_pool, pool_scale, w_out):
    b, s, d = x.shape
    lc = ctx.shape[1]
    depth = w_mod.shape[0]
    assert depth == 1, "single-layer block"
    assert b + 1 <= MOD_ROWS and s % PROJ_TM == 0 and s % ATTN_TK == 0
    assert s % Q_BLOCK == 0 and Q_BLOCK % ATTN_IB == 0 and s % GRID_W == 0
    l = 0

    cc = jnp.concatenate([c, c_ctx[None, :], jnp.zeros((MOD_ROWS - b - 1, d), F32)], axis=0)
    mod = _modulation(cc, w_mod, b_mod[l][None, :], l)
    mod3 = mod[:, None, :]

    wi = w_in.astype(BF16)[l]
    o_cq, o_ckv, o_kr = 0, Q_LORA_RANK, Q_LORA_RANK + KV_LORA_RANK
    o_ga = o_kr + QK_ROPE_DIM
    o_pin = o_ga + D_ATTN
    o_gp = o_pin + D_POOL
    w_kr = wi[:, o_kr:o_ga]
    w_in_r = jnp.concatenate(
        [wi[:, o_cq:o_kr], w_kr, _rotate_half_cols(w_kr, True), wi[:, o_ga:o_pin],
         wi[:, o_gp:], wi[:, o_pin:o_gp]], axis=1)
    wq = w_uq.astype(BF16)[l].reshape(Q_LORA_RANK, MLA_HEADS, QK_HEAD_DIM)
    wq_rope = wq[:, :, QK_NOPE_DIM:]
    w_uq_r = jnp.concatenate([wq, _rotate_half_cols(wq_rope, True)], axis=-1)
    w_uq_r = w_uq_r.reshape(Q_LORA_RANK, MLA_HEADS * HEAD_COLS)
    w_ukv_b = w_ukv.astype(BF16)[l]
    w_pool_b = w_pool.astype(BF16)[l]
    w_out_b = w_out

    bound = (Q_SCALE * QK_HEAD_DIM) * jnp.max(jnp.abs(q_norm_g[l])) * jnp.max(jnp.abs(k_norm_g[l]))

    def gain_row(g, const):
        return (jnp.concatenate([g, _rotate_half_cols(g[QK_NOPE_DIM:], False)]) * const)[None, :]

    gq = gain_row(q_norm_g[l], Q_SCALE * math.sqrt(QK_HEAD_DIM))
    gk = gain_row(k_norm_g[l], math.sqrt(QK_HEAD_DIM))
    cs = _rope_table(s)
    cs_ctx = jnp.concatenate([jnp.ones((lc, QK_ROPE_DIM), F32),
                              jnp.zeros((lc, QK_ROPE_DIM), F32)], axis=1)

    ng = norm_g[l][None, :]
    qlg = q_lora_g[l][None, :]
    kvlg = kv_lora_g[l][None, :]
    ps = pool_scale[l][None, :]

    q, k, v, sga, brp = _proj(x, mod3, ng, w_in_r, qlg, w_uq_r, kvlg, w_ukv_b, gq, gk, cs,
                              w_pool_b, ps, tm=PROJ_TM, nsub=PROJ_NSUB)
    kc, vc = _ctxkv(ctx, mod3, ng, w_in_r, kvlg, w_ukv_b, gk, cs_ctx,
                    bb=CTX_BB if b % CTX_BB == 0 else 1)
    args = (q, k, v, kc, vc, sga, brp, x, mod3, w_out_b)
    attn_out = functools.partial(_attn_out, ib=ATTN_IB, tk=ATTN_TK)
    return lax.cond(bound < MAX_PLAIN_SCORE,
                    lambda a: attn_out(*a, running_max=False),
                    lambda a: attn_out(*a, running_max=True), args)
```
